```python
import math
import jax, jax.numpy as jnp
from jax import lax
import numpy as np

D_MODEL = 1024
BATCH = 2
SEQ = 8192
DEPTH = 2
DEC_BATCH = 32
DEC_SEQ = 64
PAST_LEN = 1024

CHUNK = 64
Q_BLOCK = 128
ROPE_THETA = 10000.0
NORM_EPS = 1e-6
NEG_INF = -1e30

A_HEADS = 4
A_HD = 64
B_HEADS = 4
B_HD = 64
IDX_HEADS = 4
IDX_DIM = 64
DSA_TOPK = 256
C_HEADS = 4
C_NOPE = 64
C_ROPE = 32
C_V = 64
Q_LORA = 256
KV_LORA = 128
N_EXPERTS = 32
TOP_K = 4
D_FF = D_MODEL
SWIGLU_LIMIT = 7.0
SWIGLU_ALPHA = 1.702
EXPERT_BLOCK = 128

A_WIDTH = A_HEADS * 2 * A_HD
B_WIDTH = B_HEADS * B_HD
C_WIDTH = C_HEADS * C_V
IN_WIDTHS = (A_WIDTH, A_WIDTH, A_WIDTH, B_WIDTH, B_WIDTH, B_WIDTH, IDX_HEADS * IDX_DIM, IDX_DIM, IDX_HEADS,
             Q_LORA, KV_LORA, C_ROPE, D_MODEL, D_MODEL, D_MODEL)
IN_DIM = 3 * A_WIDTH + 3 * B_WIDTH + IDX_HEADS * IDX_DIM + IDX_DIM + IDX_HEADS + Q_LORA + KV_LORA + C_ROPE + 3 * D_MODEL

kernel_name = 'hybrid_chunk_causal_diff_dsa_mla_moe_step'


def rms_norm(x, g):
    xf = x.astype(jnp.float32)
    xf = xf * lax.rsqrt(jnp.mean(xf * xf, axis=-1, keepdims=True) + NORM_EPS)
    return xf.astype(x.dtype) * g


def rope(x, pos):
    half = x.shape[-1] // 2
    inv = ROPE_THETA ** (-jnp.arange(half, dtype=jnp.float32) / half)
    ang = pos.astype(jnp.float32)[:, None] * inv[None, :]
    cos = jnp.cos(ang)[None, :, None, :]
    sin = jnp.sin(ang)[None, :, None, :]
    xf = x.astype(jnp.float32)
    x1, x2 = xf[..., :half], xf[..., half:]
    return jnp.concatenate([x1 * cos - x2 * sin, x2 * cos + x1 * sin], axis=-1).astype(x.dtype)


def chunk_mask(q_pos, k_pos):
    return (k_pos // CHUNK)[None, :] <= (q_pos // CHUNK)[:, None]


def split_cols(x, widths):
    outs, start = [], 0
    for w in widths:
        outs.append(x[..., start:start + w])
        start += w
    return outs


def over_query_blocks(fn, q_pos, *q_arrays):
    t = q_pos.shape[0]
    if t <= Q_BLOCK:
        return fn(q_pos, *q_arrays)
    nb = t // Q_BLOCK
    pos_b = q_pos.reshape(nb, Q_BLOCK)
    arr_b = tuple(a.reshape(a.shape[0], nb, Q_BLOCK, *a.shape[2:]).swapaxes(0, 1) for a in q_arrays)
    out = lax.map(lambda args: fn(*args), (pos_b,) + arr_b)
    out = out.swapaxes(0, 1)
    return out.reshape(out.shape[0], t, *out.shape[3:])


def diff_attention(q, k, v, q_pos, lam, subln, lam_init):
    k_pos = jnp.arange(k.shape[1], dtype=jnp.int32)

    def block(qp, qb):
        s = jnp.einsum('bqhcd,bkhcd->bhcqk', qb, k, preferred_element_type=jnp.float32) * (A_HD ** -0.5)
        s = jnp.where(chunk_mask(qp, k_pos), s, NEG_INF)
        p = jax.nn.softmax(s, axis=-1)
        a = p[:, :, 0] - lam * p[:, :, 1]
        o = jnp.einsum('bhqk,bkhe->bqhe', a, v, preferred_element_type=jnp.float32)
        o = o * lax.rsqrt(jnp.mean(o * o, axis=-1, keepdims=True) + NORM_EPS)
        o = o * subln.astype(jnp.float32) * (1.0 - lam_init)
        return o.reshape(o.shape[0], o.shape[1], A_WIDTH).astype(v.dtype)

    return over_query_blocks(block, q_pos, q)


def dsa_attention(q, qi, wi, k, v, ki, q_pos):
    n_keys = k.shape[1]
    k_sel = min(DSA_TOPK, n_keys // 4)
    k_chunk = jnp.arange(n_keys, dtype=jnp.int32) // CHUNK
    gather_rows = jax.vmap(lambda rows, idx: rows[idx])

    def block(qp, qb, qib, wib):
        q_chunk = qp // CHUNK
        rel = jax.nn.relu(jnp.einsum('bqhd,bkd->bqhk', qib, ki, preferred_element_type=jnp.float32) * (IDX_DIM ** -0.5))
        score = jnp.einsum('bqh,bqhk->bqk', wib.astype(jnp.float32) * (IDX_HEADS ** -0.5), rel)
        score = jnp.where(k_chunk[None, None, :] <= q_chunk[None, :, None], score, NEG_INF)
        _, idx = lax.top_k(score, k_sel)
        valid = (idx // CHUNK) <= q_chunk[None, :, None]
        ks = gather_rows(k, idx)
        vs = gather_rows(v, idx)
        s = jnp.einsum('bqhd,bqshd->bhqs', qb, ks, preferred_element_type=jnp.float32) * (B_HD ** -0.5)
        s = jnp.where(valid[:, None], s, NEG_INF)
        p = jax.nn.softmax(s, axis=-1)
        o = jnp.einsum('bhqs,bqshd->bqhd', p, vs, preferred_element_type=jnp.float32)
        return o.reshape(o.shape[0], o.shape[1], B_WIDTH).astype(v.dtype)

    return over_query_blocks(block, q_pos, q, qi, wi)


def mla_attention(q, k, v, q_pos):
    k_pos = jnp.arange(k.shape[1], dtype=jnp.int32)

    def block(qp, qb):
        s = jnp.einsum('bqhd,bkhd->bhqk', qb, k, preferred_element_type=jnp.float32) * ((C_NOPE + C_ROPE) ** -0.5)
        s = jnp.where(chunk_mask(qp, k_pos), s, NEG_INF)
        p = jax.nn.softmax(s, axis=-1)
        o = jnp.einsum('bhqk,bkhd->bqhd', p, v, preferred_element_type=jnp.float32)
        return o.reshape(o.shape[0], o.shape[1], C_WIDTH).astype(v.dtype)

    return over_query_blocks(block, q_pos, q)


def moe_ffn(h, router_w, router_b, w_gu, b_gu, w_dn, b_dn):
    n_tok, d = h.shape
    logits = jnp.matmul(h, router_w, preferred_element_type=jnp.float32) + router_b.astype(jnp.float32)
    top_logit, top_e = lax.top_k(logits, TOP_K)
    gate = jax.nn.softmax(top_logit, axis=-1)
    nk = n_tok * TOP_K
    flat_e = top_e.reshape(nk)
    order = jnp.argsort(flat_e).astype(jnp.int32)
    e_sorted = flat_e[order]
    counts = jnp.zeros((N_EXPERTS,), jnp.int32).at[flat_e].add(1)
    padded = (counts + EXPERT_BLOCK - 1) // EXPERT_BLOCK * EXPERT_BLOCK
    pad_end = jnp.cumsum(padded)
    pad_start = pad_end - padded
    start = jnp.cumsum(counts) - counts
    rank = jnp.arange(nk, dtype=jnp.int32) - start[e_sorted]
    dest = pad_start[e_sorted] + rank
    n_blocks = (nk + N_EXPERTS * (EXPERT_BLOCK - 1) + EXPERT_BLOCK - 1) // EXPERT_BLOCK
    n_rows = n_blocks * EXPERT_BLOCK
    row_tok = jnp.full((n_rows,), n_tok, jnp.int32).at[dest].set(order // TOP_K)
    row_gate = jnp.zeros((n_rows,), h.dtype).at[dest].set(gate.reshape(nk)[order].astype(h.dtype))
    blk_start = jnp.arange(n_blocks, dtype=jnp.int32) * EXPERT_BLOCK
    blk_e = jnp.minimum(jnp.searchsorted(pad_end, blk_start, side='right'), N_EXPERTS - 1)
    h_pad = jnp.concatenate([h, jnp.zeros((1, d), h.dtype)], axis=0)
    xr = h_pad[row_tok].reshape(n_blocks, EXPERT_BLOCK, d)

    def expert_block(args):
        xb, e = args
        gu = xb @ w_gu[e] + b_gu[e]
        glu = jnp.minimum(gu[:, 0::2], SWIGLU_LIMIT)
        lin = jnp.clip(gu[:, 1::2], -SWIGLU_LIMIT, SWIGLU_LIMIT)
        act = glu * jax.nn.sigmoid(SWIGLU_ALPHA * glu) * (lin + 1.0)
        return act @ w_dn[e] + b_dn[e]

    yr = lax.map(expert_block, (xr, blk_e)).reshape(n_rows, d)
    out = jnp.zeros((n_tok + 1, d), h.dtype).at[row_tok].add(yr * row_gate[:, None])
    return out[:n_tok]


def run_group(x, c, q_pos, caches, params):
    (w_ada, b_ada, norm_mix, norm_ffn, w_in, diff_lq1, diff_lk1, diff_lq2, diff_lk2, diff_subln,
     mla_q_norm, mla_w_uq, mla_kv_norm, mla_w_ukv, w_br_a, w_br_b, w_br_c, w_out,
     router_w, router_b, exp_w_gu, exp_b_gu, exp_w_down, exp_b_down, final_norm) = params
    b, t, _ = x.shape
    rows = [[], [], [], [], [], [], []]
    for l in range(DEPTH):
        lam_init = 0.8 - 0.6 * math.exp(-0.3 * l)
        mod = jax.nn.silu(c) @ w_ada[l] + b_ada[l]
        sh_a, sc_a, gt_a, sh_m, sc_m, gt_m = [m[:, None, :] for m in jnp.split(mod, 6, axis=-1)]
        h = rms_norm(x, norm_mix[l]) * (1.0 + sc_a) + sh_a
        (aq, ak, av, bq, bk, bv, biq, bik, biw, cq_lat, ckv_lat, ck_r, g_a, g_b, g_c) = split_cols(h @ w_in[l], IN_WIDTHS)
        aq = rope(aq.reshape(b, t, 2 * A_HEADS, A_HD), q_pos).reshape(b, t, A_HEADS, 2, A_HD)
        ak = rope(ak.reshape(b, t, 2 * A_HEADS, A_HD), q_pos).reshape(b, t, A_HEADS, 2 * A_HD)
        av = av.reshape(b, t, A_HEADS, 2 * A_HD)
        bq = rope(bq.reshape(b, t, B_HEADS, B_HD), q_pos)
        bk = rope(bk.reshape(b, t, B_HEADS, B_HD), q_pos)
        bv = bv.reshape(b, t, B_HEADS, B_HD)
        biq = rope(biq.reshape(b, t, IDX_HEADS, IDX_DIM), q_pos)
        bik = rope(bik[:, :, None, :], q_pos)[:, :, 0, :]
        cq = (rms_norm(cq_lat, mla_q_norm[l]) @ mla_w_uq[l]).reshape(b, t, C_HEADS, C_NOPE + C_ROPE)
        cq = jnp.concatenate([cq[..., :C_NOPE], rope(cq[..., C_NOPE:], q_pos)], axis=-1)
        c_lat = rms_norm(ckv_lat, mla_kv_norm[l])
        ck_r = rope(ck_r[:, :, None, :], q_pos)[:, :, 0, :]
        new = (ak, av, bk, bv, bik, c_lat, ck_r)
        for i in range(7):
            rows[i].append(new[i])
        if caches is None:
            full = new
        else:
            full = [jnp.concatenate([cache[l], n_rows], axis=1) for cache, n_rows in zip(caches, new)]
        fak, fav, fbk, fbv, fbik, fclat, fckr = full
        n_keys = fclat.shape[1]
        lam = (jnp.exp(jnp.sum(diff_lq1[l] * diff_lk1[l]).astype(jnp.float32))
               - jnp.exp(jnp.sum(diff_lq2[l] * diff_lk2[l]).astype(jnp.float32)) + lam_init)
        o_a = diff_attention(aq, fak.reshape(b, n_keys, A_HEADS, 2, A_HD), fav, q_pos, lam, diff_subln[l], lam_init)
        o_b = dsa_attention(bq, biq, biw, fbk, fbv, fbik, q_pos)
        kv = (fclat @ mla_w_ukv[l]).reshape(b, n_keys, C_HEADS, C_NOPE + C_V)
        ck = jnp.concatenate([kv[..., :C_NOPE], jnp.broadcast_to(fckr[:, :, None, :], (b, n_keys, C_HEADS, C_ROPE))], axis=-1)
        o_c = mla_attention(cq, ck, kv[..., C_NOPE:], q_pos)
        merged = (jax.nn.sigmoid(g_a) * (o_a @ w_br_a[l]) + jax.nn.sigmoid(g_b) * (o_b @ w_br_b[l])
                  + jax.nn.sigmoid(g_c) * (o_c @ w_br_c[l]))
        x = x + gt_a * (merged @ w_out[l])
        h2 = rms_norm(x, norm_ffn[l]) * (1.0 + sc_m) + sh_m
        ffn = moe_ffn(h2.reshape(b * t, D_MODEL), router_w[l], router_b[l], exp_w_gu[l], exp_b_gu[l],
                      exp_w_down[l], exp_b_down[l])
        x = x + gt_m * ffn.reshape(b, t, D_MODEL)
    y = rms_norm(x, final_norm)
    stacked = [jnp.stack(r, axis=0) for r in rows]
    return y, stacked


def _nrm(key, shape, scale):
    return scale * jax.random.normal(key, shape, jnp.float32)


def setup_inputs(seed: int = 0) -> dict:
    key = jax.random.key(seed)
    k = jax.random.split(key, 36)
    d = D_MODEL
    return {
        'x_prompt': _nrm(k[0], (BATCH, SEQ, d), 1.0),
        'x_sample': _nrm(k[1], (DEC_BATCH, DEC_SEQ, d), 1.0),
        'cache_a_k': _nrm(k[2], (DEPTH, DEC_BATCH, PAST_LEN, A_HEADS, 2 * A_HD), 1.0),
        'cache_a_v': _nrm(k[3], (DEPTH, DEC_BATCH, PAST_LEN, A_HEADS, 2 * A_HD), 1.0),
        'cache_b_k': _nrm(k[4], (DEPTH, DEC_BATCH, PAST_LEN, B_HEADS, B_HD), 1.0),
        'cache_b_v': _nrm(k[5], (DEPTH, DEC_BATCH, PAST_LEN, B_HEADS, B_HD), 1.0),
        'cache_b_idx_k': _nrm(k[6], (DEPTH, DEC_BATCH, PAST_LEN, IDX_DIM), 1.0),
        'cache_c_latent': _nrm(k[7], (DEPTH, DEC_BATCH, PAST_LEN, KV_LORA), 1.0),
        'cache_c_k_rope': _nrm(k[8], (DEPTH, DEC_BATCH, PAST_LEN, C_ROPE), 1.0),
        'c_prompt': _nrm(k[9], (BATCH, d), 1.0),
        'c_sample': _nrm(k[10], (DEC_BATCH, d), 1.0),
        'w_ada': _nrm(k[11], (DEPTH, d, 6 * d), 0.5 * d ** -0.5),
        'b_ada': _nrm(k[12], (DEPTH, 6 * d), 0.01),
        'norm_mix': 1.0 + _nrm(k[13], (DEPTH, d), 0.01),
        'norm_ffn': 1.0 + _nrm(k[14], (DEPTH, d), 0.01),
        'w_in': _nrm(k[15], (DEPTH, d, IN_DIM), d ** -0.5),
        'diff_lq1': _nrm(k[16], (DEPTH, A_HD), 0.1),
        'diff_lk1': _nrm(k[17], (DEPTH, A_HD), 0.1),
        'diff_lq2': _nrm(k[18], (DEPTH, A_HD), 0.1),
        'diff_lk2': _nrm(k[19], (DEPTH, A_HD), 0.1),
        'diff_subln': 1.0 + _nrm(k[20], (DEPTH, 2 * A_HD), 0.01),
        'mla_q_norm': 1.0 + _nrm(k[21], (DEPTH, Q_LORA), 0.01),
        'mla_w_uq': _nrm(k[22], (DEPTH, Q_LORA, C_HEADS * (C_NOPE + C_ROPE)), Q_LORA ** -0.5),
        'mla_kv_norm': 1.0 + _nrm(k[23], (DEPTH, KV_LORA), 0.01),
        'mla_w_ukv': _nrm(k[24], (DEPTH, KV_LORA, C_HEADS * (C_NOPE + C_V)), KV_LORA ** -0.5),
        'w_br_a': _nrm(k[25], (DEPTH, A_WIDTH, d), A_WIDTH ** -0.5),
        'w_br_b': _nrm(k[26], (DEPTH, B_WIDTH, d), B_WIDTH ** -0.5),
        'w_br_c': _nrm(k[27], (DEPTH, C_WIDTH, d), C_WIDTH ** -0.5),
        'w_out': _nrm(k[28], (DEPTH, d, d), d ** -0.5),
        'router_w': _nrm(k[29], (DEPTH, d, N_EXPERTS), d ** -0.5),
        'router_b': _nrm(k[30], (DEPTH, N_EXPERTS), 0.01),
        'exp_w_gu': _nrm(k[31], (DEPTH, N_EXPERTS, d, 2 * D_FF), d ** -0.5),
        'exp_b_gu': _nrm(k[32], (DEPTH, N_EXPERTS, 2 * D_FF), 0.01),
        'exp_w_down': _nrm(k[33], (DEPTH, N_EXPERTS, D_FF, d), D_FF ** -0.5),
        'exp_b_down': _nrm(k[34], (DEPTH, N_EXPERTS, d), 0.01),
        'final_norm': 1.0 + _nrm(k[35], (d,), 0.01),
    }


def reference(x_prompt, x_sample, cache_a_k, cache_a_v, cache_b_k, cache_b_v, cache_b_idx_k,
              cache_c_latent, cache_c_k_rope, c_prompt, c_sample, w_ada, b_ada, norm_mix, norm_ffn, w_in,
              diff_lq1, diff_lk1, diff_lq2, diff_lk2, diff_subln, mla_q_norm, mla_w_uq, mla_kv_norm, mla_w_ukv,
              w_br_a, w_br_b, w_br_c, w_out, router_w, router_b, exp_w_gu, exp_b_gu, exp_w_down, exp_b_down,
              final_norm):
    params = (w_ada, b_ada, norm_mix, norm_ffn, w_in, diff_lq1, diff_lk1, diff_lq2, diff_lk2, diff_subln,
              mla_q_norm, mla_w_uq, mla_kv_norm, mla_w_ukv, w_br_a, w_br_b, w_br_c, w_out,
              router_w, router_b, exp_w_gu, exp_b_gu, exp_w_down, exp_b_down, final_norm)
    pos_p = jnp.arange(x_prompt.shape[1], dtype=jnp.int32)
    y_prompt, rows_p = run_group(x_prompt, c_prompt, pos_p, None, params)
    past = cache_c_latent.shape[2]
    pos_s = past + jnp.arange(x_sample.shape[1], dtype=jnp.int32)
    caches = (cache_a_k, cache_a_v, cache_b_k, cache_b_v, cache_b_idx_k, cache_c_latent, cache_c_k_rope)
    y_sample, rows_s = run_group(x_sample, c_sample, pos_s, caches, params)
    a_k_p, a_v_p, b_k_p, b_v_p, b_ik_p, c_lat_p, c_kr_p = rows_p
    a_k_s, a_v_s, b_k_s, b_v_s, b_ik_s, c_lat_s, c_kr_s = rows_s
    return (y_prompt, y_sample, a_k_p, a_v_p, b_k_p, b_v_p, b_ik_p, c_lat_p, c_kr_p,
            a_k_s, a_v_s, b_k_s, b_v_s, b_ik_s, c_lat_s, c_kr_s)
```

```python
import functools
import math

import numpy as np
import jax
import jax.numpy as jnp
from jax import lax
from jax.experimental import pallas as pl
from jax.experimental.pallas import tpu as pltpu

F32 = jnp.float32
BF16 = jnp.bfloat16

D_MODEL = 1024
CHUNK = 64
ROPE_THETA = 10000.0
NORM_EPS = 1e-6
NEG_INF = -1e30
HALF_NEG = -5e29
BELOW_NEG = -3e38
BIG_POS = 3e38
LOG2E = 1.4426950408889634

A_HEADS, A_HD = 4, 64
B_HEADS, B_HD = 4, 64
IDX_HEADS, IDX_DIM = 4, 64
DSA_TOPK = 256
C_HEADS, C_NOPE, C_ROPE, C_V = 4, 64, 32, 64
Q_LORA, KV_LORA = 256, 128
N_EXPERTS, TOP_K = 32, 4
D_FF = D_MODEL
SWIGLU_LIMIT = 7.0
SWIGLU_ALPHA = 1.702

A_WIDTH = A_HEADS * 2 * A_HD
B_WIDTH = B_HEADS * B_HD
LANES = 128

_OFF_AQ, _OFF_AK, _OFF_AV = 0, 512, 1024
_OFF_BQ, _OFF_BK, _OFF_BV = 1536, 1792, 2048
_OFF_BIQ, _OFF_BIK, _OFF_BIW = 2304, 2560, 2624
_OFF_CQ, _OFF_CKV, _OFF_CKR = 2628, 2884, 3012
_OFF_GATES = 3044
_P_AQ, _P_AK, _P_AV = 0, 512, 1024
_P_BQ, _P_BK, _P_BV = 1536, 1792, 2048
_P_BIQ, _P_BIK2, _P_MISC, _P_CQ, _P_CKV, _P_END = 2304, 2560, 2688, 2816, 3072, 3200
_MISC_BIW = 32

TOKEN_BLOCK = 256
MOD_ROWS = 8
EXPERT_ROWS = 256
VMEM_LIMIT = 56 * 1024 * 1024


def _cparams(sem, vmem=VMEM_LIMIT):
    return pltpu.CompilerParams(dimension_semantics=sem, vmem_limit_bytes=vmem)


def _rms(xf, g):
    return xf * lax.rsqrt(jnp.mean(xf * xf, axis=-1, keepdims=True) + NORM_EPS) * g


def _sigmoid(x):
    return 1.0 / (1.0 + jnp.exp(-x))


def _bdot(a, b):
    return jnp.dot(a, b, preferred_element_type=F32)


def _dot_nt(a, b):
    return lax.dot_general(a, b, (((1,), (1,)), ((), ())), preferred_element_type=F32)


def _dot_split(a, b):
    a_hi = a.astype(BF16)
    b_hi = b.astype(BF16)
    a_lo = (a - a_hi.astype(F32)).astype(BF16)
    b_lo = (b - b_hi.astype(F32)).astype(BF16)
    return _bdot(a_hi, b_hi) + (_bdot(a_hi, b_lo) + _bdot(a_lo, b_hi))


def _modulate(xn, sc, sh, rows, ch):
    n = rows // ch
    y = xn.reshape(n, ch, D_MODEL) * (1.0 + sc)[:, None, :] + sh[:, None, :]
    return y.reshape(rows, D_MODEL)


def _scale_rows(y, g, rows, ch):
    n = rows // ch
    return (y.reshape(n, ch, D_MODEL) * g[:, None, :]).reshape(rows, D_MODEL)


def _ada_kernel(c_ref, w_ref, b_ref, o_ref):
    c = c_ref[...]
    s = c * _sigmoid(c)
    o_ref[0] = _dot_split(s, w_ref[0]) + b_ref[0]


def _ada_call(c_pad, w_ada, b_ada):
    depth = w_ada.shape[0]
    mp = c_pad.shape[0]
    return pl.pallas_call(
        _ada_kernel,
        grid=(depth, 6),
        in_specs=[
            pl.BlockSpec((mp, D_MODEL), lambda l, j: (0, 0)),
            pl.BlockSpec((1, D_MODEL, D_MODEL), lambda l, j: (l, 0, j)),
            pl.BlockSpec((1, 1, D_MODEL), lambda l, j: (l, 0, j)),
        ],
        out_specs=pl.BlockSpec((1, mp, D_MODEL), lambda l, j: (l, 0, j)),
        out_shape=jax.ShapeDtypeStruct((depth, mp, 6 * D_MODEL), F32),
        compiler_params=_cparams(("arbitrary", "arbitrary")),
        name="ada_mod",
    )(c_pad, w_ada, b_ada.reshape(depth, 1, 6 * D_MODEL))


def _rope_partner(x, lane, half):
    first = (lane & (2 * half - 1)) < half
    return jnp.where(first, pltpu.roll(x, LANES - half, 1), pltpu.roll(x, half, 1))


def _ka_kernel(x_ref, sh_ref, sc_ref, cos_ref, sin_ref, cosm_ref, sinm_ref, cosq_ref, sinq_ref,
               nmix_ref, w1_ref, qn_ref, wuq_ref, kvn_ref, wk_ref, we_ref, wv_ref,
               ak_o, av_o, bk_o, bv_o, bik_o, clat_o, ckr_o, misc_o,
               aq_b, ak_b, av_b, bq_b, bk_b, bv_b, biq_b, bik2_b, cq_b, ck_b, cv_b, *, tm, ch):
    x = x_ref[...]
    h = _modulate(_rms(x, nmix_ref[...]), sc_ref[...], sh_ref[...], tm, ch)
    hb = h.astype(BF16)
    lane = lax.broadcasted_iota(jnp.int32, (tm, LANES), 1)
    cos = cos_ref[...]
    sin = sin_ref[...]

    def proj(c0, c1):
        return _bdot(hb, w1_ref[:, c0:c1])

    def rope64(xb):
        return xb * cos + _rope_partner(xb, lane, 32) * sin

    a_scale = (A_HD ** -0.5) * LOG2E
    b_scale = (B_HD ** -0.5) * LOG2E
    i_scale = IDX_DIM ** -0.5
    c_scale = ((C_NOPE + C_ROPE) ** -0.5) * LOG2E

    p = proj(_P_AQ, _P_AK)
    for c in range(A_WIDTH // LANES):
        sl = slice(c * LANES, (c + 1) * LANES)
        aq_b[:, sl] = (rope64(p[:, sl]) * a_scale).astype(BF16)
    p = proj(_P_AK, _P_AV)
    for c in range(A_WIDTH // LANES):
        sl = slice(c * LANES, (c + 1) * LANES)
        r = rope64(p[:, sl])
        ak_o[:, sl] = r
        ak_b[:, sl] = r.astype(BF16)
    p = proj(_P_AV, _P_BQ)
    av_o[...] = p
    av_b[...] = p.astype(BF16)
    p = proj(_P_BQ, _P_BK)
    for c in range(B_WIDTH // LANES):
        sl = slice(c * LANES, (c + 1) * LANES)
        bq_b[:, sl] = (rope64(p[:, sl]) * b_scale).astype(BF16)
    p = proj(_P_BK, _P_BV)
    for c in range(B_WIDTH // LANES):
        sl = slice(c * LANES, (c + 1) * LANES)
        r = rope64(p[:, sl])
        bk_o[:, sl] = r
        bk_b[:, sl] = r.astype(BF16)
    p = proj(_P_BV, _P_BIQ)
    bv_o[...] = p
    bv_b[...] = p.astype(BF16)
    p = proj(_P_BIQ, _P_BIK2)
    for c in range(B_WIDTH // LANES):
        sl = slice(c * LANES, (c + 1) * LANES)
        biq_b[:, sl] = (rope64(p[:, sl]) * i_scale).astype(BF16)
    r = rope64(proj(_P_BIK2, _P_MISC))
    bik_o[...] = r[:, :IDX_DIM]
    bik2_b[...] = r.astype(BF16)
    pm = proj(_P_MISC, _P_CQ)
    misc = pm * cosm_ref[...] + _rope_partner(pm, lane, 16) * sinm_ref[...]
    misc_o[...] = misc
    ckr_o[...] = misc[:, :C_ROPE]
    qlat = _rms(proj(_P_CQ, _P_CKV), qn_ref[...]).astype(BF16)
    cqf = _bdot(qlat, wuq_ref[...])
    cosq = cosq_ref[...]
    sinq = sinq_ref[...]
    for c in range(C_HEADS):
        sl = slice(c * LANES, (c + 1) * LANES)
        xb = cqf[:, sl]
        cq_b[:, sl] = ((xb * cosq + _rope_partner(xb, lane, 16) * sinq) * c_scale).astype(BF16)
    clat = _rms(proj(_P_CKV, _P_END), kvn_ref[...])
    clat_o[...] = clat
    clb = clat.astype(BF16)
    ck_b[...] = (_bdot(clb, wk_ref[...]) + _bdot(misc.astype(BF16), we_ref[...])).astype(BF16)
    cv_b[...] = _bdot(clb, wv_ref[...]).astype(BF16)


def _ka_call(x, sh, sc, tabs, nmix, w1, qn, wuq, kvn, wk, we, wv, *, n_prompt_blocks, tab_blocks):
    n = x.shape[0]
    tm = TOKEN_BLOCK
    ch = tm // MOD_ROWS
    nblk = n // tm

    def row(i):
        return (i, 0)

    def tab(i):
        return (jnp.where(i < n_prompt_blocks, i % tab_blocks, tab_blocks), 0)

    def const(i):
        return (0, 0)

    def full(a):
        return pl.BlockSpec(a.shape, const)

    widths_f32 = (A_WIDTH, A_WIDTH, B_WIDTH, B_WIDTH, IDX_DIM, KV_LORA, C_ROPE, LANES)
    widths_b16 = (A_WIDTH, A_WIDTH, A_WIDTH, B_WIDTH, B_WIDTH, B_WIDTH, B_WIDTH, LANES, 512, 512, 512)
    out_shape = ([jax.ShapeDtypeStruct((n, w), F32) for w in widths_f32]
                 + [jax.ShapeDtypeStruct((n, w), BF16) for w in widths_b16])
    out_specs = [pl.BlockSpec((tm, w), row) for w in widths_f32 + widths_b16]
    in_specs = ([pl.BlockSpec((tm, D_MODEL), row),
                 pl.BlockSpec((MOD_ROWS, D_MODEL), row),
                 pl.BlockSpec((MOD_ROWS, D_MODEL), row)]
                + [pl.BlockSpec((tm, LANES), tab) for _ in range(6)]
                + [full(a) for a in (nmix, w1, qn, wuq, kvn, wk, we, wv)])
    return pl.pallas_call(
        functools.partial(_ka_kernel, tm=tm, ch=ch),
        grid=(nblk,),
        in_specs=in_specs,
        out_specs=out_specs,
        out_shape=out_shape,
        compiler_params=_cparams(("arbitrary",)),
        name="pre_attention",
    )(x, sh, sc, *tabs, nmix, w1, qn, wuq, kvn, wk, we, wv)


def _mla_kv_kernel(lat_ref, kr_ref, wk_ref, we_ref, wv_ref, ck_o, cv_o):
    lb = lat_ref[...].astype(BF16)
    ck_o[...] = (_bdot(lb, wk_ref[...]) + _bdot(kr_ref[...].astype(BF16), we_ref[...])).astype(BF16)
    cv_o[...] = _bdot(lb, wv_ref[...]).astype(BF16)


def _mla_kv_call(lat, krp, wk, we, wv):
    n = lat.shape[0]
    tm = 512
    return pl.pallas_call(
        _mla_kv_kernel,
        grid=(n // tm,),
        in_specs=[pl.BlockSpec((tm, KV_LORA), lambda i: (i, 0)),
                  pl.BlockSpec((tm, LANES), lambda i: (i, 0)),
                  pl.BlockSpec(wk.shape, lambda i: (0, 0)),
                  pl.BlockSpec(we.shape, lambda i: (0, 0)),
                  pl.BlockSpec(wv.shape, lambda i: (0, 0))],
        out_specs=[pl.BlockSpec((tm, 512), lambda i: (i, 0)), pl.BlockSpec((tm, 512), lambda i: (i, 0))],
        out_shape=[jax.ShapeDtypeStruct((n, 512), BF16), jax.ShapeDtypeStruct((n, 512), BF16)],
        compiler_params=_cparams(("arbitrary",)),
        name="mla_cache_kv",
    )(lat, krp, wk, we, wv)


def _block_range(r0, tq, tk, nkb):
    n_full = jnp.minimum((r0 + CHUNK) // tk, nkb)
    n_vis = jnp.minimum((r0 + tq + tk - 1) // tk, nkb)
    return n_full, n_vis


def _visible(r0, ks, rows, tk):
    rpos = r0 + lax.broadcasted_iota(jnp.int32, (rows, tk), 0)
    kpos = ks + lax.broadcasted_iota(jnp.int32, (rows, tk), 1)
    return kpos < (((rpos >> 6) + 1) << 6)


def _online_update(s, vblk, m_sc, l_sc, acc_sc, rows, sel=None):
    m_prev = m_sc[rows, :]
    m_new = jnp.maximum(m_prev, jnp.max(s, axis=1, keepdims=True))
    p = jnp.exp2(s - m_new)
    if sel is not None:
        p = jnp.where(sel, p, 0.0)
    alpha = jnp.exp2(m_prev - m_new)
    l_sc[rows, :] = alpha * l_sc[rows, :] + jnp.sum(p, axis=1, keepdims=True)
    acc_sc[rows, :] = alpha * acc_sc[rows, :] + _bdot(p.astype(BF16), vblk)
    m_sc[rows, :] = m_new


def _split_halves(q, lane=None):
    lane1 = lax.broadcasted_iota(jnp.int32, (1, LANES), 1)
    lo = jnp.where(lane1 < 64, 1.0, 0.0).astype(q.dtype)
    return q * lo, q * (1.0 - lo).astype(q.dtype)


def _diff_kernel(lqk_ref, sub_ref, q_ref, k_ref, v_ref, o_ref, m_sc, l_sc, acc_sc,
                 *, tq, tk, nkb, q_pos0, lam_init):
    i = pl.program_id(2)
    r0 = q_pos0 + i * tq
    n_full, n_vis = _block_range(r0, tq, tk, nkb)
    lane = lax.broadcasted_iota(jnp.int32, (tq, LANES), 1)
    q1, q2 = _split_halves(q_ref[...], lane)
    qq = jnp.concatenate([q1, q2], axis=0)
    m_sc[...] = jnp.full(m_sc.shape, NEG_INF, F32)
    l_sc[...] = jnp.zeros(l_sc.shape, F32)
    acc_sc[...] = jnp.zeros(acc_sc.shape, F32)
    rows = slice(0, 2 * tq)

    def step(kb, masked):
        ks = pl.multiple_of(kb * tk, tk)
        kblk = k_ref[pl.ds(ks, tk), :]
        vblk = v_ref[pl.ds(ks, tk), :]
        s = _dot_nt(qq, kblk)
        if masked:
            vis = _visible(r0, ks, tq, tk)
            s = jnp.where(jnp.concatenate([vis, vis], axis=0), s, NEG_INF)
        _online_update(s, vblk, m_sc, l_sc, acc_sc, rows)

    def full_body(kb, c):
        step(kb, False)
        return c

    def masked_body(kb, c):
        step(kb, True)
        return c

    lax.fori_loop(0, n_full, full_body, 0)
    lax.fori_loop(n_full, n_vis, masked_body, 0)

    acc = acc_sc[...]
    l = l_sc[...]
    o1 = acc[:tq] / l[:tq]
    o2 = acc[tq:] / l[tq:]
    lq = lqk_ref[...]
    lam = (jnp.exp(jnp.sum(lq[0:1] * lq[1:2], axis=1, keepdims=True))
           - jnp.exp(jnp.sum(lq[2:3] * lq[3:4], axis=1, keepdims=True)) + lam_init)
    o = o1 - lam * o2
    o = o * lax.rsqrt(jnp.mean(o * o, axis=1, keepdims=True) + NORM_EPS)
    o_ref[...] = (o * sub_ref[...] * (1.0 - lam_init)).astype(BF16)


def _diff_call(lqk, sub, q, k, v, *, nb, nq, tq, tk, lp, q_blk0, k_blk0, q_pos0, lam_init):
    nkb = lp // tk
    kern = functools.partial(_diff_kernel, tq=tq, tk=tk, nkb=nkb, q_pos0=q_pos0, lam_init=lam_init)
    return pl.pallas_call(
        kern,
        grid=(nb, A_HEADS, nq),
        in_specs=[pl.BlockSpec(lqk.shape, lambda b, h, i: (0, 0)),
                  pl.BlockSpec((1, LANES), lambda b, h, i: (0, 0)),
                  pl.BlockSpec((tq, LANES), lambda b, h, i: (q_blk0 + b * nq + i, h)),
                  pl.BlockSpec((lp, LANES), lambda b, h, i: (k_blk0 + b, h)),
                  pl.BlockSpec((lp, LANES), lambda b, h, i: (k_blk0 + b, h))],
        out_specs=pl.BlockSpec((tq, LANES), lambda b, h, i: (b * nq + i, h)),
        out_shape=jax.ShapeDtypeStruct((nb * nq * tq, A_WIDTH), BF16),
        scratch_shapes=[pltpu.VMEM((2 * tq, 1), F32), pltpu.VMEM((2 * tq, 1), F32),
                        pltpu.VMEM((2 * tq, LANES), F32)],
        compiler_params=_cparams(("arbitrary", "arbitrary", "arbitrary")),
        name="diff_attention",
    )(lqk, sub, q, k, v)


def _mla_kernel(q_ref, k_ref, v_ref, o_ref, m_sc, l_sc, acc_sc, *, tq, tk, nkb, q_pos0):
    i = pl.program_id(2)
    r0 = q_pos0 + i * tq
    n_full, n_vis = _block_range(r0, tq, tk, nkb)
    q = q_ref[...]
    m_sc[...] = jnp.full(m_sc.shape, NEG_INF, F32)
    l_sc[...] = jnp.zeros(l_sc.shape, F32)
    acc_sc[...] = jnp.zeros(acc_sc.shape, F32)
    rows = slice(0, tq)

    def step(kb, masked):
        ks = pl.multiple_of(kb * tk, tk)
        s = _dot_nt(q, k_ref[pl.ds(ks, tk), :])
        if masked:
            s = jnp.where(_visible(r0, ks, tq, tk), s, NEG_INF)
        _online_update(s, v_ref[pl.ds(ks, tk), :], m_sc, l_sc, acc_sc, rows)

    def full_body(kb, c):
        step(kb, False)
        return c

    def masked_body(kb, c):
        step(kb, True)
        return c

    lax.fori_loop(0, n_full, full_body, 0)
    lax.fori_loop(n_full, n_vis, masked_body, 0)
    o_ref[...] = (acc_sc[...] / l_sc[...]).astype(BF16)


def _mla_call(q, k, v, *, nb, nq, tq, tk, lp, q_blk0, k_blk0, q_pos0):
    nkb = lp // tk
    kern = functools.partial(_mla_kernel, tq=tq, tk=tk, nkb=nkb, q_pos0=q_pos0)
    return pl.pallas_call(
        kern,
        grid=(nb, C_HEADS, nq),
        in_specs=[pl.BlockSpec((tq, LANES), lambda b, h, i: (q_blk0 + b * nq + i, h)),
                  pl.BlockSpec((lp, LANES), lambda b, h, i: (k_blk0 + b, h)),
                  pl.BlockSpec((lp, LANES), lambda b, h, i: (k_blk0 + b, h))],
        out_specs=pl.BlockSpec((tq, LANES), lambda b, h, i: (b * nq + i, h)),
        out_shape=jax.ShapeDtypeStruct((nb * nq * tq, C_HEADS * LANES), BF16),
        scratch_shapes=[pltpu.VMEM((tq, 1), F32), pltpu.VMEM((tq, 1), F32), pltpu.VMEM((tq, LANES), F32)],
        compiler_params=_cparams(("arbitrary", "arbitrary", "arbitrary")),
        name="mla_attention",
    )(q, k, v)


def _dsa_kernel(q_ref, qi_ref, w_ref, k_ref, v_ref, ki_ref, o_ref,
                s_sc, m_sc, l_sc, acc_sc, t_sc, need_sc, carry_sc,
                *, tq, tk, nkb, q_pos0, n_keys, k_sel, rs, n_bisect):
    i = pl.program_id(1)
    r0 = q_pos0 + i * tq
    n_full, n_vis = _block_range(r0, tq, tk, nkb)
    lane = lax.broadcasted_iota(jnp.int32, (tq, LANES), 1)
    ksel_f = float(k_sel)

    qi = qi_ref[...]
    qa, qb = _split_halves(qi[:, :LANES], lane)
    qc, qd = _split_halves(qi[:, LANES:], lane)
    qi4 = jnp.concatenate([qa, qb, qc, qd], axis=0)
    wm = w_ref[...]
    wh = [wm[:, _MISC_BIW + h:_MISC_BIW + h + 1] * (IDX_HEADS ** -0.5) for h in range(IDX_HEADS)]

    def score_step(kb, masked):
        ks = pl.multiple_of(kb * tk, tk)
        rel = jnp.maximum(_dot_nt(qi4, ki_ref[pl.ds(ks, tk), :]), 0.0)
        sc = wh[0] * rel[:tq]
        for h in range(1, IDX_HEADS):
            sc = sc + wh[h] * rel[h * tq:(h + 1) * tq]
        if masked:
            sc = jnp.where(_visible(r0, ks, tq, tk), sc, NEG_INF)
        s_sc[kb] = sc

    def score_full(kb, c):
        score_step(kb, False)
        return c

    def score_masked(kb, c):
        score_step(kb, True)
        return c

    lax.fori_loop(0, n_full, score_full, 0)
    lax.fori_loop(n_full, n_vis, score_masked, 0)

    nl = tk // LANES

    def search(sb, flag):
        rsl = pl.ds(pl.multiple_of(sb * rs, rs), rs)

        def fold(fn, init):
            def body(kb, part):
                s = s_sc[kb, rsl, :]
                for j in range(nl):
                    part = fn(part, s[:, j * LANES:(j + 1) * LANES])
                return part
            return lax.fori_loop(0, n_vis, body, init)

        def bc(x):
            return jnp.broadcast_to(x, (rs, LANES))

        def count(cmp, x):
            xb = bc(x)
            part = fold(lambda p, sj: p + jnp.where(cmp(sj, xb), 1.0, 0.0), jnp.zeros((rs, LANES), F32))
            return jnp.sum(part, axis=1, keepdims=True)

        def max_below(x, strict):
            xb = bc(x)
            if strict:
                part = fold(lambda p, sj: jnp.maximum(p, jnp.where(sj < xb, sj, BELOW_NEG)),
                            jnp.full((rs, LANES), BELOW_NEG, F32))
            else:
                part = fold(lambda p, sj: jnp.maximum(p, jnp.where(sj <= xb, sj, BELOW_NEG)),
                            jnp.full((rs, LANES), BELOW_NEG, F32))
            return jnp.max(part, axis=1, keepdims=True)

        ge = lambda a, b: a >= b
        gt = lambda a, b: a > b

        hi = jnp.max(fold(jnp.maximum, jnp.full((rs, LANES), BELOW_NEG, F32)), axis=1, keepdims=True)
        lo = jnp.min(fold(lambda p, sj: jnp.minimum(p, jnp.where(sj > HALF_NEG, sj, BIG_POS)),
                          jnp.full((rs, LANES), BIG_POS, F32)), axis=1, keepdims=True)
        rpos = r0 + sb * rs + lax.broadcasted_iota(jnp.int32, (rs, 1), 0)
        n_valid = jnp.minimum(((rpos >> 6) + 1) << 6, n_keys)
        small = n_valid < k_sel

        def bisect(_, c):
            lo_c, hi_c = c
            mid = 0.5 * (lo_c + hi_c)
            up = count(ge, mid) >= ksel_f
            return jnp.where(up, mid, lo_c), jnp.where(up, hi_c, mid)

        lo, hi = lax.fori_loop(0, n_bisect, bisect, (lo, hi))

        v0 = max_below(hi, False)
        g0 = count(ge, v0)
        done0 = jnp.where(jnp.logical_or(g0 >= ksel_f, small), 1.0, 0.0)

        def walk_cond(c):
            return jnp.min(c[2]) < 0.5

        def walk_body(c):
            t_c, g_c, done_c = c
            v = max_below(t_c, True)
            g = count(ge, v)
            keep = done_c > 0.5
            return (jnp.where(keep, t_c, v), jnp.where(keep, g_c, g),
                    jnp.where(jnp.logical_or(keep, g >= ksel_f), 1.0, 0.0))

        t, g_t, _ = lax.while_loop(walk_cond, walk_body, (v0, g0, done0))
        c_gt = count(gt, t)
        need = ksel_f - c_gt
        excess = jnp.logical_and(jnp.logical_and(g_t - c_gt > need, jnp.logical_not(small)), t > HALF_NEG)
        t_sc[rsl, :] = jnp.where(small, NEG_INF, t)
        need_sc[rsl, :] = need
        return jnp.maximum(flag, jnp.max(jnp.where(excess, 1.0, 0.0)))

    tie_flag = lax.fori_loop(0, tq // rs, search, jnp.float32(0.0))

    q = q_ref[...]
    q0a, q0b = _split_halves(q[:, :LANES], lane)
    q1a, q1b = _split_halves(q[:, LANES:], lane)
    qq = (jnp.concatenate([q0a, q0b], axis=0), jnp.concatenate([q1a, q1b], axis=0))
    m_sc[...] = jnp.full(m_sc.shape, NEG_INF, F32)
    l_sc[...] = jnp.zeros(l_sc.shape, F32)
    acc_sc[...] = jnp.zeros(acc_sc.shape, F32)
    carry_sc[...] = jnp.zeros(carry_sc.shape, F32)
    t_all = t_sc[...]

    def attend(kb, sel):
        ks = pl.multiple_of(kb * tk, tk)
        kblk = k_ref[pl.ds(ks, tk), :]
        vblk = v_ref[pl.ds(ks, tk), :]
        sel2 = jnp.concatenate([sel, sel], axis=0)
        for pr in range(2):
            s = _dot_nt(qq[pr], kblk[:, pr * LANES:(pr + 1) * LANES])
            s = jnp.where(sel2, s, NEG_INF)
            _online_update(s, vblk[:, pr * LANES:(pr + 1) * LANES], m_sc, l_sc, acc_sc,
                           slice(pr * 2 * tq, (pr + 1) * 2 * tq), sel=sel2)

    def att_step(kb, masked):
        ks = pl.multiple_of(kb * tk, tk)
        sc = s_sc[kb]
        vis = _visible(r0, ks, tq, tk) if masked else None

        def plain():
            sel = sc >= t_all
            if masked:
                sel = jnp.logical_and(sel, vis)
            attend(kb, sel)

        def with_ties():
            eq = sc == t_all
            if masked:
                eq = jnp.logical_and(eq, vis)
            eqf = jnp.where(eq, 1.0, 0.0)
            upper = (lax.broadcasted_iota(jnp.int32, (tk, tk), 0)
                     < lax.broadcasted_iota(jnp.int32, (tk, tk), 1))
            before = _bdot(eqf.astype(BF16), jnp.where(upper, 1.0, 0.0).astype(BF16)) + carry_sc[...]
            sel = jnp.logical_or(sc > t_all, jnp.logical_and(eq, before < need_sc[...]))
            if masked:
                sel = jnp.logical_and(sel, vis)
            carry_sc[...] = carry_sc[...] + jnp.sum(eqf, axis=1, keepdims=True)
            attend(kb, sel)

        lax.cond(tie_flag > 0.5, with_ties, plain)

    def att_full(kb, c):
        att_step(kb, False)
        return c

    def att_masked(kb, c):
        att_step(kb, True)
        return c

    lax.fori_loop(0, n_full, att_full, 0)
    lax.fori_loop(n_full, n_vis, att_masked, 0)

    o = acc_sc[...] / l_sc[...]
    for pr in range(2):
        lo_h = o[(2 * pr) * tq:(2 * pr + 1) * tq]
        hi_h = o[(2 * pr + 1) * tq:(2 * pr + 2) * tq]
        o_ref[:, pr * LANES:(pr + 1) * LANES] = jnp.where(lane < 64, lo_h, hi_h).astype(BF16)


def _dsa_call(q, qi, w, k, v, ki, *, nb, nq, tq, tk, lp, q_blk0, k_blk0, q_pos0, n_keys, n_bisect):
    nkb = lp // tk
    k_sel = min(DSA_TOPK, n_keys // 4)
    rs = min(64, tq)
    kern = functools.partial(_dsa_kernel, tq=tq, tk=tk, nkb=nkb, q_pos0=q_pos0, n_keys=n_keys,
                             k_sel=k_sel, rs=rs, n_bisect=n_bisect)
    return pl.pallas_call(
        kern,
        grid=(nb, nq),
        in_specs=[pl.BlockSpec((tq, B_WIDTH), lambda b, i: (q_blk0 + b * nq + i, 0)),
                  pl.BlockSpec((tq, B_WIDTH), lambda b, i: (q_blk0 + b * nq + i, 0)),
                  pl.BlockSpec((tq, LANES), lambda b, i: (q_blk0 + b * nq + i, 0)),
                  pl.BlockSpec((lp, B_WIDTH), lambda b, i: (k_blk0 + b, 0)),
                  pl.BlockSpec((lp, B_WIDTH), lambda b, i: (k_blk0 + b, 0)),
                  pl.BlockSpec((lp, LANES), lambda b, i: (k_blk0 + b, 0))],
        out_specs=pl.BlockSpec((tq, B_WIDTH), lambda b, i: (b * nq + i, 0)),
        out_shape=jax.ShapeDtypeStruct((nb * nq * tq, B_WIDTH), BF16),
        scratch_shapes=[pltpu.VMEM((nkb, tq, tk), F32),
                        pltpu.VMEM((4 * tq, 1), F32), pltpu.VMEM((4 * tq, 1), F32),
                        pltpu.VMEM((4 * tq, LANES), F32),
                        pltpu.VMEM((tq, 1), F32), pltpu.VMEM((tq, 1), F32), pltpu.VMEM((tq, 1), F32)],
        compiler_params=_cparams(("arbitrary", "arbitrary")),
        name="dsa_attention",
    )(q, qi, w, k, v, ki)


def _kb_kernel(x_ref, oa_ref, ob_ref, oc_ref, sha_ref, sca_ref, gta_ref, shm_ref, scm_ref,
               nmix_ref, nffn_ref, wg_ref, wba_ref, wbb_ref, wbc_ref, wout_ref, rw_ref, rb_ref,
               x1_o, h2_o, e_o, g_o, *, tm, ch):
    x = x_ref[...]
    hb = _modulate(_rms(x, nmix_ref[...]), sca_ref[...], sha_ref[...], tm, ch).astype(BF16)

    def gate(c0):
        return _sigmoid(_bdot(hb, wg_ref[:, c0:c0 + D_MODEL]))

    merged = gate(0) * _bdot(oa_ref[...], wba_ref[...])
    merged = merged + gate(D_MODEL) * _bdot(ob_ref[...], wbb_ref[...])
    merged = merged + gate(2 * D_MODEL) * _bdot(oc_ref[...], wbc_ref[...])
    y = _bdot(merged.astype(BF16), wout_ref[...])
    x1 = x + _scale_rows(y, gta_ref[...], tm, ch)
    x1_o[...] = x1
    h2 = _modulate(_rms(x1, nffn_ref[...]), scm_ref[...], shm_ref[...], tm, ch)
    h2_o[...] = h2.astype(BF16)
    lg = _dot_split(h2, rw_ref[...]) + rb_ref[...]
    lanef = lax.broadcasted_iota(jnp.int32, (tm, LANES), 1).astype(F32)
    e_acc = jnp.zeros((tm, LANES), F32)
    v_acc = jnp.full((tm, LANES), NEG_INF, F32)
    for k in range(TOP_K):
        mx = jnp.max(lg, axis=1, keepdims=True)
        idx = jnp.min(jnp.where(lg == mx, lanef, float(LANES)), axis=1, keepdims=True)
        e_acc = jnp.where(lanef == float(k), idx, e_acc)
        v_acc = jnp.where(lanef == float(k), mx, v_acc)
        lg = jnp.where(lanef == idx, BELOW_NEG, lg)
    ex = jnp.where(lanef < float(TOP_K), jnp.exp(v_acc - jnp.max(v_acc, axis=1, keepdims=True)), 0.0)
    e_o[...] = e_acc.astype(jnp.int32)
    g_o[...] = ex / jnp.sum(ex, axis=1, keepdims=True)


def _kb_call(x, oa, ob, oc, mods, nmix, nffn, wg, wba, wbb, wbc, wout, rw, rb):
    n = x.shape[0]
    tm = TOKEN_BLOCK
    ch = tm // MOD_ROWS

    def row(i):
        return (i, 0)

    def full(a):
        return pl.BlockSpec(a.shape, lambda i: (0, 0))

    in_specs = ([pl.BlockSpec((tm, D_MODEL), row), pl.BlockSpec((tm, A_WIDTH), row),
                 pl.BlockSpec((tm, B_WIDTH), row), pl.BlockSpec((tm, 512), row)]
                + [pl.BlockSpec((MOD_ROWS, D_MODEL), row) for _ in range(5)]
                + [full(a) for a in (nmix, nffn, wg, wba, wbb, wbc, wout, rw, rb)])
    out_shape = [jax.ShapeDtypeStruct((n, D_MODEL), F32), jax.ShapeDtypeStruct((n, D_MODEL), BF16),
                 jax.ShapeDtypeStruct((n, LANES), jnp.int32), jax.ShapeDtypeStruct((n, LANES), F32)]
    out_specs = [pl.BlockSpec((tm, D_MODEL), row), pl.BlockSpec((tm, D_MODEL), row),
                 pl.BlockSpec((tm, LANES), row), pl.BlockSpec((tm, LANES), row)]
    return pl.pallas_call(
        functools.partial(_kb_kernel, tm=tm, ch=ch),
        grid=(n // tm,),
        in_specs=in_specs,
        out_specs=out_specs,
        out_shape=out_shape,
        compiler_params=_cparams(("arbitrary",)),
        name="post_attention",
    )(x, oa, ob, oc, *mods, nmix, nffn, wg, wba, wbb, wbc, wout, rw, rb)


def _ke_kernel(be_ref, nu_ref, x_ref, wg_ref, wl_ref, wd_ref, bg_ref, bl_ref, bd_ref, y_ref):
    i = pl.program_id(0)

    @pl.when(i < nu_ref[0])
    def _():
        x = x_ref[...]
        g = jnp.minimum(_bdot(x, wg_ref[0]) + bg_ref[0], SWIGLU_LIMIT)
        l = jnp.clip(_bdot(x, wl_ref[0]) + bl_ref[0], -SWIGLU_LIMIT, SWIGLU_LIMIT)
        act = g * _sigmoid(SWIGLU_ALPHA * g) * (l + 1.0)
        y_ref[...] = _bdot(act.astype(BF16), wd_ref[0]) + bd_ref[0]

    @pl.when(i >= nu_ref[0])
    def _():
        y_ref[...] = jnp.zeros(y_ref.shape, F32)


def _ke_call(blk_e, n_used, xg, wg, wl, wd, bg, bl, bd):
    n_rows = xg.shape[0]
    eb = EXPERT_ROWS
    n_blocks = n_rows // eb
    grid_spec = pltpu.PrefetchScalarGridSpec(
        num_scalar_prefetch=2,
        grid=(n_blocks,),
        in_specs=[pl.BlockSpec((eb, D_MODEL), lambda i, be, nu: (i, 0)),
                  pl.BlockSpec((1, D_MODEL, D_FF), lambda i, be, nu: (be[i], 0, 0)),
                  pl.BlockSpec((1, D_MODEL, D_FF), lambda i, be, nu: (be[i], 0, 0)),
                  pl.BlockSpec((1, D_FF, D_MODEL), lambda i, be, nu: (be[i], 0, 0)),
                  pl.BlockSpec((1, 1, D_FF), lambda i, be, nu: (be[i], 0, 0)),
                  pl.BlockSpec((1, 1, D_FF), lambda i, be, nu: (be[i], 0, 0)),
                  pl.BlockSpec((1, 1, D_MODEL), lambda i, be, nu: (be[i], 0, 0))],
        out_specs=pl.BlockSpec((eb, D_MODEL), lambda i, be, nu: (i, 0)),
    )
    return pl.pallas_call(
        _ke_kernel,
        grid_spec=grid_spec,
        out_shape=jax.ShapeDtypeStruct((n_rows, D_MODEL), F32),
        compiler_params=_cparams(("arbitrary",)),
        name="moe_experts",
    )(blk_e, n_used, xg, wg, wl, wd, bg, bl, bd)


def _kc_kernel(x1_ref, yg_ref, g_ref, gtm_ref, fn_ref, x2_o, y_o, *, tm, ch):
    gate = g_ref[...]
    ffn = gate[:, 0:1] * yg_ref[0]
    for k in range(1, TOP_K):
        ffn = ffn + gate[:, k:k + 1] * yg_ref[k]
    x2 = x1_ref[...] + _scale_rows(ffn, gtm_ref[...], tm, ch)
    x2_o[...] = x2
    y_o[...] = _rms(x2, fn_ref[...])


def _kc_call(x1, yg, gate, gtm, fnorm):
    n = x1.shape[0]
    tm = TOKEN_BLOCK
    ch = tm // MOD_ROWS
    return pl.pallas_call(
        functools.partial(_kc_kernel, tm=tm, ch=ch),
        grid=(n // tm,),
        in_specs=[pl.BlockSpec((tm, D_MODEL), lambda i: (i, 0)),
                  pl.BlockSpec((TOP_K, tm, D_MODEL), lambda i: (0, i, 0)),
                  pl.BlockSpec((tm, LANES), lambda i: (i, 0)),
                  pl.BlockSpec((MOD_ROWS, D_MODEL), lambda i: (i, 0)),
                  pl.BlockSpec((1, D_MODEL), lambda i: (0, 0))],
        out_specs=[pl.BlockSpec((tm, D_MODEL), lambda i: (i, 0)), pl.BlockSpec((tm, D_MODEL), lambda i: (i, 0))],
        out_shape=[jax.ShapeDtypeStruct((n, D_MODEL), F32), jax.ShapeDtypeStruct((n, D_MODEL), F32)],
        compiler_params=_cparams(("arbitrary",)),
        name="moe_combine",
    )(x1, yg, gate, gtm, fnorm)


def _rope_tables(pos):
    lane = np.arange(LANES)
    inv32 = ROPE_THETA ** (-jnp.arange(32, dtype=F32) / 32)
    inv16 = ROPE_THETA ** (-jnp.arange(16, dtype=F32) / 16)
    ang64 = pos[:, None] * inv32[None, :][:, lane & 31]
    ang32 = pos[:, None] * inv16[None, :][:, lane & 15]
    sign64 = jnp.asarray(np.where((lane & 63) < 32, -1.0, 1.0), F32)[None, :]
    sign32 = jnp.asarray(np.where((lane & 31) < 16, -1.0, 1.0), F32)[None, :]
    in_m = jnp.asarray(lane < C_ROPE)[None, :]
    in_q = jnp.asarray((lane >= C_NOPE) & (lane < C_NOPE + C_ROPE))[None, :]
    cos64, sin64 = jnp.cos(ang64), jnp.sin(ang64) * sign64
    cos32, sin32 = jnp.cos(ang32), jnp.sin(ang32) * sign32
    return (cos64, sin64,
            jnp.where(in_m, cos32, 1.0), jnp.where(in_m, sin32, 0.0),
            jnp.where(in_q, cos32, 1.0), jnp.where(in_q, sin32, 0.0))


def _layer_weights(l, w_in, mla_w_uq, mla_w_ukv, w_br_c, router_w, router_b, exp_w_gu, exp_b_gu):
    wi = w_in[l]
    z = lambda n: jnp.zeros((D_MODEL, n), F32)
    w1 = jnp.concatenate([
        wi[:, _OFF_AQ:_OFF_BIK],
        wi[:, _OFF_BIK:_OFF_BIW], wi[:, _OFF_BIK:_OFF_BIW],
        wi[:, _OFF_CKR:_OFF_GATES], wi[:, _OFF_BIW:_OFF_CQ], z(LANES - C_ROPE - IDX_HEADS),
        wi[:, _OFF_CQ:_OFF_CKV], wi[:, _OFF_CKV:_OFF_CKR]], axis=1).astype(BF16)
    wg = wi[:, _OFF_GATES:].astype(BF16)
    wuq = mla_w_uq[l].reshape(Q_LORA, C_HEADS, C_NOPE + C_ROPE)
    wuq = jnp.pad(wuq, ((0, 0), (0, 0), (0, LANES - C_NOPE - C_ROPE))).reshape(Q_LORA, C_HEADS * LANES).astype(BF16)
    wukv = mla_w_ukv[l].reshape(KV_LORA, C_HEADS, C_NOPE + C_V)
    wk = jnp.pad(wukv[:, :, :C_NOPE], ((0, 0), (0, 0), (0, LANES - C_NOPE))).reshape(KV_LORA, C_HEADS * LANES).astype(BF16)
    wv = jnp.pad(wukv[:, :, C_NOPE:], ((0, 0), (0, 0), (0, LANES - C_V))).reshape(KV_LORA, C_HEADS * LANES).astype(BF16)
    e = np.zeros((LANES, C_HEADS * LANES), np.float32)
    for h in range(C_HEADS):
        e[np.arange(C_ROPE), h * LANES + C_NOPE + np.arange(C_ROPE)] = 1.0
    we = jnp.asarray(e, BF16)
    wbc = jnp.pad(w_br_c[l].reshape(C_HEADS, C_V, D_MODEL), ((0, 0), (0, LANES - C_V), (0, 0)))
    wbc = wbc.reshape(C_HEADS * LANES, D_MODEL).astype(BF16)
    rw = jnp.pad(router_w[l], ((0, 0), (0, LANES - N_EXPERTS)))
    rb = jnp.pad(router_b[l], (0, LANES - N_EXPERTS), constant_values=NEG_INF).reshape(1, LANES)
    wgl = exp_w_gu[l].reshape(N_EXPERTS, D_MODEL, D_FF, 2)
    bgl = exp_b_gu[l].reshape(N_EXPERTS, 1, D_FF, 2)
    return dict(w1=w1, wg=wg, wuq=wuq, wk=wk, wv=wv, we=we, wbc=wbc, rw=rw, rb=rb,
                e_wg=wgl[..., 0].astype(BF16), e_wl=wgl[..., 1].astype(BF16),
                e_bg=bgl[..., 0], e_bl=bgl[..., 1])


def _route(e_pad, n_rows_pad):
    eb = EXPERT_ROWS
    flat_e = e_pad[:, :TOP_K].reshape(-1)
    nk = flat_e.shape[0]
    onehot = (flat_e[:, None] == jnp.arange(N_EXPERTS, dtype=jnp.int32)[None, :]).astype(jnp.int32)
    csum = jnp.cumsum(onehot, axis=0)
    rank = jnp.take_along_axis(csum, flat_e[:, None], axis=1)[:, 0] - 1
    counts = csum[-1]
    padded = (counts + eb - 1) // eb * eb
    pad_end = jnp.cumsum(padded)
    pad_start = pad_end - padded
    dest = pad_start[flat_e] + rank
    row_tok = jnp.zeros((n_rows_pad,), jnp.int32).at[dest].set(jnp.arange(nk, dtype=jnp.int32) // TOP_K)
    n_blocks = n_rows_pad // eb
    blk_start = jnp.arange(n_blocks, dtype=jnp.int32) * eb
    blk_e = jnp.minimum(jnp.searchsorted(pad_end, blk_start, side='right'), N_EXPERTS - 1).astype(jnp.int32)
    n_used = (pad_end[-1] // eb).astype(jnp.int32).reshape(1)
    return dest, row_tok, blk_e, n_used


def kernel(x_prompt, x_sample, cache_a_k, cache_a_v, cache_b_k, cache_b_v, cache_b_idx_k, cache_c_latent, cache_c_k_rope, c_prompt, c_sample, w_ada, b_ada, norm_mix, norm_ffn, w_in, diff_lq1, diff_lk1, diff_lq2, diff_lk2, diff_subln, mla_q_norm, mla_w_uq, mla_kv_norm, mla_w_ukv, w_br_a, w_br_b, w_br_c, w_out, router_w, router_b, exp_w_gu, exp_b_gu, exp_w_down, exp_b_down, final_norm):
    depth = w_ada.shape[0]
    bp, tp, _ = x_prompt.shape
    bs, ts, _ = x_sample.shape
    past = cache_c_latent.shape[2]
    n_p, n_s = bp * tp, bs * ts
    n = n_p + n_s
    tm = TOKEN_BLOCK
    ch = tm // MOD_ROWS
    assert ts == CHUNK and tp % tm == 0 and n_s % tm == 0 and past % CHUNK == 0

    x = jnp.concatenate([x_prompt.reshape(n_p, D_MODEL), x_sample.reshape(n_s, D_MODEL)], axis=0)

    n_seq = bp + bs
    c_all = jnp.concatenate([c_prompt, c_sample], axis=0)
    c_pad = jnp.pad(c_all, ((0, (-n_seq) % 8), (0, 0)))
    mod = _ada_call(c_pad, w_ada, b_ada)
    def per_chunk(m, reps):
        return jnp.broadcast_to(m[:, :, None, :], m.shape[:2] + (reps, m.shape[2])).reshape(depth, -1, m.shape[2])

    mod_rows = jnp.concatenate([per_chunk(mod[:, :bp], tp // ch), per_chunk(mod[:, bp:n_seq], ts // ch)],
                               axis=1)

    pos = jnp.concatenate([jnp.arange(tp, dtype=F32),
                           jnp.tile(past + jnp.arange(ts, dtype=F32), tm // ts)])
    tabs = _rope_tables(pos)

    tq_p = min(256, tp)
    tk_p = tq_p
    nq_p = tp // tq_p
    l_s = past + ts
    tk_s = 128
    lp_s = -(-l_s // tk_s) * tk_s
    n_rows_pad = -(-(n * TOP_K + N_EXPERTS * (EXPERT_ROWS - 1)) // EXPERT_ROWS) * EXPERT_ROWS

    def with_cache(cache_l, new, width):
        parts = [cache_l.reshape(bs, past, width).astype(BF16), new.reshape(bs, ts, width)]
        if lp_s > l_s:
            parts.append(jnp.zeros((bs, lp_s - l_s, width), BF16))
        return jnp.concatenate(parts, axis=1).reshape(bs * lp_s, width)

    caches = [[] for _ in range(7)]
    y = None
    for l in range(depth):
        lam_init = 0.8 - 0.6 * math.exp(-0.3 * l)
        wl = _layer_weights(l, w_in, mla_w_uq, mla_w_ukv, w_br_c, router_w, router_b, exp_w_gu, exp_b_gu)
        m6 = [mod_rows[l, :, j * D_MODEL:(j + 1) * D_MODEL] for j in range(6)]
        sh_a, sc_a, gt_a, sh_m, sc_m, gt_m = m6
        nmix = norm_mix[l].reshape(1, D_MODEL)
        nffn = norm_ffn[l].reshape(1, D_MODEL)

        (ak, av, bk, bv, bik, clat, ckr, misc,
         aq_b, ak_b, av_b, bq_b, bk_b, bv_b, biq_b, bik2_b, cq_b, ck_b, cv_b) = _ka_call(
            x, sh_a, sc_a, tabs, nmix, wl['w1'], mla_q_norm[l].reshape(1, Q_LORA), wl['wuq'],
            mla_kv_norm[l].reshape(1, KV_LORA), wl['wk'], wl['we'], wl['wv'],
            n_prompt_blocks=n_p // tm, tab_blocks=tp // tm)
        for i, a in enumerate((ak, av, bk, bv, bik, clat, ckr)):
            caches[i].append(a)

        lqk = jnp.pad(jnp.stack([diff_lq1[l], diff_lk1[l], diff_lq2[l], diff_lk2[l]]),
                      ((0, 4), (0, LANES - A_HD)))
        sub = diff_subln[l].reshape(1, 2 * A_HD)
        oa_p = _diff_call(lqk, sub, aq_b, ak_b, av_b, nb=bp, nq=nq_p, tq=tq_p, tk=tk_p, lp=tp,
                          q_blk0=0, k_blk0=0, q_pos0=0, lam_init=lam_init)
        ak_s = with_cache(cache_a_k[l], ak_b[n_p:], A_WIDTH)
        av_s = with_cache(cache_a_v[l], av_b[n_p:], A_WIDTH)
        oa_s = _diff_call(lqk, sub, aq_b, ak_s, av_s, nb=bs, nq=1, tq=ts, tk=tk_s, lp=lp_s,
                          q_blk0=n_p // ts, k_blk0=0, q_pos0=past, lam_init=lam_init)
        ob_p = _dsa_call(bq_b, biq_b, misc, bk_b, bv_b, bik2_b, nb=bp, nq=nq_p, tq=tq_p, tk=tk_p, lp=tp,
                         q_blk0=0, k_blk0=0, q_pos0=0, n_keys=tp, n_bisect=14)
        bk_s = with_cache(cache_b_k[l], bk_b[n_p:], B_WIDTH)
        bv_s = with_cache(cache_b_v[l], bv_b[n_p:], B_WIDTH)
        cik = cache_b_idx_k[l].reshape(bs, past, IDX_DIM)
        bik_s = with_cache(jnp.concatenate([cik, cik], axis=-1), bik2_b[n_p:], LANES)
        ob_s = _dsa_call(bq_b, biq_b, misc, bk_s, bv_s, bik_s, nb=bs, nq=1, tq=ts, tk=tk_s, lp=lp_s,
                         q_blk0=n_p // ts, k_blk0=0, q_pos0=past, n_keys=l_s, n_bisect=12)
        oc_p = _mla_call(cq_b, ck_b, cv_b, nb=bp, nq=nq_p, tq=tq_p, tk=tk_p, lp=tp,
                         q_blk0=0, k_blk0=0, q_pos0=0)
        lat_c = cache_c_latent[l].reshape(bs * past, KV_LORA)
        kr_c = jnp.pad(cache_c_k_rope[l].reshape(bs * past, C_ROPE), ((0, 0), (0, LANES - C_ROPE)))
        ck_c, cv_c = _mla_kv_call(lat_c, kr_c, wl['wk'], wl['we'], wl['wv'])
        ck_s = with_cache(ck_c, ck_b[n_p:], 512)
        cv_s = with_cache(cv_c, cv_b[n_p:], 512)
        oc_s = _mla_call(cq_b, ck_s, cv_s, nb=bs, nq=1, tq=ts, tk=tk_s, lp=lp_s,
                         q_blk0=n_p // ts, k_blk0=0, q_pos0=past)

        oa = jnp.concatenate([oa_p, oa_s], axis=0)
        ob = jnp.concatenate([ob_p, ob_s], axis=0)
        oc = jnp.concatenate([oc_p, oc_s], axis=0)

        x1, h2, e_pad, g_pad = _kb_call(
            x, oa, ob, oc, (sh_a, sc_a, gt_a, sh_m, sc_m), nmix, nffn, wl['wg'],
            w_br_a[l].astype(BF16), w_br_b[l].astype(BF16), wl['wbc'], w_out[l].astype(BF16),
            wl['rw'], wl['rb'])

        dest, row_tok, blk_e, n_used = _route(e_pad, n_rows_pad)
        xg = jnp.take(h2, row_tok, axis=0)
        yr = _ke_call(blk_e, n_used, xg, wl['e_wg'], wl['e_wl'], exp_w_down[l].astype(BF16),
                      wl['e_bg'], wl['e_bl'], exp_b_down[l].reshape(N_EXPERTS, 1, D_MODEL))
        yg = jnp.take(yr, dest.reshape(n, TOP_K).T, axis=0)
        x, y = _kc_call(x1, yg, g_pad, gt_m, final_norm.reshape(1, D_MODEL))

    def split(a, tail):
        a = jnp.stack(a, axis=0)
        return (a[:, :n_p].reshape((depth, bp, tp) + tail), a[:, n_p:].reshape((depth, bs, ts) + tail))

    tails = ((A_HEADS, 2 * A_HD), (A_HEADS, 2 * A_HD), (B_HEADS, B_HD), (B_HEADS, B_HD),
             (IDX_DIM,), (KV_LORA,), (C_ROPE,))
    ps = [split(c, t) for c, t in zip(caches, tails)]
    y_prompt = y[:n_p].reshape(bp, tp, D_MODEL)
    y_sample = y[n_p:].reshape(bs, ts, D_MODEL)
    return (y_prompt, y_sample) + tuple(p[0] for p in ps) + tuple(p[1] for p in ps)
```

```python
import functools
import math

import numpy as np
import jax
import jax.numpy as jnp
from jax import lax
from jax.experimental import pallas as pl
from jax.experimental.pallas import tpu as pltpu

F32 = jnp.float32
BF16 = jnp.bfloat16

D_MODEL = 1024
CHUNK = 64
ROPE_THETA = 10000.0
NORM_EPS = 1e-6
NEG_INF = -1e30
HALF_NEG = -5e29
BELOW_NEG = -3e38
BIG_POS = 3e38
LOG2E = 1.4426950408889634

A_HEADS, A_HD = 4, 64
B_HEADS, B_HD = 4, 64
IDX_HEADS, IDX_DIM = 4, 64
DSA_TOPK = 256
C_HEADS, C_NOPE, C_ROPE, C_V = 4, 64, 32, 64
Q_LORA, KV_LORA = 256, 128
N_EXPERTS, TOP_K = 32, 4
D_FF = D_MODEL
SWIGLU_LIMIT = 7.0
SWIGLU_ALPHA = 1.702

A_WIDTH = A_HEADS * 2 * A_HD
B_WIDTH = B_HEADS * B_HD
LANES = 128

_OFF_AQ, _OFF_AK, _OFF_AV = 0, 512, 1024
_OFF_BQ, _OFF_BK, _OFF_BV = 1536, 1792, 2048
_OFF_BIQ, _OFF_BIK, _OFF_BIW = 2304, 2560, 2624
_OFF_CQ, _OFF_CKV, _OFF_CKR = 2628, 2884, 3012
_OFF_GATES = 3044
_P_AQ, _P_AK, _P_AV = 0, 512, 1024
_P_BQ, _P_BK, _P_BV = 1536, 1792, 2048
_P_BIQ, _P_BIK2, _P_MISC, _P_CQ, _P_CKV, _P_END = 2304, 2560, 2688, 2816, 3072, 3200
_MISC_BIW = 32

TOKEN_BLOCK = 256
MOD_ROWS = 8
EXPERT_ROWS = 256
VMEM_LIMIT = 56 * 1024 * 1024


def _cparams(sem, vmem=VMEM_LIMIT):
    return pltpu.CompilerParams(dimension_semantics=sem, vmem_limit_bytes=vmem)


def _rms(xf, g):
    return xf * lax.rsqrt(jnp.mean(xf * xf, axis=-1, keepdims=True) + NORM_EPS) * g


def _sigmoid(x):
    return 1.0 / (1.0 + jnp.exp(-x))


def _bdot(a, b):
    return jnp.dot(a, b, preferred_element_type=F32)


def _dot_nt(a, b):
    return lax.dot_general(a, b, (((1,), (1,)), ((), ())), preferred_element_type=F32)


def _dot_split(a, b):
    a_hi = a.astype(BF16)
    b_hi = b.astype(BF16)
    a_lo = (a - a_hi.astype(F32)).astype(BF16)
    b_lo = (b - b_hi.astype(F32)).astype(BF16)
    return _bdot(a_hi, b_hi) + (_bdot(a_hi, b_lo) + _bdot(a_lo, b_hi))


def _modulate(xn, sc, sh, rows, ch):
    n = rows // ch
    y = xn.reshape(n, ch, D_MODEL) * (1.0 + sc)[:, None, :] + sh[:, None, :]
    return y.reshape(rows, D_MODEL)


def _scale_rows(y, g, rows, ch):
    n = rows // ch
    return (y.reshape(n, ch, D_MODEL) * g[:, None, :]).reshape(rows, D_MODEL)


def _ada_kernel(c_ref, w_ref, b_ref, o_ref):
    c = c_ref[...]
    s = c * _sigmoid(c)
    o_ref[0] = _dot_split(s, w_ref[0]) + b_ref[0]


def _ada_call(c_pad, w_ada, b_ada):
    depth = w_ada.shape[0]
    mp = c_pad.shape[0]
    return pl.pallas_call(
        _ada_kernel,
        grid=(depth, 6),
        in_specs=[
            pl.BlockSpec((mp, D_MODEL), lambda l, j: (0, 0)),
            pl.BlockSpec((1, D_MODEL, D_MODEL), lambda l, j: (l, 0, j)),
            pl.BlockSpec((1, 1, D_MODEL), lambda l, j: (l, 0, j)),
        ],
        out_specs=pl.BlockSpec((1, mp, D_MODEL), lambda l, j: (l, 0, j)),
        out_shape=jax.ShapeDtypeStruct((depth, mp, 6 * D_MODEL), F32),
        compiler_params=_cparams(("arbitrary", "arbitrary")),
        name="ada_mod",
    )(c_pad, w_ada, b_ada.reshape(depth, 1, 6 * D_MODEL))


def _rope_partner(x, lane, half):
    first = (lane & (2 * half - 1)) < half
    return jnp.where(first, pltpu.roll(x, LANES - half, 1), pltpu.roll(x, half, 1))


def _ka_kernel(x_ref, sh_ref, sc_ref, cos_ref, sin_ref, cosm_ref, sinm_ref, cosq_ref, sinq_ref,
               nmix_ref, w1_ref, qn_ref, wuq_ref, kvn_ref, wk_ref, we_ref, wv_ref,
               ak_o, av_o, bk_o, bv_o, bik_o, clat_o, ckr_o, misc_o,
               aq_b, ak_b, av_b, bq_b, bk_b, bv_b, biq_b, bik2_b, cq_b, ck_b, cv_b, *, tm, ch):
    x = x_ref[...]
    h = _modulate(_rms(x, nmix_ref[...]), sc_ref[...], sh_ref[...], tm, ch)
    hb = h.astype(BF16)
    lane = lax.broadcasted_iota(jnp.int32, (tm, LANES), 1)
    cos = cos_ref[...]
    sin = sin_ref[...]

    def proj(c0, c1):
        return _bdot(hb, w1_ref[:, c0:c1])

    def rope64(xb):
        return xb * cos + _rope_partner(xb, lane, 32) * sin

    a_scale = (A_HD ** -0.5) * LOG2E
    b_scale = (B_HD ** -0.5) * LOG2E
    i_scale = IDX_DIM ** -0.5
    c_scale = ((C_NOPE + C_ROPE) ** -0.5) * LOG2E

    p = proj(_P_AQ, _P_AK)
    for c in range(A_WIDTH // LANES):
        sl = slice(c * LANES, (c + 1) * LANES)
        aq_b[:, sl] = (rope64(p[:, sl]) * a_scale).astype(BF16)
    p = proj(_P_AK, _P_AV)
    for c in range(A_WIDTH // LANES):
        sl = slice(c * LANES, (c + 1) * LANES)
        r = rope64(p[:, sl])
        ak_o[:, sl] = r
        ak_b[:, sl] = r.astype(BF16)
    p = proj(_P_AV, _P_BQ)
    av_o[...] = p
    av_b[...] = p.astype(BF16)
    p = proj(_P_BQ, _P_BK)
    for c in range(B_WIDTH // LANES):
        sl = slice(c * LANES, (c + 1) * LANES)
        bq_b[:, sl] = (rope64(p[:, sl]) * b_scale).astype(BF16)
    p = proj(_P_BK, _P_BV)
    for c in range(B_WIDTH // LANES):
        sl = slice(c * LANES, (c + 1) * LANES)
        r = rope64(p[:, sl])
        bk_o[:, sl] = r
        bk_b[:, sl] = r.astype(BF16)
    p = proj(_P_BV, _P_BIQ)
    bv_o[...] = p
    bv_b[...] = p.astype(BF16)
    p = proj(_P_BIQ, _P_BIK2)
    for c in range(B_WIDTH // LANES):
        sl = slice(c * LANES, (c + 1) * LANES)
        biq_b[:, sl] = (rope64(p[:, sl]) * i_scale).astype(BF16)
    r = rope64(proj(_P_BIK2, _P_MISC))
    bik_o[...] = r[:, :IDX_DIM]
    bik2_b[...] = r.astype(BF16)
    pm = proj(_P_MISC, _P_CQ)
    misc = pm * cosm_ref[...] + _rope_partner(pm, lane, 16) * sinm_ref[...]
    misc_o[...] = misc
    ckr_o[...] = misc[:, :C_ROPE]
    qlat = _rms(proj(_P_CQ, _P_CKV), qn_ref[...]).astype(BF16)
    cqf = _bdot(qlat, wuq_ref[...])
    cosq = cosq_ref[...]
    sinq = sinq_ref[...]
    for c in range(C_HEADS):
        sl = slice(c * LANES, (c + 1) * LANES)
        xb = cqf[:, sl]
        cq_b[:, sl] = ((xb * cosq + _rope_partner(xb, lane, 16) * sinq) * c_scale).astype(BF16)
    clat = _rms(proj(_P_CKV, _P_END), kvn_ref[...])
    clat_o[...] = clat
    clb = clat.astype(BF16)
    ck_b[...] = (_bdot(clb, wk_ref[...]) + _bdot(misc.astype(BF16), we_ref[...])).astype(BF16)
    cv_b[...] = _bdot(clb, wv_ref[...]).astype(BF16)


def _ka_call(x, sh, sc, tabs, nmix, w1, qn, wuq, kvn, wk, we, wv, *, n_prompt_blocks, tab_blocks):
    n = x.shape[0]
    tm = TOKEN_BLOCK
    ch = tm // MOD_ROWS
    nblk = n // tm

    def row(i):
        return (i, 0)

    def tab(i):
        return (jnp.where(i < n_prompt_blocks, i % tab_blocks, tab_blocks), 0)

    def const(i):
        return (0, 0)

    def full(a):
        return pl.BlockSpec(a.shape, const)

    widths_f32 = (A_WIDTH, A_WIDTH, B_WIDTH, B_WIDTH, IDX_DIM, KV_LORA, C_ROPE, LANES)
    widths_b16 = (A_WIDTH, A_WIDTH, A_WIDTH, B_WIDTH, B_WIDTH, B_WIDTH, B_WIDTH, LANES, 512, 512, 512)
    out_shape = ([jax.ShapeDtypeStruct((n, w), F32) for w in widths_f32]
                 + [jax.ShapeDtypeStruct((n, w), BF16) for w in widths_b16])
    out_specs = [pl.BlockSpec((tm, w), row) for w in widths_f32 + widths_b16]
    in_specs = ([pl.BlockSpec((tm, D_MODEL), row),
                 pl.BlockSpec((MOD_ROWS, D_MODEL), row),
                 pl.BlockSpec((MOD_ROWS, D_MODEL), row)]
                + [pl.BlockSpec((tm, LANES), tab) for _ in range(6)]
                + [full(a) for a in (nmix, w1, qn, wuq, kvn, wk, we, wv)])
    return pl.pallas_call(
        functools.partial(_ka_kernel, tm=tm, ch=ch),
        grid=(nblk,),
        in_specs=in_specs,
        out_specs=out_specs,
        out_shape=out_shape,
        compiler_params=_cparams(("arbitrary",)),
        name="pre_attention",
    )(x, sh, sc, *tabs, nmix, w1, qn, wuq, kvn, wk, we, wv)


def _mla_kv_kernel(lat_ref, kr_ref, wk_ref, we_ref, wv_ref, ck_o, cv_o):
    lb = lat_ref[...].astype(BF16)
    ck_o[...] = (_bdot(lb, wk_ref[...]) + _bdot(kr_ref[...].astype(BF16), we_ref[...])).astype(BF16)
    cv_o[...] = _bdot(lb, wv_ref[...]).astype(BF16)


def _mla_kv_call(lat, krp, wk, we, wv):
    n = lat.shape[0]
    tm = 512
    return pl.pallas_call(
        _mla_kv_kernel,
        grid=(n // tm,),
        in_specs=[pl.BlockSpec((tm, KV_LORA), lambda i: (i, 0)),
                  pl.BlockSpec((tm, LANES), lambda i: (i, 0)),
                  pl.BlockSpec(wk.shape, lambda i: (0, 0)),
                  pl.BlockSpec(we.shape, lambda i: (0, 0)),
                  pl.BlockSpec(wv.shape, lambda i: (0, 0))],
        out_specs=[pl.BlockSpec((tm, 512), lambda i: (i, 0)), pl.BlockSpec((tm, 512), lambda i: (i, 0))],
        out_shape=[jax.ShapeDtypeStruct((n, 512), BF16), jax.ShapeDtypeStruct((n, 512), BF16)],
        compiler_params=_cparams(("arbitrary",)),
        name="mla_cache_kv",
    )(lat, krp, wk, we, wv)


def _block_range(r0, tq, tk, nkb):
    n_full = jnp.minimum((r0 + CHUNK) // tk, nkb)
    n_vis = jnp.minimum((r0 + tq + tk - 1) // tk, nkb)
    return n_full, n_vis


def _visible(r0, ks, rows, tk):
    rpos = r0 + lax.broadcasted_iota(jnp.int32, (rows, tk), 0)
    kpos = ks + lax.broadcasted_iota(jnp.int32, (rows, tk), 1)
    return kpos < (((rpos >> 6) + 1) << 6)


def _fold_max(s, m_sc, rows):
    mp = m_sc[rows, :]
    for j in range(s.shape[1] // LANES):
        mp = jnp.maximum(mp, s[:, j * LANES:(j + 1) * LANES])
    m_sc[rows, :] = mp


def _finish_max(m_sc):
    m = jnp.max(m_sc[...], axis=1, keepdims=True)
    m_sc[...] = jnp.broadcast_to(m, m_sc.shape)


def _accumulate(s, vblk, m_sc, l_sc, acc_sc, rows):
    m = m_sc[rows, :]
    lp = l_sc[rows, :]
    ps = []
    for j in range(s.shape[1] // LANES):
        pj = jnp.exp2(s[:, j * LANES:(j + 1) * LANES] - m)
        lp = lp + pj
        ps.append(pj.astype(BF16))
    l_sc[rows, :] = lp
    acc_sc[rows, :] = acc_sc[rows, :] + _bdot(jnp.concatenate(ps, axis=1), vblk)


def _two_pass(n_full, n_vis, score, values, m_sc, l_sc, acc_sc, groups):
    m_sc[...] = jnp.full(m_sc.shape, NEG_INF, F32)
    l_sc[...] = jnp.zeros(l_sc.shape, F32)
    acc_sc[...] = jnp.zeros(acc_sc.shape, F32)

    def sweep(fn):
        def full_body(kb, c):
            fn(kb, False)
            return c

        def masked_body(kb, c):
            fn(kb, True)
            return c

        lax.fori_loop(0, n_full, full_body, 0)
        lax.fori_loop(n_full, n_vis, masked_body, 0)

    def pass1(kb, masked):
        for g, rows in enumerate(groups):
            _fold_max(score(kb, masked, g), m_sc, rows)

    def pass2(kb, masked):
        for g, rows in enumerate(groups):
            _accumulate(score(kb, masked, g), values(kb, g), m_sc, l_sc, acc_sc, rows)

    sweep(pass1)
    _finish_max(m_sc)
    sweep(pass2)
    return acc_sc[...] / jnp.sum(l_sc[...], axis=1, keepdims=True)


def _split_halves(q, lane=None):
    lane1 = lax.broadcasted_iota(jnp.int32, (1, LANES), 1)
    lo = jnp.where(lane1 < 64, 1.0, 0.0).astype(q.dtype)
    return q * lo, q * (1.0 - lo).astype(q.dtype)


def _diff_kernel(lqk_ref, sub_ref, q_ref, k_ref, v_ref, o_ref, m_sc, l_sc, acc_sc,
                 *, tq, tk, nkb, q_pos0, lam_init):
    i = pl.program_id(2)
    r0 = q_pos0 + i * tq
    n_full, n_vis = _block_range(r0, tq, tk, nkb)
    lane = lax.broadcasted_iota(jnp.int32, (tq, LANES), 1)
    qq = jnp.concatenate(_split_halves(q_ref[...], lane), axis=0)

    def score(kb, masked, g):
        ks = pl.multiple_of(kb * tk, tk)
        s = _dot_nt(qq, k_ref[pl.ds(ks, tk), :])
        if masked:
            vis = _visible(r0, ks, tq, tk)
            s = jnp.where(jnp.concatenate([vis, vis], axis=0), s, NEG_INF)
        return s

    def values(kb, g):
        return v_ref[pl.ds(pl.multiple_of(kb * tk, tk), tk), :]

    o = _two_pass(n_full, n_vis, score, values, m_sc, l_sc, acc_sc, (slice(0, 2 * tq),))
    o1 = o[:tq]
    o2 = o[tq:]
    lq = lqk_ref[...]
    lam = (jnp.exp(jnp.sum(lq[0:1] * lq[1:2], axis=1, keepdims=True))
           - jnp.exp(jnp.sum(lq[2:3] * lq[3:4], axis=1, keepdims=True)) + lam_init)
    o = o1 - lam * o2
    o = o * lax.rsqrt(jnp.mean(o * o, axis=1, keepdims=True) + NORM_EPS)
    o_ref[...] = (o * sub_ref[...] * (1.0 - lam_init)).astype(BF16)


def _diff_call(lqk, sub, q, k, v, *, nb, nq, tq, tk, lp, q_blk0, k_blk0, q_pos0, lam_init):
    nkb = lp // tk
    kern = functools.partial(_diff_kernel, tq=tq, tk=tk, nkb=nkb, q_pos0=q_pos0, lam_init=lam_init)
    return pl.pallas_call(
        kern,
        grid=(nb, A_HEADS, nq),
        in_specs=[pl.BlockSpec(lqk.shape, lambda b, h, i: (0, 0)),
                  pl.BlockSpec((1, LANES), lambda b, h, i: (0, 0)),
                  pl.BlockSpec((tq, LANES), lambda b, h, i: (q_blk0 + b * nq + i, h)),
                  pl.BlockSpec((lp, LANES), lambda b, h, i: (k_blk0 + b, h)),
                  pl.BlockSpec((lp, LANES), lambda b, h, i: (k_blk0 + b, h))],
        out_specs=pl.BlockSpec((tq, LANES), lambda b, h, i: (b * nq + i, h)),
        out_shape=jax.ShapeDtypeStruct((nb * nq * tq, A_WIDTH), BF16),
        scratch_shapes=[pltpu.VMEM((2 * tq, LANES), F32) for _ in range(3)],
        compiler_params=_cparams(("arbitrary", "arbitrary", "arbitrary")),
        name="diff_attention",
    )(lqk, sub, q, k, v)


def _mla_kernel(q_ref, k_ref, v_ref, o_ref, m_sc, l_sc, acc_sc, *, tq, tk, nkb, q_pos0):
    i = pl.program_id(2)
    r0 = q_pos0 + i * tq
    n_full, n_vis = _block_range(r0, tq, tk, nkb)
    q = q_ref[...]

    def score(kb, masked, g):
        ks = pl.multiple_of(kb * tk, tk)
        s = _dot_nt(q, k_ref[pl.ds(ks, tk), :])
        if masked:
            s = jnp.where(_visible(r0, ks, tq, tk), s, NEG_INF)
        return s

    def values(kb, g):
        return v_ref[pl.ds(pl.multiple_of(kb * tk, tk), tk), :]

    o = _two_pass(n_full, n_vis, score, values, m_sc, l_sc, acc_sc, (slice(0, tq),))
    o_ref[...] = o.astype(BF16)


def _mla_call(q, k, v, *, nb, nq, tq, tk, lp, q_blk0, k_blk0, q_pos0):
    nkb = lp // tk
    kern = functools.partial(_mla_kernel, tq=tq, tk=tk, nkb=nkb, q_pos0=q_pos0)
    return pl.pallas_call(
        kern,
        grid=(nb, C_HEADS, nq),
        in_specs=[pl.BlockSpec((tq, LANES), lambda b, h, i: (q_blk0 + b * nq + i, h)),
                  pl.BlockSpec((lp, LANES), lambda b, h, i: (k_blk0 + b, h)),
                  pl.BlockSpec((lp, LANES), lambda b, h, i: (k_blk0 + b, h))],
        out_specs=pl.BlockSpec((tq, LANES), lambda b, h, i: (b * nq + i, h)),
        out_shape=jax.ShapeDtypeStruct((nb * nq * tq, C_HEADS * LANES), BF16),
        scratch_shapes=[pltpu.VMEM((tq, LANES), F32) for _ in range(3)],
        compiler_params=_cparams(("arbitrary", "arbitrary", "arbitrary")),
        name="mla_attention",
    )(q, k, v)


def _dsa_kernel(q_ref, qi_ref, w_ref, k_ref, v_ref, ki_ref, o_ref,
                s_sc, m_sc, l_sc, acc_sc, t_sc, need_sc, carry_sc, wrep_sc,
                *, tq, tk, nkb, q_pos0, n_keys, k_sel, rs, n_bisect):
    i = pl.program_id(1)
    r0 = q_pos0 + i * tq
    n_full, n_vis = _block_range(r0, tq, tk, nkb)
    lane = lax.broadcasted_iota(jnp.int32, (tq, LANES), 1)
    ksel_f = float(k_sel)

    qi = qi_ref[...]
    qa, qb = _split_halves(qi[:, :LANES], lane)
    qc, qd = _split_halves(qi[:, LANES:], lane)
    qi4 = jnp.concatenate([qa, qb, qc, qd], axis=0)
    wm = w_ref[...]
    for h in range(IDX_HEADS):
        wcol = wm[:, _MISC_BIW + h:_MISC_BIW + h + 1] * (IDX_HEADS ** -0.5)
        wrep_sc[h] = jnp.broadcast_to(wcol, (tq, LANES))

    def score_step(kb, masked):
        ks = pl.multiple_of(kb * tk, tk)
        rel = _dot_nt(qi4, ki_ref[pl.ds(ks, tk), :])
        cols = []
        for j in range(tk // LANES):
            cl = slice(j * LANES, (j + 1) * LANES)
            sc = wrep_sc[0] * jnp.maximum(rel[:tq, cl], 0.0)
            for h in range(1, IDX_HEADS):
                sc = sc + wrep_sc[h] * jnp.maximum(rel[h * tq:(h + 1) * tq, cl], 0.0)
            cols.append(sc)
        sc = jnp.concatenate(cols, axis=1)
        if masked:
            sc = jnp.where(_visible(r0, ks, tq, tk), sc, NEG_INF)
        s_sc[kb] = sc

    def score_full(kb, c):
        score_step(kb, False)
        return c

    def score_masked(kb, c):
        score_step(kb, True)
        return c

    lax.fori_loop(0, n_full, score_full, 0)
    lax.fori_loop(n_full, n_vis, score_masked, 0)

    nl = tk // LANES

    def search(sb, flag):
        rsl = pl.ds(pl.multiple_of(sb * rs, rs), rs)

        def fold(fn, init):
            def body(kb, part):
                s = s_sc[kb, rsl, :]
                for j in range(nl):
                    part = fn(part, s[:, j * LANES:(j + 1) * LANES])
                return part
            return lax.fori_loop(0, n_vis, body, init)

        def bc(x):
            return jnp.broadcast_to(x, (rs, LANES))

        def count(cmp, x):
            xb = bc(x)
            part = fold(lambda p, sj: p + jnp.where(cmp(sj, xb), 1.0, 0.0), jnp.zeros((rs, LANES), F32))
            return jnp.sum(part, axis=1, keepdims=True)

        def max_below(x, strict):
            xb = bc(x)
            if strict:
                part = fold(lambda p, sj: jnp.maximum(p, jnp.where(sj < xb, sj, BELOW_NEG)),
                            jnp.full((rs, LANES), BELOW_NEG, F32))
            else:
                part = fold(lambda p, sj: jnp.maximum(p, jnp.where(sj <= xb, sj, BELOW_NEG)),
                            jnp.full((rs, LANES), BELOW_NEG, F32))
            return jnp.max(part, axis=1, keepdims=True)

        ge = lambda a, b: a >= b
        gt = lambda a, b: a > b

        hi = jnp.max(fold(jnp.maximum, jnp.full((rs, LANES), BELOW_NEG, F32)), axis=1, keepdims=True)
        lo = jnp.min(fold(lambda p, sj: jnp.minimum(p, jnp.where(sj > HALF_NEG, sj, BIG_POS)),
                          jnp.full((rs, LANES), BIG_POS, F32)), axis=1, keepdims=True)
        rpos = r0 + sb * rs + lax.broadcasted_iota(jnp.int32, (rs, 1), 0)
        n_valid = jnp.minimum(((rpos >> 6) + 1) << 6, n_keys)
        small = n_valid < k_sel

        def bisect(_, c):
            lo_c, hi_c = c
            mid = 0.5 * (lo_c + hi_c)
            up = count(ge, mid) >= ksel_f
            return jnp.where(up, mid, lo_c), jnp.where(up, hi_c, mid)

        lo, hi = lax.fori_loop(0, n_bisect, bisect, (lo, hi))

        v0 = max_below(hi, False)
        g0 = count(ge, v0)
        done0 = jnp.where(jnp.logical_or(g0 >= ksel_f, small), 1.0, 0.0)

        def walk_cond(c):
            return jnp.min(c[2]) < 0.5

        def walk_body(c):
            t_c, g_c, done_c = c
            v = max_below(t_c, True)
            g = count(ge, v)
            keep = done_c > 0.5
            return (jnp.where(keep, t_c, v), jnp.where(keep, g_c, g),
                    jnp.where(jnp.logical_or(keep, g >= ksel_f), 1.0, 0.0))

        t, g_t, _ = lax.while_loop(walk_cond, walk_body, (v0, g0, done0))
        c_gt = count(gt, t)
        need = ksel_f - c_gt
        excess = jnp.logical_and(jnp.logical_and(g_t - c_gt > need, jnp.logical_not(small)), t > HALF_NEG)
        t_sc[rsl, :] = jnp.where(small, NEG_INF, t)
        need_sc[rsl, :] = need
        return jnp.maximum(flag, jnp.max(jnp.where(excess, 1.0, 0.0)))

    tie_flag = lax.fori_loop(0, tq // rs, search, jnp.float32(0.0))

    q = q_ref[...]
    q0a, q0b = _split_halves(q[:, :LANES], lane)
    q1a, q1b = _split_halves(q[:, LANES:], lane)
    qq = (jnp.concatenate([q0a, q0b], axis=0), jnp.concatenate([q1a, q1b], axis=0))
    carry_sc[...] = jnp.zeros(carry_sc.shape, F32)
    t_all = t_sc[...]

    def bias_step(kb, masked):
        ks = pl.multiple_of(kb * tk, tk)
        sc = s_sc[kb]
        vis = _visible(r0, ks, tq, tk) if masked else None

        def store(sel):
            if masked:
                sel = jnp.logical_and(sel, vis)
            s_sc[kb] = jnp.where(sel, 0.0, NEG_INF)

        def plain():
            store(sc >= t_all)

        def with_ties():
            eq = sc == t_all
            if masked:
                eq = jnp.logical_and(eq, vis)
            eqf = jnp.where(eq, 1.0, 0.0)
            upper = (lax.broadcasted_iota(jnp.int32, (tk, tk), 0)
                     < lax.broadcasted_iota(jnp.int32, (tk, tk), 1))
            before = _bdot(eqf.astype(BF16), jnp.where(upper, 1.0, 0.0).astype(BF16)) + carry_sc[...]
            carry_sc[...] = carry_sc[...] + jnp.sum(eqf, axis=1, keepdims=True)
            store(jnp.logical_or(sc > t_all, jnp.logical_and(eq, before < need_sc[...])))

        lax.cond(tie_flag > 0.5, with_ties, plain)

    def bias_full(kb, c):
        bias_step(kb, False)
        return c

    def bias_masked(kb, c):
        bias_step(kb, True)
        return c

    lax.fori_loop(0, n_full, bias_full, 0)
    lax.fori_loop(n_full, n_vis, bias_masked, 0)

    def score(kb, masked, g):
        ks = pl.multiple_of(kb * tk, tk)
        bias = s_sc[kb]
        s = _dot_nt(qq[g], k_ref[pl.ds(ks, tk), g * LANES:(g + 1) * LANES])
        return s + jnp.concatenate([bias, bias], axis=0)

    def values(kb, g):
        return v_ref[pl.ds(pl.multiple_of(kb * tk, tk), tk), g * LANES:(g + 1) * LANES]

    o = _two_pass(n_full, n_vis, score, values, m_sc, l_sc, acc_sc,
                  (slice(0, 2 * tq), slice(2 * tq, 4 * tq)))
    for pr in range(2):
        lo_h = o[(2 * pr) * tq:(2 * pr + 1) * tq]
        hi_h = o[(2 * pr + 1) * tq:(2 * pr + 2) * tq]
        o_ref[:, pr * LANES:(pr + 1) * LANES] = jnp.where(lane < 64, lo_h, hi_h).astype(BF16)


def _dsa_call(q, qi, w, k, v, ki, *, nb, nq, tq, tk, lp, q_blk0, k_blk0, q_pos0, n_keys, n_bisect):
    nkb = lp // tk
    k_sel = min(DSA_TOPK, n_keys // 4)
    rs = min(64, tq)
    kern = functools.partial(_dsa_kernel, tq=tq, tk=tk, nkb=nkb, q_pos0=q_pos0, n_keys=n_keys,
                             k_sel=k_sel, rs=rs, n_bisect=n_bisect)
    return pl.pallas_call(
        kern,
        grid=(nb, nq),
        in_specs=[pl.BlockSpec((tq, B_WIDTH), lambda b, i: (q_blk0 + b * nq + i, 0)),
                  pl.BlockSpec((tq, B_WIDTH), lambda b, i: (q_blk0 + b * nq + i, 0)),
                  pl.BlockSpec((tq, LANES), lambda b, i: (q_blk0 + b * nq + i, 0)),
                  pl.BlockSpec((lp, B_WIDTH), lambda b, i: (k_blk0 + b, 0)),
                  pl.BlockSpec((lp, B_WIDTH), lambda b, i: (k_blk0 + b, 0)),
                  pl.BlockSpec((lp, LANES), lambda b, i: (k_blk0 + b, 0))],
        out_specs=pl.BlockSpec((tq, B_WIDTH), lambda b, i: (b * nq + i, 0)),
        out_shape=jax.ShapeDtypeStruct((nb * nq * tq, B_WIDTH), BF16),
        scratch_shapes=[pltpu.VMEM((nkb, tq, tk), F32),
                        pltpu.VMEM((4 * tq, LANES), F32), pltpu.VMEM((4 * tq, LANES), F32),
                        pltpu.VMEM((4 * tq, LANES), F32),
                        pltpu.VMEM((tq, 1), F32), pltpu.VMEM((tq, 1), F32), pltpu.VMEM((tq, 1), F32),
                        pltpu.VMEM((IDX_HEADS, tq, LANES), F32)],
        compiler_params=_cparams(("arbitrary", "arbitrary")),
        name="dsa_attention",
    )(q, qi, w, k, v, ki)


def _kb_kernel(x_ref, oa_ref, ob_ref, oc_ref, sha_ref, sca_ref, gta_ref, shm_ref, scm_ref,
               nmix_ref, nffn_ref, wg_ref, wba_ref, wbb_ref, wbc_ref, wout_ref, rw_ref, rb_ref,
               x1_o, h2_o, e_o, g_o, *, tm, ch):
    x = x_ref[...]
    hb = _modulate(_rms(x, nmix_ref[...]), sca_ref[...], sha_ref[...], tm, ch).astype(BF16)

    def gate(c0):
        return _sigmoid(_bdot(hb, wg_ref[:, c0:c0 + D_MODEL]))

    merged = gate(0) * _bdot(oa_ref[...], wba_ref[...])
    merged = merged + gate(D_MODEL) * _bdot(ob_ref[...], wbb_ref[...])
    merged = merged + gate(2 * D_MODEL) * _bdot(oc_ref[...], wbc_ref[...])
    y = _bdot(merged.astype(BF16), wout_ref[...])
    x1 = x + _scale_rows(y, gta_ref[...], tm, ch)
    x1_o[...] = x1
    h2 = _modulate(_rms(x1, nffn_ref[...]), scm_ref[...], shm_ref[...], tm, ch)
    h2_o[...] = h2.astype(BF16)
    lg = _dot_split(h2, rw_ref[...]) + rb_ref[...]
    lanef = lax.broadcasted_iota(jnp.int32, (tm, LANES), 1).astype(F32)
    e_acc = jnp.zeros((tm, LANES), F32)
    v_acc = jnp.full((tm, LANES), NEG_INF, F32)
    for k in range(TOP_K):
        mx = jnp.max(lg, axis=1, keepdims=True)
        idx = jnp.min(jnp.where(lg == mx, lanef, float(LANES)), axis=1, keepdims=True)
        e_acc = jnp.where(lanef == float(k), idx, e_acc)
        v_acc = jnp.where(lanef == float(k), mx, v_acc)
        lg = jnp.where(lanef == idx, BELOW_NEG, lg)
    ex = jnp.where(lanef < float(TOP_K), jnp.exp(v_acc - jnp.max(v_acc, axis=1, keepdims=True)), 0.0)
    e_o[...] = e_acc.astype(jnp.int32)
    g_o[...] = ex / jnp.sum(ex, axis=1, keepdims=True)


def _kb_call(x, oa, ob, oc, mods, nmix, nffn, wg, wba, wbb, wbc, wout, rw, rb):
    n = x.shape[0]
    tm = TOKEN_BLOCK
    ch = tm // MOD_ROWS

    def row(i):
        return (i, 0)

    def full(a):
        return pl.BlockSpec(a.shape, lambda i: (0, 0))

    in_specs = ([pl.BlockSpec((tm, D_MODEL), row), pl.BlockSpec((tm, A_WIDTH), row),
                 pl.BlockSpec((tm, B_WIDTH), row), pl.BlockSpec((tm, 512), row)]
                + [pl.BlockSpec((MOD_ROWS, D_MODEL), row) for _ in range(5)]
                + [full(a) for a in (nmix, nffn, wg, wba, wbb, wbc, wout, rw, rb)])
    out_shape = [jax.ShapeDtypeStruct((n, D_MODEL), F32), jax.ShapeDtypeStruct((n, D_MODEL), BF16),
                 jax.ShapeDtypeStruct((n, LANES), jnp.int32), jax.ShapeDtypeStruct((n, LANES), F32)]
    out_specs = [pl.BlockSpec((tm, D_MODEL), row), pl.BlockSpec((tm, D_MODEL), row),
                 pl.BlockSpec((tm, LANES), row), pl.BlockSpec((tm, LANES), row)]
    return pl.pallas_call(
        functools.partial(_kb_kernel, tm=tm, ch=ch),
        grid=(n // tm,),
        in_specs=in_specs,
        out_specs=out_specs,
        out_shape=out_shape,
        compiler_params=_cparams(("arbitrary",)),
        name="post_attention",
    )(x, oa, ob, oc, *mods, nmix, nffn, wg, wba, wbb, wbc, wout, rw, rb)


def _ke_kernel(be_ref, nu_ref, x_ref, wg_ref, wl_ref, wd_ref, bg_ref, bl_ref, bd_ref, y_ref):
    i = pl.program_id(0)

    @pl.when(i < nu_ref[0])
    def _():
        x = x_ref[...]
        g = jnp.minimum(_bdot(x, wg_ref[0]) + bg_ref[0], SWIGLU_LIMIT)
        l = jnp.clip(_bdot(x, wl_ref[0]) + bl_ref[0], -SWIGLU_LIMIT, SWIGLU_LIMIT)
        act = g * _sigmoid(SWIGLU_ALPHA * g) * (l + 1.0)
        y_ref[...] = _bdot(act.astype(BF16), wd_ref[0]) + bd_ref[0]

    @pl.when(i >= nu_ref[0])
    def _():
        y_ref[...] = jnp.zeros(y_ref.shape, F32)


def _ke_call(blk_e, n_used, xg, wg, wl, wd, bg, bl, bd):
    n_rows = xg.shape[0]
    eb = EXPERT_ROWS
    n_blocks = n_rows // eb
    grid_spec = pltpu.PrefetchScalarGridSpec(
        num_scalar_prefetch=2,
        grid=(n_blocks,),
        in_specs=[pl.BlockSpec((eb, D_MODEL), lambda i, be, nu: (i, 0)),
                  pl.BlockSpec((1, D_MODEL, D_FF), lambda i, be, nu: (be[i], 0, 0)),
                  pl.BlockSpec((1, D_MODEL, D_FF), lambda i, be, nu: (be[i], 0, 0)),
                  pl.BlockSpec((1, D_FF, D_MODEL), lambda i, be, nu: (be[i], 0, 0)),
                  pl.BlockSpec((1, 1, D_FF), lambda i, be, nu: (be[i], 0, 0)),
                  pl.BlockSpec((1, 1, D_FF), lambda i, be, nu: (be[i], 0, 0)),
                  pl.BlockSpec((1, 1, D_MODEL), lambda i, be, nu: (be[i], 0, 0))],
        out_specs=pl.BlockSpec((eb, D_MODEL), lambda i, be, nu: (i, 0)),
    )
    return pl.pallas_call(
        _ke_kernel,
        grid_spec=grid_spec,
        out_shape=jax.ShapeDtypeStruct((n_rows, D_MODEL), F32),
        compiler_params=_cparams(("arbitrary",)),
        name="moe_experts",
    )(blk_e, n_used, xg, wg, wl, wd, bg, bl, bd)


def _kc_kernel(x1_ref, yg_ref, g_ref, gtm_ref, fn_ref, x2_o, y_o, *, tm, ch):
    gate = g_ref[...]
    ffn = gate[:, 0:1] * yg_ref[0]
    for k in range(1, TOP_K):
        ffn = ffn + gate[:, k:k + 1] * yg_ref[k]
    x2 = x1_ref[...] + _scale_rows(ffn, gtm_ref[...], tm, ch)
    x2_o[...] = x2
    y_o[...] = _rms(x2, fn_ref[...])


def _kc_call(x1, yg, gate, gtm, fnorm):
    n = x1.shape[0]
    tm = TOKEN_BLOCK
    ch = tm // MOD_ROWS
    return pl.pallas_call(
        functools.partial(_kc_kernel, tm=tm, ch=ch),
        grid=(n // tm,),
        in_specs=[pl.BlockSpec((tm, D_MODEL), lambda i: (i, 0)),
                  pl.BlockSpec((TOP_K, tm, D_MODEL), lambda i: (0, i, 0)),
                  pl.BlockSpec((tm, LANES), lambda i: (i, 0)),
                  pl.BlockSpec((MOD_ROWS, D_MODEL), lambda i: (i, 0)),
                  pl.BlockSpec((1, D_MODEL), lambda i: (0, 0))],
        out_specs=[pl.BlockSpec((tm, D_MODEL), lambda i: (i, 0)), pl.BlockSpec((tm, D_MODEL), lambda i: (i, 0))],
        out_shape=[jax.ShapeDtypeStruct((n, D_MODEL), F32), jax.ShapeDtypeStruct((n, D_MODEL), F32)],
        compiler_params=_cparams(("arbitrary",)),
        name="moe_combine",
    )(x1, yg, gate, gtm, fnorm)


def _rope_tables(pos):
    lane = np.arange(LANES)
    inv32 = ROPE_THETA ** (-jnp.arange(32, dtype=F32) / 32)
    inv16 = ROPE_THETA ** (-jnp.arange(16, dtype=F32) / 16)
    ang64 = pos[:, None] * inv32[None, :][:, lane & 31]
    ang32 = pos[:, None] * inv16[None, :][:, lane & 15]
    sign64 = jnp.asarray(np.where((lane & 63) < 32, -1.0, 1.0), F32)[None, :]
    sign32 = jnp.asarray(np.where((lane & 31) < 16, -1.0, 1.0), F32)[None, :]
    in_m = jnp.asarray(lane < C_ROPE)[None, :]
    in_q = jnp.asarray((lane >= C_NOPE) & (lane < C_NOPE + C_ROPE))[None, :]
    cos64, sin64 = jnp.cos(ang64), jnp.sin(ang64) * sign64
    cos32, sin32 = jnp.cos(ang32), jnp.sin(ang32) * sign32
    return (cos64, sin64,
            jnp.where(in_m, cos32, 1.0), jnp.where(in_m, sin32, 0.0),
            jnp.where(in_q, cos32, 1.0), jnp.where(in_q, sin32, 0.0))


def _layer_weights(l, w_in, mla_w_uq, mla_w_ukv, w_br_c, router_w, router_b, exp_w_gu, exp_b_gu):
    wi = w_in[l]
    z = lambda n: jnp.zeros((D_MODEL, n), F32)
    w1 = jnp.concatenate([
        wi[:, _OFF_AQ:_OFF_BIK],
        wi[:, _OFF_BIK:_OFF_BIW], wi[:, _OFF_BIK:_OFF_BIW],
        wi[:, _OFF_CKR:_OFF_GATES], wi[:, _OFF_BIW:_OFF_CQ], z(LANES - C_ROPE - IDX_HEADS),
        wi[:, _OFF_CQ:_OFF_CKV], wi[:, _OFF_CKV:_OFF_CKR]], axis=1).astype(BF16)
    wg = wi[:, _OFF_GATES:].astype(BF16)
    wuq = mla_w_uq[l].reshape(Q_LORA, C_HEADS, C_NOPE + C_ROPE)
    wuq = jnp.pad(wuq, ((0, 0), (0, 0), (0, LANES - C_NOPE - C_ROPE))).reshape(Q_LORA, C_HEADS * LANES).astype(BF16)
    wukv = mla_w_ukv[l].reshape(KV_LORA, C_HEADS, C_NOPE + C_V)
    wk = jnp.pad(wukv[:, :, :C_NOPE], ((0, 0), (0, 0), (0, LANES - C_NOPE))).reshape(KV_LORA, C_HEADS * LANES).astype(BF16)
    wv = jnp.pad(wukv[:, :, C_NOPE:], ((0, 0), (0, 0), (0, LANES - C_V))).reshape(KV_LORA, C_HEADS * LANES).astype(BF16)
    e = np.zeros((LANES, C_HEADS * LANES), np.float32)
    for h in range(C_HEADS):
        e[np.arange(C_ROPE), h * LANES + C_NOPE + np.arange(C_ROPE)] = 1.0
    we = jnp.asarray(e, BF16)
    wbc = jnp.pad(w_br_c[l].reshape(C_HEADS, C_V, D_MODEL), ((0, 0), (0, LANES - C_V), (0, 0)))
    wbc = wbc.reshape(C_HEADS * LANES, D_MODEL).astype(BF16)
    rw = jnp.pad(router_w[l], ((0, 0), (0, LANES - N_EXPERTS)))
    rb = jnp.pad(router_b[l], (0, LANES - N_EXPERTS), constant_values=NEG_INF).reshape(1, LANES)
    wgl = exp_w_gu[l].reshape(N_EXPERTS, D_MODEL, D_FF, 2)
    bgl = exp_b_gu[l].reshape(N_EXPERTS, 1, D_FF, 2)
    return dict(w1=w1, wg=wg, wuq=wuq, wk=wk, wv=wv, we=we, wbc=wbc, rw=rw, rb=rb,
                e_wg=wgl[..., 0].astype(BF16), e_wl=wgl[..., 1].astype(BF16),
                e_bg=bgl[..., 0], e_bl=bgl[..., 1])


def _route(e_pad, n_rows_pad):
    eb = EXPERT_ROWS
    flat_e = e_pad[:, :TOP_K].reshape(-1)
    nk = flat_e.shape[0]
    onehot = (flat_e[:, None] == jnp.arange(N_EXPERTS, dtype=jnp.int32)[None, :]).astype(jnp.int32)
    csum = jnp.cumsum(onehot, axis=0)
    rank = jnp.take_along_axis(csum, flat_e[:, None], axis=1)[:, 0] - 1
    counts = csum[-1]
    padded = (counts + eb - 1) // eb * eb
    pad_end = jnp.cumsum(padded)
    pad_start = pad_end - padded
    dest = pad_start[flat_e] + rank
    row_tok = jnp.zeros((n_rows_pad,), jnp.int32).at[dest].set(jnp.arange(nk, dtype=jnp.int32) // TOP_K)
    n_blocks = n_rows_pad // eb
    blk_start = jnp.arange(n_blocks, dtype=jnp.int32) * eb
    blk_e = jnp.minimum(jnp.searchsorted(pad_end, blk_start, side='right'), N_EXPERTS - 1).astype(jnp.int32)
    n_used = (pad_end[-1] // eb).astype(jnp.int32).reshape(1)
    return dest, row_tok, blk_e, n_used


def kernel(x_prompt, x_sample, cache_a_k, cache_a_v, cache_b_k, cache_b_v, cache_b_idx_k, cache_c_latent, cache_c_k_rope, c_prompt, c_sample, w_ada, b_ada, norm_mix, norm_ffn, w_in, diff_lq1, diff_lk1, diff_lq2, diff_lk2, diff_subln, mla_q_norm, mla_w_uq, mla_kv_norm, mla_w_ukv, w_br_a, w_br_b, w_br_c, w_out, router_w, router_b, exp_w_gu, exp_b_gu, exp_w_down, exp_b_down, final_norm):
    depth = w_ada.shape[0]
    bp, tp, _ = x_prompt.shape
    bs, ts, _ = x_sample.shape
    past = cache_c_latent.shape[2]
    n_p, n_s = bp * tp, bs * ts
    n = n_p + n_s
    tm = TOKEN_BLOCK
    ch = tm // MOD_ROWS
    assert ts == CHUNK and tp % tm == 0 and n_s % tm == 0 and past % CHUNK == 0

    x = jnp.concatenate([x_prompt.reshape(n_p, D_MODEL), x_sample.reshape(n_s, D_MODEL)], axis=0)

    n_seq = bp + bs
    c_all = jnp.concatenate([c_prompt, c_sample], axis=0)
    c_pad = jnp.pad(c_all, ((0, (-n_seq) % 8), (0, 0)))
    mod = _ada_call(c_pad, w_ada, b_ada)
    def per_chunk(m, reps):
        return jnp.broadcast_to(m[:, :, None, :], m.shape[:2] + (reps, m.shape[2])).reshape(depth, -1, m.shape[2])

    mod_rows = jnp.concatenate([per_chunk(mod[:, :bp], tp // ch), per_chunk(mod[:, bp:n_seq], ts // ch)],
                               axis=1)

    pos = jnp.concatenate([jnp.arange(tp, dtype=F32),
                           jnp.tile(past + jnp.arange(ts, dtype=F32), tm // ts)])
    tabs = _rope_tables(pos)

    tq_p = min(256, tp)
    tk_p = min(512, tp)
    nq_p = tp // tq_p
    tq_c = min(512, tp)
    l_s = past + ts
    tk_s = 384
    lp_s = -(-l_s // tk_s) * tk_s
    n_rows_pad = -(-(n * TOP_K + N_EXPERTS * (EXPERT_ROWS - 1)) // EXPERT_ROWS) * EXPERT_ROWS

    def with_cache(cache_l, new, width):
        parts = [cache_l.reshape(bs, past, width).astype(BF16), new.reshape(bs, ts, width)]
        if lp_s > l_s:
            parts.append(jnp.zeros((bs, lp_s - l_s, width), BF16))
        return jnp.concatenate(parts, axis=1).reshape(bs * lp_s, width)

    caches = [[] for _ in range(7)]
    y = None
    for l in range(depth):
        lam_init = 0.8 - 0.6 * math.exp(-0.3 * l)
        wl = _layer_weights(l, w_in, mla_w_uq, mla_w_ukv, w_br_c, router_w, router_b, exp_w_gu, exp_b_gu)
        m6 = [mod_rows[l, :, j * D_MODEL:(j + 1) * D_MODEL] for j in range(6)]
        sh_a, sc_a, gt_a, sh_m, sc_m, gt_m = m6
        nmix = norm_mix[l].reshape(1, D_MODEL)
        nffn = norm_ffn[l].reshape(1, D_MODEL)

        (ak, av, bk, bv, bik, clat, ckr, misc,
         aq_b, ak_b, av_b, bq_b, bk_b, bv_b, biq_b, bik2_b, cq_b, ck_b, cv_b) = _ka_call(
            x, sh_a, sc_a, tabs, nmix, wl['w1'], mla_q_norm[l].reshape(1, Q_LORA), wl['wuq'],
            mla_kv_norm[l].reshape(1, KV_LORA), wl['wk'], wl['we'], wl['wv'],
            n_prompt_blocks=n_p // tm, tab_blocks=tp // tm)
        for i, a in enumerate((ak, av, bk, bv, bik, clat, ckr)):
            caches[i].append(a)

        lqk = jnp.pad(jnp.stack([diff_lq1[l], diff_lk1[l], diff_lq2[l], diff_lk2[l]]),
                      ((0, 4), (0, LANES - A_HD)))
        sub = diff_subln[l].reshape(1, 2 * A_HD)
        oa_p = _diff_call(lqk, sub, aq_b, ak_b, av_b, nb=bp, nq=nq_p, tq=tq_p, tk=tk_p, lp=tp,
                          q_blk0=0, k_blk0=0, q_pos0=0, lam_init=lam_init)
        ak_s = with_cache(cache_a_k[l], ak_b[n_p:], A_WIDTH)
        av_s = with_cache(cache_a_v[l], av_b[n_p:], A_WIDTH)
        oa_s = _diff_call(lqk, sub, aq_b, ak_s, av_s, nb=bs, nq=1, tq=ts, tk=tk_s, lp=lp_s,
                          q_blk0=n_p // ts, k_blk0=0, q_pos0=past, lam_init=lam_init)
        ob_p = _dsa_call(bq_b, biq_b, misc, bk_b, bv_b, bik2_b, nb=bp, nq=nq_p, tq=tq_p, tk=tk_p, lp=tp,
                         q_blk0=0, k_blk0=0, q_pos0=0, n_keys=tp, n_bisect=14)
        bk_s = with_cache(cache_b_k[l], bk_b[n_p:], B_WIDTH)
        bv_s = with_cache(cache_b_v[l], bv_b[n_p:], B_WIDTH)
        cik = cache_b_idx_k[l].reshape(bs, past, IDX_DIM)
        bik_s = with_cache(jnp.concatenate([cik, cik], axis=-1), bik2_b[n_p:], LANES)
        ob_s = _dsa_call(bq_b, biq_b, misc, bk_s, bv_s, bik_s, nb=bs, nq=1, tq=ts, tk=tk_s, lp=lp_s,
                         q_blk0=n_p // ts, k_blk0=0, q_pos0=past, n_keys=l_s, n_bisect=12)
        oc_p = _mla_call(cq_b, ck_b, cv_b, nb=bp, nq=tp // tq_c, tq=tq_c, tk=tk_p, lp=tp,
                         q_blk0=0, k_blk0=0, q_pos0=0)
        lat_c = cache_c_latent[l].reshape(bs * past, KV_LORA)
        kr_c = jnp.pad(cache_c_k_rope[l].reshape(bs * past, C_ROPE), ((0, 0), (0, LANES - C_ROPE)))
        ck_c, cv_c = _mla_kv_call(lat_c, kr_c, wl['wk'], wl['we'], wl['wv'])
        ck_s = with_cache(ck_c, ck_b[n_p:], 512)
        cv_s = with_cache(cv_c, cv_b[n_p:], 512)
        oc_s = _mla_call(cq_b, ck_s, cv_s, nb=bs, nq=1, tq=ts, tk=tk_s, lp=lp_s,
                         q_blk0=n_p // ts, k_blk0=0, q_pos0=past)

        oa = jnp.concatenate([oa_p, oa_s], axis=0)
        ob = jnp.concatenate([ob_p, ob_s], axis=0)
        oc = jnp.concatenate([oc_p, oc_s], axis=0)

        x1, h2, e_pad, g_pad = _kb_call(
            x, oa, ob, oc, (sh_a, sc_a, gt_a, sh_m, sc_m), nmix, nffn, wl['wg'],
            w_br_a[l].astype(BF16), w_br_b[l].astype(BF16), wl['wbc'], w_out[l].astype(BF16),
            wl['rw'], wl['rb'])

        dest, row_tok, blk_e, n_used = _route(e_pad, n_rows_pad)
        xg = jnp.take(h2, row_tok, axis=0)
        yr = _ke_call(blk_e, n_used, xg, wl['e_wg'], wl['e_wl'], exp_w_down[l].astype(BF16),
                      wl['e_bg'], wl['e_bl'], exp_b_down[l].reshape(N_EXPERTS, 1, D_MODEL))
        yg = jnp.take(yr, dest.reshape(n, TOP_K).T, axis=0)
        x, y = _kc_call(x1, yg, g_pad, gt_m, final_norm.reshape(1, D_MODEL))

    def split(a, tail):
        a = jnp.stack(a, axis=0)
        return (a[:, :n_p].reshape((depth, bp, tp) + tail), a[:, n_p:].reshape((depth, bs, ts) + tail))

    tails = ((A_HEADS, 2 * A_HD), (A_HEADS, 2 * A_HD), (B_HEADS, B_HD), (B_HEADS, B_HD),
             (IDX_DIM,), (KV_LORA,), (C_ROPE,))
    ps = [split(c, t) for c, t in zip(caches, tails)]
    y_prompt = y[:n_p].reshape(bp, tp, D_MODEL)
    y_sample = y[n_p:].reshape(bs, ts, D_MODEL)
    return (y_prompt, y_sample) + tuple(p[0] for p in ps) + tuple(p[1] for p in ps)
```

```python
import functools
import math

import numpy as np
import jax
import jax.numpy as jnp
from jax import lax
from jax.experimental import pallas as pl
from jax.experimental.pallas import tpu as pltpu

F32 = jnp.float32
BF16 = jnp.bfloat16

D_MODEL = 1024
CHUNK = 64
ROPE_THETA = 10000.0
NORM_EPS = 1e-6
NEG_INF = -1e30
HALF_NEG = -5e29
BELOW_NEG = -3e38
BIG_POS = 3e38
LOG2E = 1.4426950408889634

A_HEADS, A_HD = 4, 64
B_HEADS, B_HD = 4, 64
IDX_HEADS, IDX_DIM = 4, 64
DSA_TOPK = 256
C_HEADS, C_NOPE, C_ROPE, C_V = 4, 64, 32, 64
Q_LORA, KV_LORA = 256, 128
N_EXPERTS, TOP_K = 32, 4
D_FF = D_MODEL
SWIGLU_LIMIT = 7.0
SWIGLU_ALPHA = 1.702

A_WIDTH = A_HEADS * 2 * A_HD
B_WIDTH = B_HEADS * B_HD
LANES = 128

_OFF_AQ, _OFF_AK, _OFF_AV = 0, 512, 1024
_OFF_BQ, _OFF_BK, _OFF_BV = 1536, 1792, 2048
_OFF_BIQ, _OFF_BIK, _OFF_BIW = 2304, 2560, 2624
_OFF_CQ, _OFF_CKV, _OFF_CKR = 2628, 2884, 3012
_OFF_GATES = 3044
_P_AQ, _P_AK, _P_AV = 0, 512, 1024
_P_BQ, _P_BK, _P_BV = 1536, 1792, 2048
_P_BIQ, _P_BIK2, _P_MISC, _P_CQ, _P_CKV, _P_END = 2304, 2560, 2688, 2816, 3072, 3200
_MISC_BIW = 32

TOKEN_BLOCK = 256
MOD_ROWS = 8
EXPERT_ROWS = 256
VMEM_LIMIT = 56 * 1024 * 1024


def _cparams(sem, vmem=VMEM_LIMIT):
    return pltpu.CompilerParams(dimension_semantics=sem, vmem_limit_bytes=vmem)


def _rms(xf, g):
    return xf * lax.rsqrt(jnp.mean(xf * xf, axis=-1, keepdims=True) + NORM_EPS) * g


def _sigmoid(x):
    return 1.0 / (1.0 + jnp.exp(-x))


def _bdot(a, b):
    return jnp.dot(a, b, preferred_element_type=F32)


def _dot_nt(a, b):
    return lax.dot_general(a, b, (((1,), (1,)), ((), ())), preferred_element_type=F32)


def _dot_split(a, b):
    a_hi = a.astype(BF16)
    b_hi = b.astype(BF16)
    a_lo = (a - a_hi.astype(F32)).astype(BF16)
    b_lo = (b - b_hi.astype(F32)).astype(BF16)
    return _bdot(a_hi, b_hi) + (_bdot(a_hi, b_lo) + _bdot(a_lo, b_hi))


def _modulate(xn, sc, sh, rows, ch):
    n = rows // ch
    y = xn.reshape(n, ch, D_MODEL) * (1.0 + sc)[:, None, :] + sh[:, None, :]
    return y.reshape(rows, D_MODEL)


def _scale_rows(y, g, rows, ch):
    n = rows // ch
    return (y.reshape(n, ch, D_MODEL) * g[:, None, :]).reshape(rows, D_MODEL)


def _ada_kernel(c_ref, w_ref, b_ref, o_ref):
    c = c_ref[...]
    s = c * _sigmoid(c)
    o_ref[0] = _dot_split(s, w_ref[0]) + b_ref[0]


def _ada_call(c_pad, w_ada, b_ada):
    depth = w_ada.shape[0]
    mp = c_pad.shape[0]
    return pl.pallas_call(
        _ada_kernel,
        grid=(depth, 6),
        in_specs=[
            pl.BlockSpec((mp, D_MODEL), lambda l, j: (0, 0)),
            pl.BlockSpec((1, D_MODEL, D_MODEL), lambda l, j: (l, 0, j)),
            pl.BlockSpec((1, 1, D_MODEL), lambda l, j: (l, 0, j)),
        ],
        out_specs=pl.BlockSpec((1, mp, D_MODEL), lambda l, j: (l, 0, j)),
        out_shape=jax.ShapeDtypeStruct((depth, mp, 6 * D_MODEL), F32),
        compiler_params=_cparams(("arbitrary", "arbitrary")),
        name="ada_mod",
    )(c_pad, w_ada, b_ada.reshape(depth, 1, 6 * D_MODEL))


def _rope_partner(x, lane, half):
    first = (lane & (2 * half - 1)) < half
    return jnp.where(first, pltpu.roll(x, LANES - half, 1), pltpu.roll(x, half, 1))


def _ka_kernel(x_ref, sh_ref, sc_ref, cos_ref, sin_ref, cosm_ref, sinm_ref, cosq_ref, sinq_ref,
               nmix_ref, w1_ref, qn_ref, wuq_ref, kvn_ref, wk_ref, we_ref, wv_ref,
               ak_o, av_o, bk_o, bv_o, bik_o, clat_o, ckr_o, misc_o,
               aq_b, ak_b, av_b, bq_b, bk_b, bv_b, biq_b, bik2_b, cq_b, ck_b, cv_b, *, tm, ch):
    x = x_ref[...]
    h = _modulate(_rms(x, nmix_ref[...]), sc_ref[...], sh_ref[...], tm, ch)
    hb = h.astype(BF16)
    lane = lax.broadcasted_iota(jnp.int32, (tm, LANES), 1)
    cos = cos_ref[...]
    sin = sin_ref[...]

    def proj(c0, c1):
        return _bdot(hb, w1_ref[:, c0:c1])

    def rope64(xb):
        return xb * cos + _rope_partner(xb, lane, 32) * sin

    a_scale = (A_HD ** -0.5) * LOG2E
    b_scale = (B_HD ** -0.5) * LOG2E
    i_scale = IDX_DIM ** -0.5
    c_scale = ((C_NOPE + C_ROPE) ** -0.5) * LOG2E

    p = proj(_P_AQ, _P_AK)
    for c in range(A_WIDTH // LANES):
        sl = slice(c * LANES, (c + 1) * LANES)
        aq_b[:, sl] = (rope64(p[:, sl]) * a_scale).astype(BF16)
    p = proj(_P_AK, _P_AV)
    for c in range(A_WIDTH // LANES):
        sl = slice(c * LANES, (c + 1) * LANES)
        r = rope64(p[:, sl])
        ak_o[:, sl] = r
        ak_b[:, sl] = r.astype(BF16)
    p = proj(_P_AV, _P_BQ)
    av_o[...] = p
    av_b[...] = p.astype(BF16)
    p = proj(_P_BQ, _P_BK)
    for c in range(B_WIDTH // LANES):
        sl = slice(c * LANES, (c + 1) * LANES)
        bq_b[:, sl] = (rope64(p[:, sl]) * b_scale).astype(BF16)
    p = proj(_P_BK, _P_BV)
    for c in range(B_WIDTH // LANES):
        sl = slice(c * LANES, (c + 1) * LANES)
        r = rope64(p[:, sl])
        bk_o[:, sl] = r
        bk_b[:, sl] = r.astype(BF16)
    p = proj(_P_BV, _P_BIQ)
    bv_o[...] = p
    bv_b[...] = p.astype(BF16)
    p = proj(_P_BIQ, _P_BIK2)
    for c in range(B_WIDTH // LANES):
        sl = slice(c * LANES, (c + 1) * LANES)
        biq_b[:, sl] = (rope64(p[:, sl]) * i_scale).astype(BF16)
    r = rope64(proj(_P_BIK2, _P_MISC))
    bik_o[...] = r[:, :IDX_DIM]
    bik2_b[...] = r.astype(BF16)
    pm = proj(_P_MISC, _P_CQ)
    misc = pm * cosm_ref[...] + _rope_partner(pm, lane, 16) * sinm_ref[...]
    misc_o[...] = misc
    ckr_o[...] = misc[:, :C_ROPE]
    qlat = _rms(proj(_P_CQ, _P_CKV), qn_ref[...]).astype(BF16)
    cqf = _bdot(qlat, wuq_ref[...])
    cosq = cosq_ref[...]
    sinq = sinq_ref[...]
    for c in range(C_HEADS):
        sl = slice(c * LANES, (c + 1) * LANES)
        xb = cqf[:, sl]
        cq_b[:, sl] = ((xb * cosq + _rope_partner(xb, lane, 16) * sinq) * c_scale).astype(BF16)
    clat = _rms(proj(_P_CKV, _P_END), kvn_ref[...])
    clat_o[...] = clat
    clb = clat.astype(BF16)
    ck_b[...] = (_bdot(clb, wk_ref[...]) + _bdot(misc.astype(BF16), we_ref[...])).astype(BF16)
    cv_b[...] = _bdot(clb, wv_ref[...]).astype(BF16)


def _ka_call(x, sh, sc, tabs, nmix, w1, qn, wuq, kvn, wk, we, wv, *, n_prompt_blocks, tab_blocks):
    n = x.shape[0]
    tm = TOKEN_BLOCK
    ch = tm // MOD_ROWS
    nblk = n // tm

    def row(i):
        return (i, 0)

    def tab(i):
        return (jnp.where(i < n_prompt_blocks, i % tab_blocks, tab_blocks), 0)

    def const(i):
        return (0, 0)

    def full(a):
        return pl.BlockSpec(a.shape, const)

    widths_f32 = (A_WIDTH, A_WIDTH, B_WIDTH, B_WIDTH, IDX_DIM, KV_LORA, C_ROPE, LANES)
    widths_b16 = (A_WIDTH, A_WIDTH, A_WIDTH, B_WIDTH, B_WIDTH, B_WIDTH, B_WIDTH, LANES, 512, 512, 512)
    out_shape = ([jax.ShapeDtypeStruct((n, w), F32) for w in widths_f32]
                 + [jax.ShapeDtypeStruct((n, w), BF16) for w in widths_b16])
    out_specs = [pl.BlockSpec((tm, w), row) for w in widths_f32 + widths_b16]
    in_specs = ([pl.BlockSpec((tm, D_MODEL), row),
                 pl.BlockSpec((MOD_ROWS, D_MODEL), row),
                 pl.BlockSpec((MOD_ROWS, D_MODEL), row)]
                + [pl.BlockSpec((tm, LANES), tab) for _ in range(6)]
                + [full(a) for a in (nmix, w1, qn, wuq, kvn, wk, we, wv)])
    return pl.pallas_call(
        functools.partial(_ka_kernel, tm=tm, ch=ch),
        grid=(nblk,),
        in_specs=in_specs,
        out_specs=out_specs,
        out_shape=out_shape,
        compiler_params=_cparams(("arbitrary",)),
        name="pre_attention",
    )(x, sh, sc, *tabs, nmix, w1, qn, wuq, kvn, wk, we, wv)


def _mla_kv_kernel(lat_ref, kr_ref, wk_ref, we_ref, wv_ref, ck_o, cv_o):
    lb = lat_ref[...].astype(BF16)
    ck_o[...] = (_bdot(lb, wk_ref[...]) + _bdot(kr_ref[...].astype(BF16), we_ref[...])).astype(BF16)
    cv_o[...] = _bdot(lb, wv_ref[...]).astype(BF16)


def _mla_kv_call(lat, krp, wk, we, wv):
    n = lat.shape[0]
    tm = 512
    return pl.pallas_call(
        _mla_kv_kernel,
        grid=(n // tm,),
        in_specs=[pl.BlockSpec((tm, KV_LORA), lambda i: (i, 0)),
                  pl.BlockSpec((tm, LANES), lambda i: (i, 0)),
                  pl.BlockSpec(wk.shape, lambda i: (0, 0)),
                  pl.BlockSpec(we.shape, lambda i: (0, 0)),
                  pl.BlockSpec(wv.shape, lambda i: (0, 0))],
        out_specs=[pl.BlockSpec((tm, 512), lambda i: (i, 0)), pl.BlockSpec((tm, 512), lambda i: (i, 0))],
        out_shape=[jax.ShapeDtypeStruct((n, 512), BF16), jax.ShapeDtypeStruct((n, 512), BF16)],
        compiler_params=_cparams(("arbitrary",)),
        name="mla_cache_kv",
    )(lat, krp, wk, we, wv)


def _block_range(r0, tq, tk, nkb):
    n_full = jnp.minimum((r0 + CHUNK) // tk, nkb)
    n_vis = jnp.minimum((r0 + tq + tk - 1) // tk, nkb)
    return n_full, n_vis


def _visible(r0, ks, rows, tk):
    rpos = r0 + lax.broadcasted_iota(jnp.int32, (rows, tk), 0)
    kpos = ks + lax.broadcasted_iota(jnp.int32, (rows, tk), 1)
    return kpos < (((rpos >> 6) + 1) << 6)


def _fold_max(s, m_sc, rows):
    mp = m_sc[rows, :]
    for j in range(s.shape[1] // LANES):
        mp = jnp.maximum(mp, s[:, j * LANES:(j + 1) * LANES])
    m_sc[rows, :] = mp


def _finish_max(m_sc, rows):
    mp = m_sc[rows, :]
    m_sc[rows, :] = jnp.broadcast_to(jnp.max(mp, axis=1, keepdims=True), mp.shape)


def _accumulate(s, vblk, m_sc, l_sc, acc_sc, rows):
    m = m_sc[rows, :]
    lp = l_sc[rows, :]
    ps = []
    for j in range(s.shape[1] // LANES):
        pj = jnp.exp2(s[:, j * LANES:(j + 1) * LANES] - m)
        lp = lp + pj
        ps.append(pj.astype(BF16))
    l_sc[rows, :] = lp
    acc_sc[rows, :] = acc_sc[rows, :] + _bdot(jnp.concatenate(ps, axis=1), vblk)


def _two_pass(n_full, n_vis, score, values, m_sc, l_sc, acc_sc, sbuf, groups):
    m_sc[...] = jnp.full(m_sc.shape, NEG_INF, F32)
    l_sc[...] = jnp.zeros(l_sc.shape, F32)
    acc_sc[...] = jnp.zeros(acc_sc.shape, F32)

    for g, rows in enumerate(groups):
        def full_body(kb, c):
            s = score(kb, False, g)
            sbuf[kb] = s
            _fold_max(s, m_sc, rows)
            return c

        def masked_body(kb, c):
            s = score(kb, True, g)
            sbuf[kb] = s
            _fold_max(s, m_sc, rows)
            return c

        lax.fori_loop(0, n_full, full_body, 0)
        lax.fori_loop(n_full, n_vis, masked_body, 0)
        _finish_max(m_sc, rows)

        def acc_body(kb, c):
            _accumulate(sbuf[kb], values(kb, g), m_sc, l_sc, acc_sc, rows)
            return c

        lax.fori_loop(0, n_vis, acc_body, 0)
    return acc_sc[...] / jnp.sum(l_sc[...], axis=1, keepdims=True)


def _split_halves(q, lane=None):
    lane1 = lax.broadcasted_iota(jnp.int32, (1, LANES), 1)
    lo = jnp.where(lane1 < 64, 1.0, 0.0).astype(q.dtype)
    return q * lo, q * (1.0 - lo).astype(q.dtype)


def _diff_kernel(lqk_ref, sub_ref, q_ref, k_ref, v_ref, o_ref, m_sc, l_sc, acc_sc, sbuf,
                 *, tq, tk, nkb, q_pos0, lam_init):
    i = pl.program_id(2)
    r0 = q_pos0 + i * tq
    n_full, n_vis = _block_range(r0, tq, tk, nkb)
    lane = lax.broadcasted_iota(jnp.int32, (tq, LANES), 1)
    qq = jnp.concatenate(_split_halves(q_ref[...], lane), axis=0)

    def score(kb, masked, g):
        ks = pl.multiple_of(kb * tk, tk)
        s = _dot_nt(qq, k_ref[pl.ds(ks, tk), :])
        if masked:
            vis = _visible(r0, ks, tq, tk)
            s = jnp.where(jnp.concatenate([vis, vis], axis=0), s, NEG_INF)
        return s

    def values(kb, g):
        return v_ref[pl.ds(pl.multiple_of(kb * tk, tk), tk), :]

    o = _two_pass(n_full, n_vis, score, values, m_sc, l_sc, acc_sc, sbuf, (slice(0, 2 * tq),))
    o1 = o[:tq]
    o2 = o[tq:]
    lq = lqk_ref[...]
    lam = (jnp.exp(jnp.sum(lq[0:1] * lq[1:2], axis=1, keepdims=True))
           - jnp.exp(jnp.sum(lq[2:3] * lq[3:4], axis=1, keepdims=True)) + lam_init)
    o = o1 - lam * o2
    o = o * lax.rsqrt(jnp.mean(o * o, axis=1, keepdims=True) + NORM_EPS)
    o_ref[...] = (o * sub_ref[...] * (1.0 - lam_init)).astype(BF16)


def _diff_call(lqk, sub, q, k, v, *, nb, nq, tq, tk, lp, q_blk0, k_blk0, q_pos0, lam_init):
    nkb = lp // tk
    kern = functools.partial(_diff_kernel, tq=tq, tk=tk, nkb=nkb, q_pos0=q_pos0, lam_init=lam_init)
    return pl.pallas_call(
        kern,
        grid=(nb, A_HEADS, nq),
        in_specs=[pl.BlockSpec(lqk.shape, lambda b, h, i: (0, 0)),
                  pl.BlockSpec((1, LANES), lambda b, h, i: (0, 0)),
                  pl.BlockSpec((tq, LANES), lambda b, h, i: (q_blk0 + b * nq + i, h)),
                  pl.BlockSpec((lp, LANES), lambda b, h, i: (k_blk0 + b, h)),
                  pl.BlockSpec((lp, LANES), lambda b, h, i: (k_blk0 + b, h))],
        out_specs=pl.BlockSpec((tq, LANES), lambda b, h, i: (b * nq + i, h)),
        out_shape=jax.ShapeDtypeStruct((nb * nq * tq, A_WIDTH), BF16),
        scratch_shapes=([pltpu.VMEM((2 * tq, LANES), F32) for _ in range(3)]
                        + [pltpu.VMEM((nkb, 2 * tq, tk), F32)]),
        compiler_params=_cparams(("arbitrary", "arbitrary", "arbitrary")),
        name="diff_attention",
    )(lqk, sub, q, k, v)


def _mla_kernel(q_ref, k_ref, v_ref, o_ref, m_sc, l_sc, acc_sc, sbuf, *, tq, tk, nkb, q_pos0):
    i = pl.program_id(2)
    r0 = q_pos0 + i * tq
    n_full, n_vis = _block_range(r0, tq, tk, nkb)
    q = q_ref[...]

    def score(kb, masked, g):
        ks = pl.multiple_of(kb * tk, tk)
        s = _dot_nt(q, k_ref[pl.ds(ks, tk), :])
        if masked:
            s = jnp.where(_visible(r0, ks, tq, tk), s, NEG_INF)
        return s

    def values(kb, g):
        return v_ref[pl.ds(pl.multiple_of(kb * tk, tk), tk), :]

    o = _two_pass(n_full, n_vis, score, values, m_sc, l_sc, acc_sc, sbuf, (slice(0, tq),))
    o_ref[...] = o.astype(BF16)


def _mla_call(q, k, v, *, nb, nq, tq, tk, lp, q_blk0, k_blk0, q_pos0):
    nkb = lp // tk
    kern = functools.partial(_mla_kernel, tq=tq, tk=tk, nkb=nkb, q_pos0=q_pos0)
    return pl.pallas_call(
        kern,
        grid=(nb, C_HEADS, nq),
        in_specs=[pl.BlockSpec((tq, LANES), lambda b, h, i: (q_blk0 + b * nq + i, h)),
                  pl.BlockSpec((lp, LANES), lambda b, h, i: (k_blk0 + b, h)),
                  pl.BlockSpec((lp, LANES), lambda b, h, i: (k_blk0 + b, h))],
        out_specs=pl.BlockSpec((tq, LANES), lambda b, h, i: (b * nq + i, h)),
        out_shape=jax.ShapeDtypeStruct((nb * nq * tq, C_HEADS * LANES), BF16),
        scratch_shapes=([pltpu.VMEM((tq, LANES), F32) for _ in range(3)]
                        + [pltpu.VMEM((nkb, tq, tk), F32)]),
        compiler_params=_cparams(("arbitrary", "arbitrary", "arbitrary")),
        name="mla_attention",
    )(q, k, v)


def _dsa_kernel(q_ref, qi_ref, w_ref, k_ref, v_ref, ki_ref, o_ref,
                s_sc, m_sc, l_sc, acc_sc, t_sc, need_sc, carry_sc, wrep_sc, sbuf,
                *, tq, tk, nkb, q_pos0, n_keys, k_sel, rs, n_bisect):
    i = pl.program_id(1)
    r0 = q_pos0 + i * tq
    n_full, n_vis = _block_range(r0, tq, tk, nkb)
    lane = lax.broadcasted_iota(jnp.int32, (tq, LANES), 1)
    ksel_f = float(k_sel)

    qi = qi_ref[...]
    qa, qb = _split_halves(qi[:, :LANES], lane)
    qc, qd = _split_halves(qi[:, LANES:], lane)
    qi4 = jnp.concatenate([qa, qb, qc, qd], axis=0)
    wm = w_ref[...]
    for h in range(IDX_HEADS):
        wcol = wm[:, _MISC_BIW + h:_MISC_BIW + h + 1] * (IDX_HEADS ** -0.5)
        wrep_sc[h] = jnp.broadcast_to(wcol, (tq, LANES))

    def score_step(kb, masked):
        ks = pl.multiple_of(kb * tk, tk)
        rel = _dot_nt(qi4, ki_ref[pl.ds(ks, tk), :])
        cols = []
        for j in range(tk // LANES):
            cl = slice(j * LANES, (j + 1) * LANES)
            sc = wrep_sc[0] * jnp.maximum(rel[:tq, cl], 0.0)
            for h in range(1, IDX_HEADS):
                sc = sc + wrep_sc[h] * jnp.maximum(rel[h * tq:(h + 1) * tq, cl], 0.0)
            cols.append(sc)
        sc = jnp.concatenate(cols, axis=1)
        if masked:
            sc = jnp.where(_visible(r0, ks, tq, tk), sc, NEG_INF)
        s_sc[kb] = sc

    def score_full(kb, c):
        score_step(kb, False)
        return c

    def score_masked(kb, c):
        score_step(kb, True)
        return c

    lax.fori_loop(0, n_full, score_full, 0)
    lax.fori_loop(n_full, n_vis, score_masked, 0)

    nl = tk // LANES

    def search(sb, flag):
        rsl = pl.ds(pl.multiple_of(sb * rs, rs), rs)

        def fold(fn, init):
            def body(kb, part):
                s = s_sc[kb, rsl, :]
                for j in range(nl):
                    part = fn(part, s[:, j * LANES:(j + 1) * LANES])
                return part
            return lax.fori_loop(0, n_vis, body, init)

        def bc(x):
            return jnp.broadcast_to(x, (rs, LANES))

        def count(cmp, x):
            xb = bc(x)
            part = fold(lambda p, sj: p + jnp.where(cmp(sj, xb), 1.0, 0.0), jnp.zeros((rs, LANES), F32))
            return jnp.sum(part, axis=1, keepdims=True)

        def max_below(x, strict):
            xb = bc(x)
            if strict:
                part = fold(lambda p, sj: jnp.maximum(p, jnp.where(sj < xb, sj, BELOW_NEG)),
                            jnp.full((rs, LANES), BELOW_NEG, F32))
            else:
                part = fold(lambda p, sj: jnp.maximum(p, jnp.where(sj <= xb, sj, BELOW_NEG)),
                            jnp.full((rs, LANES), BELOW_NEG, F32))
            return jnp.max(part, axis=1, keepdims=True)

        ge = lambda a, b: a >= b
        gt = lambda a, b: a > b

        hi = jnp.max(fold(jnp.maximum, jnp.full((rs, LANES), BELOW_NEG, F32)), axis=1, keepdims=True)
        lo = jnp.min(fold(lambda p, sj: jnp.minimum(p, jnp.where(sj > HALF_NEG, sj, BIG_POS)),
                          jnp.full((rs, LANES), BIG_POS, F32)), axis=1, keepdims=True)
        rpos = r0 + sb * rs + lax.broadcasted_iota(jnp.int32, (rs, 1), 0)
        n_valid = jnp.minimum(((rpos >> 6) + 1) << 6, n_keys)
        small = n_valid < k_sel

        def bisect(_, c):
            lo_c, hi_c = c
            mid = 0.5 * (lo_c + hi_c)
            up = count(ge, mid) >= ksel_f
            return jnp.where(up, mid, lo_c), jnp.where(up, hi_c, mid)

        lo, hi = lax.fori_loop(0, n_bisect, bisect, (lo, hi))

        v0 = max_below(hi, False)
        g0 = count(ge, v0)
        done0 = jnp.where(jnp.logical_or(g0 >= ksel_f, small), 1.0, 0.0)

        def walk_cond(c):
            return jnp.min(c[2]) < 0.5

        def walk_body(c):
            t_c, g_c, done_c = c
            v = max_below(t_c, True)
            g = count(ge, v)
            keep = done_c > 0.5
            return (jnp.where(keep, t_c, v), jnp.where(keep, g_c, g),
                    jnp.where(jnp.logical_or(keep, g >= ksel_f), 1.0, 0.0))

        t, g_t, _ = lax.while_loop(walk_cond, walk_body, (v0, g0, done0))
        c_gt = count(gt, t)
        need = ksel_f - c_gt
        excess = jnp.logical_and(jnp.logical_and(g_t - c_gt > need, jnp.logical_not(small)), t > HALF_NEG)
        t_sc[rsl, :] = jnp.where(small, NEG_INF, t)
        need_sc[rsl, :] = need
        return jnp.maximum(flag, jnp.max(jnp.where(excess, 1.0, 0.0)))

    tie_flag = lax.fori_loop(0, tq // rs, search, jnp.float32(0.0))

    q = q_ref[...]
    q0a, q0b = _split_halves(q[:, :LANES], lane)
    q1a, q1b = _split_halves(q[:, LANES:], lane)
    qq = (jnp.concatenate([q0a, q0b], axis=0), jnp.concatenate([q1a, q1b], axis=0))
    carry_sc[...] = jnp.zeros(carry_sc.shape, F32)
    t_all = t_sc[...]

    def bias_step(kb, masked):
        ks = pl.multiple_of(kb * tk, tk)
        sc = s_sc[kb]
        vis = _visible(r0, ks, tq, tk) if masked else None

        def store(sel):
            if masked:
                sel = jnp.logical_and(sel, vis)
            s_sc[kb] = jnp.where(sel, 0.0, NEG_INF)

        def plain():
            store(sc >= t_all)

        def with_ties():
            eq = sc == t_all
            if masked:
                eq = jnp.logical_and(eq, vis)
            eqf = jnp.where(eq, 1.0, 0.0)
            upper = (lax.broadcasted_iota(jnp.int32, (tk, tk), 0)
                     < lax.broadcasted_iota(jnp.int32, (tk, tk), 1))
            before = _bdot(eqf.astype(BF16), jnp.where(upper, 1.0, 0.0).astype(BF16)) + carry_sc[...]
            carry_sc[...] = carry_sc[...] + jnp.sum(eqf, axis=1, keepdims=True)
            store(jnp.logical_or(sc > t_all, jnp.logical_and(eq, before < need_sc[...])))

        lax.cond(tie_flag > 0.5, with_ties, plain)

    def bias_full(kb, c):
        bias_step(kb, False)
        return c

    def bias_masked(kb, c):
        bias_step(kb, True)
        return c

    lax.fori_loop(0, n_full, bias_full, 0)
    lax.fori_loop(n_full, n_vis, bias_masked, 0)

    def score(kb, masked, g):
        ks = pl.multiple_of(kb * tk, tk)
        bias = s_sc[kb]
        s = _dot_nt(qq[g], k_ref[pl.ds(ks, tk), g * LANES:(g + 1) * LANES])
        return s + jnp.concatenate([bias, bias], axis=0)

    def values(kb, g):
        return v_ref[pl.ds(pl.multiple_of(kb * tk, tk), tk), g * LANES:(g + 1) * LANES]

    o = _two_pass(n_full, n_vis, score, values, m_sc, l_sc, acc_sc, sbuf,
                  (slice(0, 2 * tq), slice(2 * tq, 4 * tq)))
    for pr in range(2):
        lo_h = o[(2 * pr) * tq:(2 * pr + 1) * tq]
        hi_h = o[(2 * pr + 1) * tq:(2 * pr + 2) * tq]
        o_ref[:, pr * LANES:(pr + 1) * LANES] = jnp.where(lane < 64, lo_h, hi_h).astype(BF16)


def _dsa_call(q, qi, w, k, v, ki, *, nb, nq, tq, tk, lp, q_blk0, k_blk0, q_pos0, n_keys, n_bisect):
    nkb = lp // tk
    k_sel = min(DSA_TOPK, n_keys // 4)
    rs = min(64, tq)
    kern = functools.partial(_dsa_kernel, tq=tq, tk=tk, nkb=nkb, q_pos0=q_pos0, n_keys=n_keys,
                             k_sel=k_sel, rs=rs, n_bisect=n_bisect)
    return pl.pallas_call(
        kern,
        grid=(nb, nq),
        in_specs=[pl.BlockSpec((tq, B_WIDTH), lambda b, i: (q_blk0 + b * nq + i, 0)),
                  pl.BlockSpec((tq, B_WIDTH), lambda b, i: (q_blk0 + b * nq + i, 0)),
                  pl.BlockSpec((tq, LANES), lambda b, i: (q_blk0 + b * nq + i, 0)),
                  pl.BlockSpec((lp, B_WIDTH), lambda b, i: (k_blk0 + b, 0)),
                  pl.BlockSpec((lp, B_WIDTH), lambda b, i: (k_blk0 + b, 0)),
                  pl.BlockSpec((lp, LANES), lambda b, i: (k_blk0 + b, 0))],
        out_specs=pl.BlockSpec((tq, B_WIDTH), lambda b, i: (b * nq + i, 0)),
        out_shape=jax.ShapeDtypeStruct((nb * nq * tq, B_WIDTH), BF16),
        scratch_shapes=[pltpu.VMEM((nkb, tq, tk), F32),
                        pltpu.VMEM((4 * tq, LANES), F32), pltpu.VMEM((4 * tq, LANES), F32),
                        pltpu.VMEM((4 * tq, LANES), F32),
                        pltpu.VMEM((tq, 1), F32), pltpu.VMEM((tq, 1), F32), pltpu.VMEM((tq, 1), F32),
                        pltpu.VMEM((IDX_HEADS, tq, LANES), F32),
                        pltpu.VMEM((nkb, 2 * tq, tk), F32)],
        compiler_params=_cparams(("arbitrary", "arbitrary")),
        name="dsa_attention",
    )(q, qi, w, k, v, ki)


def _kb_kernel(x_ref, oa_ref, ob_ref, oc_ref, sha_ref, sca_ref, gta_ref, shm_ref, scm_ref,
               nmix_ref, nffn_ref, wg_ref, wba_ref, wbb_ref, wbc_ref, wout_ref, rw_ref, rb_ref,
               x1_o, h2_o, e_o, g_o, cnt_o, cnt_sc, *, tm, ch):
    x = x_ref[...]
    hb = _modulate(_rms(x, nmix_ref[...]), sca_ref[...], sha_ref[...], tm, ch).astype(BF16)

    def gate(c0):
        return _sigmoid(_bdot(hb, wg_ref[:, c0:c0 + D_MODEL]))

    merged = gate(0) * _bdot(oa_ref[...], wba_ref[...])
    merged = merged + gate(D_MODEL) * _bdot(ob_ref[...], wbb_ref[...])
    merged = merged + gate(2 * D_MODEL) * _bdot(oc_ref[...], wbc_ref[...])
    y = _bdot(merged.astype(BF16), wout_ref[...])
    x1 = x + _scale_rows(y, gta_ref[...], tm, ch)
    x1_o[...] = x1
    h2 = _modulate(_rms(x1, nffn_ref[...]), scm_ref[...], shm_ref[...], tm, ch)
    h2_o[...] = h2.astype(BF16)
    lg = _dot_split(h2, rw_ref[...]) + rb_ref[...]
    lanef = lax.broadcasted_iota(jnp.int32, (tm, LANES), 1).astype(F32)
    e_acc = jnp.zeros((tm, LANES), F32)
    v_acc = jnp.full((tm, LANES), NEG_INF, F32)
    chosen = jnp.zeros((tm, LANES), F32)
    picks = []
    for k in range(TOP_K):
        mx = jnp.max(lg, axis=1, keepdims=True)
        idx = jnp.min(jnp.where(lg == mx, lanef, float(LANES)), axis=1, keepdims=True)
        hit = lanef == idx
        picks.append(hit)
        chosen = jnp.where(hit, 1.0, chosen)
        e_acc = jnp.where(lanef == float(k), idx, e_acc)
        v_acc = jnp.where(lanef == float(k), mx, v_acc)
        lg = jnp.where(hit, BELOW_NEG, lg)
    ex = jnp.where(lanef < float(TOP_K), jnp.exp(v_acc - jnp.max(v_acc, axis=1, keepdims=True)), 0.0)
    g_o[...] = ex / jnp.sum(ex, axis=1, keepdims=True)
    @pl.when(pl.program_id(0) == 0)
    def _():
        cnt_sc[...] = jnp.zeros(cnt_sc.shape, F32)

    earlier = (lax.broadcasted_iota(jnp.int32, (tm, tm), 1) < lax.broadcasted_iota(jnp.int32, (tm, tm), 0))
    before = _bdot(jnp.where(earlier, 1.0, 0.0).astype(BF16), chosen.astype(BF16)) + cnt_sc[0:1, :]
    for k in range(TOP_K):
        rank = jnp.sum(jnp.where(picks[k], before, 0.0), axis=1, keepdims=True)
        e_acc = jnp.where(lanef == float(TOP_K + k), rank, e_acc)
    e_o[...] = e_acc.astype(jnp.int32)
    cnt_sc[...] = cnt_sc[...] + jnp.sum(chosen, axis=0, keepdims=True)
    cnt_o[...] = cnt_sc[...]


def _kb_call(x, oa, ob, oc, mods, nmix, nffn, wg, wba, wbb, wbc, wout, rw, rb):
    n = x.shape[0]
    tm = TOKEN_BLOCK
    ch = tm // MOD_ROWS

    def row(i):
        return (i, 0)

    def full(a):
        return pl.BlockSpec(a.shape, lambda i: (0, 0))

    in_specs = ([pl.BlockSpec((tm, D_MODEL), row), pl.BlockSpec((tm, A_WIDTH), row),
                 pl.BlockSpec((tm, B_WIDTH), row), pl.BlockSpec((tm, 512), row)]
                + [pl.BlockSpec((MOD_ROWS, D_MODEL), row) for _ in range(5)]
                + [full(a) for a in (nmix, nffn, wg, wba, wbb, wbc, wout, rw, rb)])
    out_shape = [jax.ShapeDtypeStruct((n, D_MODEL), F32), jax.ShapeDtypeStruct((n, D_MODEL), BF16),
                 jax.ShapeDtypeStruct((n, LANES), jnp.int32), jax.ShapeDtypeStruct((n, LANES), F32),
                 jax.ShapeDtypeStruct((8, LANES), F32)]
    out_specs = [pl.BlockSpec((tm, D_MODEL), row), pl.BlockSpec((tm, D_MODEL), row),
                 pl.BlockSpec((tm, LANES), row), pl.BlockSpec((tm, LANES), row),
                 pl.BlockSpec((8, LANES), lambda i: (0, 0))]
    return pl.pallas_call(
        functools.partial(_kb_kernel, tm=tm, ch=ch),
        grid=(n // tm,),
        in_specs=in_specs,
        out_specs=out_specs,
        out_shape=out_shape,
        scratch_shapes=[pltpu.VMEM((8, LANES), F32)],
        compiler_params=_cparams(("arbitrary",)),
        name="post_attention",
    )(x, oa, ob, oc, *mods, nmix, nffn, wg, wba, wbb, wbc, wout, rw, rb)


def _ke_kernel(be_ref, nu_ref, x_ref, wg_ref, wl_ref, wd_ref, bg_ref, bl_ref, bd_ref, y_ref):
    i = pl.program_id(0)

    @pl.when(i < nu_ref[0])
    def _():
        x = x_ref[...]
        g = jnp.minimum(_bdot(x, wg_ref[0]) + bg_ref[0], SWIGLU_LIMIT)
        l = jnp.clip(_bdot(x, wl_ref[0]) + bl_ref[0], -SWIGLU_LIMIT, SWIGLU_LIMIT)
        act = g * _sigmoid(SWIGLU_ALPHA * g) * (l + 1.0)
        y_ref[...] = _bdot(act.astype(BF16), wd_ref[0]) + bd_ref[0]

    @pl.when(i >= nu_ref[0])
    def _():
        y_ref[...] = jnp.zeros(y_ref.shape, F32)


def _ke_call(blk_e, n_used, xg, wg, wl, wd, bg, bl, bd):
    n_rows = xg.shape[0]
    eb = EXPERT_ROWS
    n_blocks = n_rows // eb
    grid_spec = pltpu.PrefetchScalarGridSpec(
        num_scalar_prefetch=2,
        grid=(n_blocks,),
        in_specs=[pl.BlockSpec((eb, D_MODEL), lambda i, be, nu: (i, 0)),
                  pl.BlockSpec((1, D_MODEL, D_FF), lambda i, be, nu: (be[i], 0, 0)),
                  pl.BlockSpec((1, D_MODEL, D_FF), lambda i, be, nu: (be[i], 0, 0)),
                  pl.BlockSpec((1, D_FF, D_MODEL), lambda i, be, nu: (be[i], 0, 0)),
                  pl.BlockSpec((1, 1, D_FF), lambda i, be, nu: (be[i], 0, 0)),
                  pl.BlockSpec((1, 1, D_FF), lambda i, be, nu: (be[i], 0, 0)),
                  pl.BlockSpec((1, 1, D_MODEL), lambda i, be, nu: (be[i], 0, 0))],
        out_specs=pl.BlockSpec((eb, D_MODEL), lambda i, be, nu: (i, 0)),
    )
    return pl.pallas_call(
        _ke_kernel,
        grid_spec=grid_spec,
        out_shape=jax.ShapeDtypeStruct((n_rows, D_MODEL), F32),
        compiler_params=_cparams(("arbitrary",)),
        name="moe_experts",
    )(blk_e, n_used, xg, wg, wl, wd, bg, bl, bd)


def _kc_kernel(x1_ref, yg_ref, g_ref, gtm_ref, fn_ref, x2_o, y_o, *, tm, ch):
    gate = g_ref[...]
    ffn = gate[:, 0:1] * yg_ref[0]
    for k in range(1, TOP_K):
        ffn = ffn + gate[:, k:k + 1] * yg_ref[k]
    x2 = x1_ref[...] + _scale_rows(ffn, gtm_ref[...], tm, ch)
    x2_o[...] = x2
    y_o[...] = _rms(x2, fn_ref[...])


def _kc_call(x1, yg, gate, gtm, fnorm):
    n = x1.shape[0]
    tm = TOKEN_BLOCK
    ch = tm // MOD_ROWS
    return pl.pallas_call(
        functools.partial(_kc_kernel, tm=tm, ch=ch),
        grid=(n // tm,),
        in_specs=[pl.BlockSpec((tm, D_MODEL), lambda i: (i, 0)),
                  pl.BlockSpec((TOP_K, tm, D_MODEL), lambda i: (0, i, 0)),
                  pl.BlockSpec((tm, LANES), lambda i: (i, 0)),
                  pl.BlockSpec((MOD_ROWS, D_MODEL), lambda i: (i, 0)),
                  pl.BlockSpec((1, D_MODEL), lambda i: (0, 0))],
        out_specs=[pl.BlockSpec((tm, D_MODEL), lambda i: (i, 0)), pl.BlockSpec((tm, D_MODEL), lambda i: (i, 0))],
        out_shape=[jax.ShapeDtypeStruct((n, D_MODEL), F32), jax.ShapeDtypeStruct((n, D_MODEL), F32)],
        compiler_params=_cparams(("arbitrary",)),
        name="moe_combine",
    )(x1, yg, gate, gtm, fnorm)


def _rope_tables(pos):
    lane = np.arange(LANES)
    inv32 = ROPE_THETA ** (-jnp.arange(32, dtype=F32) / 32)
    inv16 = ROPE_THETA ** (-jnp.arange(16, dtype=F32) / 16)
    ang64 = pos[:, None] * inv32[None, :][:, lane & 31]
    ang32 = pos[:, None] * inv16[None, :][:, lane & 15]
    sign64 = jnp.asarray(np.where((lane & 63) < 32, -1.0, 1.0), F32)[None, :]
    sign32 = jnp.asarray(np.where((lane & 31) < 16, -1.0, 1.0), F32)[None, :]
    in_m = jnp.asarray(lane < C_ROPE)[None, :]
    in_q = jnp.asarray((lane >= C_NOPE) & (lane < C_NOPE + C_ROPE))[None, :]
    cos64, sin64 = jnp.cos(ang64), jnp.sin(ang64) * sign64
    cos32, sin32 = jnp.cos(ang32), jnp.sin(ang32) * sign32
    return (cos64, sin64,
            jnp.where(in_m, cos32, 1.0), jnp.where(in_m, sin32, 0.0),
            jnp.where(in_q, cos32, 1.0), jnp.where(in_q, sin32, 0.0))


def _layer_weights(l, w_in, mla_w_uq, mla_w_ukv, w_br_c, router_w, router_b, exp_w_gu, exp_b_gu):
    wi = w_in[l]
    z = lambda n: jnp.zeros((D_MODEL, n), F32)
    w1 = jnp.concatenate([
        wi[:, _OFF_AQ:_OFF_BIK],
        wi[:, _OFF_BIK:_OFF_BIW], wi[:, _OFF_BIK:_OFF_BIW],
        wi[:, _OFF_CKR:_OFF_GATES], wi[:, _OFF_BIW:_OFF_CQ], z(LANES - C_ROPE - IDX_HEADS),
        wi[:, _OFF_CQ:_OFF_CKV], wi[:, _OFF_CKV:_OFF_CKR]], axis=1).astype(BF16)
    wg = wi[:, _OFF_GATES:].astype(BF16)
    wuq = mla_w_uq[l].reshape(Q_LORA, C_HEADS, C_NOPE + C_ROPE)
    wuq = jnp.pad(wuq, ((0, 0), (0, 0), (0, LANES - C_NOPE - C_ROPE))).reshape(Q_LORA, C_HEADS * LANES).astype(BF16)
    wukv = mla_w_ukv[l].reshape(KV_LORA, C_HEADS, C_NOPE + C_V)
    wk = jnp.pad(wukv[:, :, :C_NOPE], ((0, 0), (0, 0), (0, LANES - C_NOPE))).reshape(KV_LORA, C_HEADS * LANES).astype(BF16)
    wv = jnp.pad(wukv[:, :, C_NOPE:], ((0, 0), (0, 0), (0, LANES - C_V))).reshape(KV_LORA, C_HEADS * LANES).astype(BF16)
    e = np.zeros((LANES, C_HEADS * LANES), np.float32)
    for h in range(C_HEADS):
        e[np.arange(C_ROPE), h * LANES + C_NOPE + np.arange(C_ROPE)] = 1.0
    we = jnp.asarray(e, BF16)
    wbc = jnp.pad(w_br_c[l].reshape(C_HEADS, C_V, D_MODEL), ((0, 0), (0, LANES - C_V), (0, 0)))
    wbc = wbc.reshape(C_HEADS * LANES, D_MODEL).astype(BF16)
    rw = jnp.pad(router_w[l], ((0, 0), (0, LANES - N_EXPERTS)))
    rb = jnp.pad(router_b[l], (0, LANES - N_EXPERTS), constant_values=NEG_INF).reshape(1, LANES)
    wgl = exp_w_gu[l].reshape(N_EXPERTS, D_MODEL, D_FF, 2)
    bgl = exp_b_gu[l].reshape(N_EXPERTS, 1, D_FF, 2)
    return dict(w1=w1, wg=wg, wuq=wuq, wk=wk, wv=wv, we=we, wbc=wbc, rw=rw, rb=rb,
                e_wg=wgl[..., 0].astype(BF16), e_wl=wgl[..., 1].astype(BF16),
                e_bg=bgl[..., 0], e_bl=bgl[..., 1])


def _route(e_pad, cnt, n_rows_pad):
    eb = EXPERT_ROWS
    flat_e = e_pad[:, :TOP_K].reshape(-1)
    rank = e_pad[:, TOP_K:2 * TOP_K].reshape(-1)
    nk = flat_e.shape[0]
    counts = cnt[0, :N_EXPERTS].astype(jnp.int32)
    padded = (counts + eb - 1) // eb * eb
    pad_end = jnp.cumsum(padded)
    pad_start = pad_end - padded
    dest = pad_start[flat_e] + rank
    row_tok = jnp.zeros((n_rows_pad,), jnp.int32).at[dest].set(jnp.arange(nk, dtype=jnp.int32) // TOP_K)
    n_blocks = n_rows_pad // eb
    blk_start = jnp.arange(n_blocks, dtype=jnp.int32) * eb
    blk_e = jnp.minimum(jnp.searchsorted(pad_end, blk_start, side='right'), N_EXPERTS - 1).astype(jnp.int32)
    n_used = (pad_end[-1] // eb).astype(jnp.int32).reshape(1)
    return dest, row_tok, blk_e, n_used


def kernel(x_prompt, x_sample, cache_a_k, cache_a_v, cache_b_k, cache_b_v, cache_b_idx_k, cache_c_latent, cache_c_k_rope, c_prompt, c_sample, w_ada, b_ada, norm_mix, norm_ffn, w_in, diff_lq1, diff_lk1, diff_lq2, diff_lk2, diff_subln, mla_q_norm, mla_w_uq, mla_kv_norm, mla_w_ukv, w_br_a, w_br_b, w_br_c, w_out, router_w, router_b, exp_w_gu, exp_b_gu, exp_w_down, exp_b_down, final_norm):
    depth = w_ada.shape[0]
    bp, tp, _ = x_prompt.shape
    bs, ts, _ = x_sample.shape
    past = cache_c_latent.shape[2]
    n_p, n_s = bp * tp, bs * ts
    n = n_p + n_s
    tm = TOKEN_BLOCK
    ch = tm // MOD_ROWS
    assert ts == CHUNK and tp % tm == 0 and n_s % tm == 0 and past % CHUNK == 0

    x = jnp.concatenate([x_prompt.reshape(n_p, D_MODEL), x_sample.reshape(n_s, D_MODEL)], axis=0)

    n_seq = bp + bs
    c_all = jnp.concatenate([c_prompt, c_sample], axis=0)
    c_pad = jnp.pad(c_all, ((0, (-n_seq) % 8), (0, 0)))
    mod = _ada_call(c_pad, w_ada, b_ada)
    def per_chunk(m, reps):
        return jnp.broadcast_to(m[:, :, None, :], m.shape[:2] + (reps, m.shape[2])).reshape(depth, -1, m.shape[2])

    mod_rows = jnp.concatenate([per_chunk(mod[:, :bp], tp // ch), per_chunk(mod[:, bp:n_seq], ts // ch)],
                               axis=1)

    pos = jnp.concatenate([jnp.arange(tp, dtype=F32),
                           jnp.tile(past + jnp.arange(ts, dtype=F32), tm // ts)])
    tabs = _rope_tables(pos)

    tq_p = min(256, tp)
    tk_p = min(512, tp)
    nq_p = tp // tq_p
    tq_c = min(512, tp)
    l_s = past + ts
    tk_s = 384
    lp_s = -(-l_s // tk_s) * tk_s
    n_rows_pad = -(-(n * TOP_K + N_EXPERTS * (EXPERT_ROWS - 1)) // EXPERT_ROWS) * EXPERT_ROWS

    def with_cache(cache_l, new, width):
        parts = [cache_l.reshape(bs, past, width).astype(BF16), new.reshape(bs, ts, width)]
        if lp_s > l_s:
            parts.append(jnp.zeros((bs, lp_s - l_s, width), BF16))
        return jnp.concatenate(parts, axis=1).reshape(bs * lp_s, width)

    caches = [[] for _ in range(7)]
    y = None
    for l in range(depth):
        lam_init = 0.8 - 0.6 * math.exp(-0.3 * l)
        wl = _layer_weights(l, w_in, mla_w_uq, mla_w_ukv, w_br_c, router_w, router_b, exp_w_gu, exp_b_gu)
        m6 = [mod_rows[l, :, j * D_MODEL:(j + 1) * D_MODEL] for j in range(6)]
        sh_a, sc_a, gt_a, sh_m, sc_m, gt_m = m6
        nmix = norm_mix[l].reshape(1, D_MODEL)
        nffn = norm_ffn[l].reshape(1, D_MODEL)

        (ak, av, bk, bv, bik, clat, ckr, misc,
         aq_b, ak_b, av_b, bq_b, bk_b, bv_b, biq_b, bik2_b, cq_b, ck_b, cv_b) = _ka_call(
            x, sh_a, sc_a, tabs, nmix, wl['w1'], mla_q_norm[l].reshape(1, Q_LORA), wl['wuq'],
            mla_kv_norm[l].reshape(1, KV_LORA), wl['wk'], wl['we'], wl['wv'],
            n_prompt_blocks=n_p // tm, tab_blocks=tp // tm)
        for i, a in enumerate((ak, av, bk, bv, bik, clat, ckr)):
            caches[i].append(a)

        lqk = jnp.pad(jnp.stack([diff_lq1[l], diff_lk1[l], diff_lq2[l], diff_lk2[l]]),
                      ((0, 4), (0, LANES - A_HD)))
        sub = diff_subln[l].reshape(1, 2 * A_HD)
        oa_p = _diff_call(lqk, sub, aq_b, ak_b, av_b, nb=bp, nq=nq_p, tq=tq_p, tk=tk_p, lp=tp,
                          q_blk0=0, k_blk0=0, q_pos0=0, lam_init=lam_init)
        ak_s = with_cache(cache_a_k[l], ak_b[n_p:], A_WIDTH)
        av_s = with_cache(cache_a_v[l], av_b[n_p:], A_WIDTH)
        oa_s = _diff_call(lqk, sub, aq_b, ak_s, av_s, nb=bs, nq=1, tq=ts, tk=tk_s, lp=lp_s,
                          q_blk0=n_p // ts, k_blk0=0, q_pos0=past, lam_init=lam_init)
        ob_p = _dsa_call(bq_b, biq_b, misc, bk_b, bv_b, bik2_b, nb=bp, nq=nq_p, tq=tq_p, tk=tk_p, lp=tp,
                         q_blk0=0, k_blk0=0, q_pos0=0, n_keys=tp, n_bisect=14)
        bk_s = with_cache(cache_b_k[l], bk_b[n_p:], B_WIDTH)
        bv_s = with_cache(cache_b_v[l], bv_b[n_p:], B_WIDTH)
        cik = cache_b_idx_k[l].reshape(bs, past, IDX_DIM)
        bik_s = with_cache(jnp.concatenate([cik, cik], axis=-1), bik2_b[n_p:], LANES)
        ob_s = _dsa_call(bq_b, biq_b, misc, bk_s, bv_s, bik_s, nb=bs, nq=1, tq=ts, tk=tk_s, lp=lp_s,
                         q_blk0=n_p // ts, k_blk0=0, q_pos0=past, n_keys=l_s, n_bisect=12)
        oc_p = _mla_call(cq_b, ck_b, cv_b, nb=bp, nq=tp // tq_c, tq=tq_c, tk=tk_p, lp=tp,
                         q_blk0=0, k_blk0=0, q_pos0=0)
        lat_c = cache_c_latent[l].reshape(bs * past, KV_LORA)
        kr_c = jnp.pad(cache_c_k_rope[l].reshape(bs * past, C_ROPE), ((0, 0), (0, LANES - C_ROPE)))
        ck_c, cv_c = _mla_kv_call(lat_c, kr_c, wl['wk'], wl['we'], wl['wv'])
        ck_s = with_cache(ck_c, ck_b[n_p:], 512)
        cv_s = with_cache(cv_c, cv_b[n_p:], 512)
        oc_s = _mla_call(cq_b, ck_s, cv_s, nb=bs, nq=1, tq=ts, tk=tk_s, lp=lp_s,
                         q_blk0=n_p // ts, k_blk0=0, q_pos0=past)

        oa = jnp.concatenate([oa_p, oa_s], axis=0)
        ob = jnp.concatenate([ob_p, ob_s], axis=0)
        oc = jnp.concatenate([oc_p, oc_s], axis=0)

        x1, h2, e_pad, g_pad, cnt = _kb_call(
            x, oa, ob, oc, (sh_a, sc_a, gt_a, sh_m, sc_m), nmix, nffn, wl['wg'],
            w_br_a[l].astype(BF16), w_br_b[l].astype(BF16), wl['wbc'], w_out[l].astype(BF16),
            wl['rw'], wl['rb'])

        dest, row_tok, blk_e, n_used = _route(e_pad, cnt, n_rows_pad)
        xg = jnp.take(h2, row_tok, axis=0)
        yr = _ke_call(blk_e, n_used, xg, wl['e_wg'], wl['e_wl'], exp_w_down[l].astype(BF16),
                      wl['e_bg'], wl['e_bl'], exp_b_down[l].reshape(N_EXPERTS, 1, D_MODEL))
        yg = jnp.take(yr, dest.reshape(n, TOP_K).T.reshape(-1), axis=0).reshape(TOP_K, n, D_MODEL)
        x, y = _kc_call(x1, yg, g_pad, gt_m, final_norm.reshape(1, D_MODEL))

    def split(a, tail):
        a = jnp.stack(a, axis=0)
        return (a[:, :n_p].reshape((depth, bp, tp) + tail), a[:, n_p:].reshape((depth, bs, ts) + tail))

    tails = ((A_HEADS, 2 * A_HD), (A_HEADS, 2 * A_HD), (B_HEADS, B_HD), (B_HEADS, B_HD),
             (IDX_DIM,), (KV_LORA,), (C_ROPE,))
    ps = [split(c, t) for c, t in zip(caches, tails)]
    y_prompt = y[:n_p].reshape(bp, tp, D_MODEL)
    y_sample = y[n_p:].reshape(bs, ts, D_MODEL)
    return (y_prompt, y_sample) + tuple(p[0] for p in ps) + tuple(p[1] for p in ps)
```

```python
import functools
import math

import numpy as np
import jax
import jax.numpy as jnp
from jax import lax
from jax.experimental import pallas as pl
from jax.experimental.pallas import tpu as pltpu

F32 = jnp.float32
BF16 = jnp.bfloat16

D_MODEL = 1024
CHUNK = 64
ROPE_THETA = 10000.0
NORM_EPS = 1e-6
NEG_INF = -1e30
HALF_NEG = -5e29
BELOW_NEG = -3e38
BIG_POS = 3e38
LOG2E = 1.4426950408889634

A_HEADS, A_HD = 4, 64
B_HEADS, B_HD = 4, 64
IDX_HEADS, IDX_DIM = 4, 64
DSA_TOPK = 256
C_HEADS, C_NOPE, C_ROPE, C_V = 4, 64, 32, 64
Q_LORA, KV_LORA = 256, 128
N_EXPERTS, TOP_K = 32, 4
D_FF = D_MODEL
SWIGLU_LIMIT = 7.0
SWIGLU_ALPHA = 1.702

A_WIDTH = A_HEADS * 2 * A_HD
B_WIDTH = B_HEADS * B_HD
LANES = 128

_OFF_AQ, _OFF_AK, _OFF_AV = 0, 512, 1024
_OFF_BQ, _OFF_BK, _OFF_BV = 1536, 1792, 2048
_OFF_BIQ, _OFF_BIK, _OFF_BIW = 2304, 2560, 2624
_OFF_CQ, _OFF_CKV, _OFF_CKR = 2628, 2884, 3012
_OFF_GATES = 3044
_P_AQ, _P_AK, _P_AV = 0, 512, 1024
_P_BQ, _P_BK, _P_BV = 1536, 1792, 2048
_P_BIQ, _P_BIK2, _P_MISC, _P_CQ, _P_CKV, _P_END = 2304, 2560, 2688, 2816, 3072, 3200
_MISC_BIW = 32

TOKEN_BLOCK = 256
MOD_ROWS = 8
EXPERT_ROWS = 256
VMEM_LIMIT = 56 * 1024 * 1024


def _cparams(sem, vmem=VMEM_LIMIT):
    return pltpu.CompilerParams(dimension_semantics=sem, vmem_limit_bytes=vmem)


def _rms(xf, g):
    return xf * lax.rsqrt(jnp.mean(xf * xf, axis=-1, keepdims=True) + NORM_EPS) * g


def _sigmoid(x):
    return 1.0 / (1.0 + jnp.exp(-x))


def _bdot(a, b):
    return jnp.dot(a, b, preferred_element_type=F32)


def _dot_nt(a, b):
    return lax.dot_general(a, b, (((1,), (1,)), ((), ())), preferred_element_type=F32)


def _dot_split(a, b):
    a_hi = a.astype(BF16)
    b_hi = b.astype(BF16)
    a_lo = (a - a_hi.astype(F32)).astype(BF16)
    b_lo = (b - b_hi.astype(F32)).astype(BF16)
    return _bdot(a_hi, b_hi) + (_bdot(a_hi, b_lo) + _bdot(a_lo, b_hi))


def _modulate(xn, sc, sh, rows, ch):
    n = rows // ch
    y = xn.reshape(n, ch, D_MODEL) * (1.0 + sc)[:, None, :] + sh[:, None, :]
    return y.reshape(rows, D_MODEL)


def _scale_rows(y, g, rows, ch):
    n = rows // ch
    return (y.reshape(n, ch, D_MODEL) * g[:, None, :]).reshape(rows, D_MODEL)


def _ada_kernel(c_ref, w_ref, b_ref, o_ref):
    c = c_ref[...]
    s = c * _sigmoid(c)
    o_ref[0] = _dot_split(s, w_ref[0]) + b_ref[0]


def _ada_call(c_pad, w_ada, b_ada):
    depth = w_ada.shape[0]
    mp = c_pad.shape[0]
    return pl.pallas_call(
        _ada_kernel,
        grid=(depth, 6),
        in_specs=[
            pl.BlockSpec((mp, D_MODEL), lambda l, j: (0, 0)),
            pl.BlockSpec((1, D_MODEL, D_MODEL), lambda l, j: (l, 0, j)),
            pl.BlockSpec((1, 1, D_MODEL), lambda l, j: (l, 0, j)),
        ],
        out_specs=pl.BlockSpec((1, mp, D_MODEL), lambda l, j: (l, 0, j)),
        out_shape=jax.ShapeDtypeStruct((depth, mp, 6 * D_MODEL), F32),
        compiler_params=_cparams(("arbitrary", "arbitrary")),
        name="ada_mod",
    )(c_pad, w_ada, b_ada.reshape(depth, 1, 6 * D_MODEL))


def _rope_partner(x, lane, half):
    first = (lane & (2 * half - 1)) < half
    return jnp.where(first, pltpu.roll(x, LANES - half, 1), pltpu.roll(x, half, 1))


def _ka_kernel(x_ref, sh_ref, sc_ref, cos_ref, sin_ref, cosm_ref, sinm_ref, cosq_ref, sinq_ref,
               nmix_ref, w1_ref, qn_ref, wuq_ref, kvn_ref, wk_ref, we_ref, wv_ref,
               ak_o, av_o, bk_o, bv_o, bik_o, clat_o, ckr_o, misc_o,
               aq_b, ak_b, av_b, bq_b, bk_b, bv_b, biq_b, bik2_b, cq_b, ck_b, cv_b, *, tm, ch):
    x = x_ref[...]
    h = _modulate(_rms(x, nmix_ref[...]), sc_ref[...], sh_ref[...], tm, ch)
    hb = h.astype(BF16)
    lane = lax.broadcasted_iota(jnp.int32, (tm, LANES), 1)
    cos = cos_ref[...]
    sin = sin_ref[...]

    def proj(c0, c1):
        return _bdot(hb, w1_ref[:, c0:c1])

    def rope64(xb):
        return xb * cos + _rope_partner(xb, lane, 32) * sin

    a_scale = (A_HD ** -0.5) * LOG2E
    b_scale = (B_HD ** -0.5) * LOG2E
    i_scale = IDX_DIM ** -0.5
    c_scale = ((C_NOPE + C_ROPE) ** -0.5) * LOG2E

    p = proj(_P_AQ, _P_AK)
    for c in range(A_WIDTH // LANES):
        sl = slice(c * LANES, (c + 1) * LANES)
        aq_b[:, sl] = (rope64(p[:, sl]) * a_scale).astype(BF16)
    p = proj(_P_AK, _P_AV)
    for c in range(A_WIDTH // LANES):
        sl = slice(c * LANES, (c + 1) * LANES)
        r = rope64(p[:, sl])
        ak_o[:, sl] = r
        ak_b[:, sl] = r.astype(BF16)
    p = proj(_P_AV, _P_BQ)
    av_o[...] = p
    av_b[...] = p.astype(BF16)
    p = proj(_P_BQ, _P_BK)
    for c in range(B_WIDTH // LANES):
        sl = slice(c * LANES, (c + 1) * LANES)
        bq_b[:, sl] = (rope64(p[:, sl]) * b_scale).astype(BF16)
    p = proj(_P_BK, _P_BV)
    for c in range(B_WIDTH // LANES):
        sl = slice(c * LANES, (c + 1) * LANES)
        r = rope64(p[:, sl])
        bk_o[:, sl] = r
        bk_b[:, sl] = r.astype(BF16)
    p = proj(_P_BV, _P_BIQ)
    bv_o[...] = p
    bv_b[...] = p.astype(BF16)
    p = proj(_P_BIQ, _P_BIK2)
    for c in range(B_WIDTH // LANES):
        sl = slice(c * LANES, (c + 1) * LANES)
        biq_b[:, sl] = (rope64(p[:, sl]) * i_scale).astype(BF16)
    r = rope64(proj(_P_BIK2, _P_MISC))
    bik_o[...] = r[:, :IDX_DIM]
    bik2_b[...] = r.astype(BF16)
    pm = proj(_P_MISC, _P_CQ)
    misc = pm * cosm_ref[...] + _rope_partner(pm, lane, 16) * sinm_ref[...]
    misc_o[...] = misc
    ckr_o[...] = misc[:, :C_ROPE]
    qlat = _rms(proj(_P_CQ, _P_CKV), qn_ref[...]).astype(BF16)
    cqf = _bdot(qlat, wuq_ref[...])
    cosq = cosq_ref[...]
    sinq = sinq_ref[...]
    for c in range(C_HEADS):
        sl = slice(c * LANES, (c + 1) * LANES)
        xb = cqf[:, sl]
        cq_b[:, sl] = ((xb * cosq + _rope_partner(xb, lane, 16) * sinq) * c_scale).astype(BF16)
    clat = _rms(proj(_P_CKV, _P_END), kvn_ref[...])
    clat_o[...] = clat
    clb = clat.astype(BF16)
    ck_b[...] = (_bdot(clb, wk_ref[...]) + _bdot(misc.astype(BF16), we_ref[...])).astype(BF16)
    cv_b[...] = _bdot(clb, wv_ref[...]).astype(BF16)


def _ka_call(x, sh, sc, tabs, nmix, w1, qn, wuq, kvn, wk, we, wv, *, n_prompt_blocks, tab_blocks):
    n = x.shape[0]
    tm = TOKEN_BLOCK
    ch = tm // MOD_ROWS
    nblk = n // tm

    def row(i):
        return (i, 0)

    def tab(i):
        return (jnp.where(i < n_prompt_blocks, i % tab_blocks, tab_blocks), 0)

    def const(i):
        return (0, 0)

    def full(a):
        return pl.BlockSpec(a.shape, const)

    widths_f32 = (A_WIDTH, A_WIDTH, B_WIDTH, B_WIDTH, IDX_DIM, KV_LORA, C_ROPE, LANES)
    widths_b16 = (A_WIDTH, A_WIDTH, A_WIDTH, B_WIDTH, B_WIDTH, B_WIDTH, B_WIDTH, LANES, 512, 512, 512)
    out_shape = ([jax.ShapeDtypeStruct((n, w), F32) for w in widths_f32]
                 + [jax.ShapeDtypeStruct((n, w), BF16) for w in widths_b16])
    out_specs = [pl.BlockSpec((tm, w), row) for w in widths_f32 + widths_b16]
    in_specs = ([pl.BlockSpec((tm, D_MODEL), row),
                 pl.BlockSpec((MOD_ROWS, D_MODEL), row),
                 pl.BlockSpec((MOD_ROWS, D_MODEL), row)]
                + [pl.BlockSpec((tm, LANES), tab) for _ in range(6)]
                + [full(a) for a in (nmix, w1, qn, wuq, kvn, wk, we, wv)])
    return pl.pallas_call(
        functools.partial(_ka_kernel, tm=tm, ch=ch),
        grid=(nblk,),
        in_specs=in_specs,
        out_specs=out_specs,
        out_shape=out_shape,
        compiler_params=_cparams(("arbitrary",)),
        name="pre_attention",
    )(x, sh, sc, *tabs, nmix, w1, qn, wuq, kvn, wk, we, wv)


def _mla_kv_kernel(lat_ref, kr_ref, wk_ref, we_ref, wv_ref, ck_o, cv_o):
    lb = lat_ref[...].astype(BF16)
    ck_o[...] = (_bdot(lb, wk_ref[...]) + _bdot(kr_ref[...].astype(BF16), we_ref[...])).astype(BF16)
    cv_o[...] = _bdot(lb, wv_ref[...]).astype(BF16)


def _mla_kv_call(lat, krp, wk, we, wv):
    n = lat.shape[0]
    tm = 512
    return pl.pallas_call(
        _mla_kv_kernel,
        grid=(n // tm,),
        in_specs=[pl.BlockSpec((tm, KV_LORA), lambda i: (i, 0)),
                  pl.BlockSpec((tm, LANES), lambda i: (i, 0)),
                  pl.BlockSpec(wk.shape, lambda i: (0, 0)),
                  pl.BlockSpec(we.shape, lambda i: (0, 0)),
                  pl.BlockSpec(wv.shape, lambda i: (0, 0))],
        out_specs=[pl.BlockSpec((tm, 512), lambda i: (i, 0)), pl.BlockSpec((tm, 512), lambda i: (i, 0))],
        out_shape=[jax.ShapeDtypeStruct((n, 512), BF16), jax.ShapeDtypeStruct((n, 512), BF16)],
        compiler_params=_cparams(("arbitrary",)),
        name="mla_cache_kv",
    )(lat, krp, wk, we, wv)


def _block_range(r0, tq, tk, nkb):
    n_full = jnp.minimum((r0 + CHUNK) // tk, nkb)
    n_vis = jnp.minimum((r0 + tq + tk - 1) // tk, nkb)
    return n_full, n_vis


def _visible(r0, ks, rows, tk):
    rpos = r0 + lax.broadcasted_iota(jnp.int32, (rows, tk), 0)
    kpos = ks + lax.broadcasted_iota(jnp.int32, (rows, tk), 1)
    return kpos < (((rpos >> 6) + 1) << 6)


def _fold_max(s, m_sc, rows):
    mp = m_sc[rows, :]
    for j in range(s.shape[1] // LANES):
        mp = jnp.maximum(mp, s[:, j * LANES:(j + 1) * LANES])
    m_sc[rows, :] = mp


def _finish_max(m_sc, rows):
    mp = m_sc[rows, :]
    m_sc[rows, :] = jnp.broadcast_to(jnp.max(mp, axis=1, keepdims=True), mp.shape)


def _accumulate(s, vblk, m_sc, l_sc, acc_sc, rows):
    m = m_sc[rows, :]
    lp = l_sc[rows, :]
    ps = []
    for j in range(s.shape[1] // LANES):
        pj = jnp.exp2(s[:, j * LANES:(j + 1) * LANES] - m)
        lp = lp + pj
        ps.append(pj.astype(BF16))
    l_sc[rows, :] = lp
    acc_sc[rows, :] = acc_sc[rows, :] + _bdot(jnp.concatenate(ps, axis=1), vblk)


WIDE = 2


def _two_pass(n_full, n_vis, score, values, m_sc, l_sc, acc_sc, sbuf, groups):
    tk = sbuf.shape[2]
    m_sc[...] = jnp.full(m_sc.shape, NEG_INF, F32)
    l_sc[...] = jnp.zeros(l_sc.shape, F32)
    acc_sc[...] = jnp.zeros(acc_sc.shape, F32)

    for g, rows in enumerate(groups):
        def keep(kb, nblk, masked):
            s = score(kb, nblk, masked, g)
            for w in range(nblk):
                sbuf[kb + w] = s[:, w * tk:(w + 1) * tk]
            _fold_max(s, m_sc, rows)

        def wide_body(j, c):
            keep(j * WIDE, WIDE, False)
            return c

        def full_body(kb, c):
            keep(kb, 1, False)
            return c

        def masked_body(kb, c):
            keep(kb, 1, True)
            return c

        n_wide = n_full // WIDE
        lax.fori_loop(0, n_wide, wide_body, 0)
        lax.fori_loop(n_wide * WIDE, n_full, full_body, 0)
        lax.fori_loop(n_full, n_vis, masked_body, 0)
        _finish_max(m_sc, rows)

        def acc_wide(j, c):
            kb = j * WIDE
            s = jnp.concatenate([sbuf[kb + w] for w in range(WIDE)], axis=1)
            _accumulate(s, values(kb, WIDE, g), m_sc, l_sc, acc_sc, rows)
            return c

        def acc_body(kb, c):
            _accumulate(sbuf[kb], values(kb, 1, g), m_sc, l_sc, acc_sc, rows)
            return c

        n_wide = n_vis // WIDE
        lax.fori_loop(0, n_wide, acc_wide, 0)
        lax.fori_loop(n_wide * WIDE, n_vis, acc_body, 0)
    return acc_sc[...] / jnp.sum(l_sc[...], axis=1, keepdims=True)


def _split_halves(q, lane=None):
    lane1 = lax.broadcasted_iota(jnp.int32, (1, LANES), 1)
    lo = jnp.where(lane1 < 64, 1.0, 0.0).astype(q.dtype)
    return q * lo, q * (1.0 - lo).astype(q.dtype)


def _diff_kernel(lqk_ref, sub_ref, q_ref, k_ref, v_ref, o_ref, m_sc, l_sc, acc_sc, sbuf,
                 *, tq, tk, nkb, q_pos0, lam_init):
    i = pl.program_id(2)
    r0 = q_pos0 + i * tq
    n_full, n_vis = _block_range(r0, tq, tk, nkb)
    lane = lax.broadcasted_iota(jnp.int32, (tq, LANES), 1)
    qq = jnp.concatenate(_split_halves(q_ref[...], lane), axis=0)

    def score(kb, nblk, masked, g):
        ks = pl.multiple_of(kb * tk, tk)
        s = _dot_nt(qq, k_ref[pl.ds(ks, nblk * tk), :])
        if masked:
            vis = _visible(r0, ks, tq, nblk * tk)
            s = jnp.where(jnp.concatenate([vis, vis], axis=0), s, NEG_INF)
        return s

    def values(kb, nblk, g):
        return v_ref[pl.ds(pl.multiple_of(kb * tk, tk), nblk * tk), :]

    o = _two_pass(n_full, n_vis, score, values, m_sc, l_sc, acc_sc, sbuf, (slice(0, 2 * tq),))
    o1 = o[:tq]
    o2 = o[tq:]
    lq = lqk_ref[...]
    lam = (jnp.exp(jnp.sum(lq[0:1] * lq[1:2], axis=1, keepdims=True))
           - jnp.exp(jnp.sum(lq[2:3] * lq[3:4], axis=1, keepdims=True)) + lam_init)
    o = o1 - lam * o2
    o = o * lax.rsqrt(jnp.mean(o * o, axis=1, keepdims=True) + NORM_EPS)
    o_ref[...] = (o * sub_ref[...] * (1.0 - lam_init)).astype(BF16)


def _diff_call(lqk, sub, q, k, v, *, nb, nq, tq, tk, lp, q_blk0, k_blk0, q_pos0, lam_init):
    nkb = lp // tk
    kern = functools.partial(_diff_kernel, tq=tq, tk=tk, nkb=nkb, q_pos0=q_pos0, lam_init=lam_init)
    return pl.pallas_call(
        kern,
        grid=(nb, A_HEADS, nq),
        in_specs=[pl.BlockSpec(lqk.shape, lambda b, h, i: (0, 0)),
                  pl.BlockSpec((1, LANES), lambda b, h, i: (0, 0)),
                  pl.BlockSpec((tq, LANES), lambda b, h, i: (q_blk0 + b * nq + i, h)),
                  pl.BlockSpec((lp, LANES), lambda b, h, i: (k_blk0 + b, h)),
                  pl.BlockSpec((lp, LANES), lambda b, h, i: (k_blk0 + b, h))],
        out_specs=pl.BlockSpec((tq, LANES), lambda b, h, i: (b * nq + i, h)),
        out_shape=jax.ShapeDtypeStruct((nb * nq * tq, A_WIDTH), BF16),
        scratch_shapes=([pltpu.VMEM((2 * tq, LANES), F32) for _ in range(3)]
                        + [pltpu.VMEM((nkb, 2 * tq, tk), F32)]),
        compiler_params=_cparams(("arbitrary", "arbitrary", "arbitrary")),
        name="diff_attention",
    )(lqk, sub, q, k, v)


def _mla_kernel(q_ref, k_ref, v_ref, o_ref, m_sc, l_sc, acc_sc, sbuf, *, tq, tk, nkb, q_pos0):
    i = pl.program_id(2)
    r0 = q_pos0 + i * tq
    n_full, n_vis = _block_range(r0, tq, tk, nkb)
    q = q_ref[...]

    def score(kb, nblk, masked, g):
        ks = pl.multiple_of(kb * tk, tk)
        s = _dot_nt(q, k_ref[pl.ds(ks, nblk * tk), :])
        if masked:
            s = jnp.where(_visible(r0, ks, tq, nblk * tk), s, NEG_INF)
        return s

    def values(kb, nblk, g):
        return v_ref[pl.ds(pl.multiple_of(kb * tk, tk), nblk * tk), :]

    o = _two_pass(n_full, n_vis, score, values, m_sc, l_sc, acc_sc, sbuf, (slice(0, tq),))
    o_ref[...] = o.astype(BF16)


def _mla_call(q, k, v, *, nb, nq, tq, tk, lp, q_blk0, k_blk0, q_pos0):
    nkb = lp // tk
    kern = functools.partial(_mla_kernel, tq=tq, tk=tk, nkb=nkb, q_pos0=q_pos0)
    return pl.pallas_call(
        kern,
        grid=(nb, C_HEADS, nq),
        in_specs=[pl.BlockSpec((tq, LANES), lambda b, h, i: (q_blk0 + b * nq + i, h)),
                  pl.BlockSpec((lp, LANES), lambda b, h, i: (k_blk0 + b, h)),
                  pl.BlockSpec((lp, LANES), lambda b, h, i: (k_blk0 + b, h))],
        out_specs=pl.BlockSpec((tq, LANES), lambda b, h, i: (b * nq + i, h)),
        out_shape=jax.ShapeDtypeStruct((nb * nq * tq, C_HEADS * LANES), BF16),
        scratch_shapes=([pltpu.VMEM((tq, LANES), F32) for _ in range(3)]
                        + [pltpu.VMEM((nkb, tq, tk), F32)]),
        compiler_params=_cparams(("arbitrary", "arbitrary", "arbitrary")),
        name="mla_attention",
    )(q, k, v)


def _dsa_kernel(q_ref, qi_ref, w_ref, k_ref, v_ref, ki_ref, o_ref,
                s_sc, m_sc, l_sc, acc_sc, t_sc, need_sc, carry_sc, wrep_sc, sbuf,
                *, tq, tk, nkb, q_pos0, n_keys, k_sel, rs, n_bisect):
    i = pl.program_id(1)
    r0 = q_pos0 + i * tq
    n_full, n_vis = _block_range(r0, tq, tk, nkb)
    lane = lax.broadcasted_iota(jnp.int32, (tq, LANES), 1)
    ksel_f = float(k_sel)

    qi = qi_ref[...]
    qa, qb = _split_halves(qi[:, :LANES], lane)
    qc, qd = _split_halves(qi[:, LANES:], lane)
    qi4 = jnp.concatenate([qa, qb, qc, qd], axis=0)
    wm = w_ref[...]
    for h in range(IDX_HEADS):
        wcol = wm[:, _MISC_BIW + h:_MISC_BIW + h + 1] * (IDX_HEADS ** -0.5)
        wrep_sc[h] = jnp.broadcast_to(wcol, (tq, LANES))

    def score_step(kb, masked):
        ks = pl.multiple_of(kb * tk, tk)
        rel = _dot_nt(qi4, ki_ref[pl.ds(ks, tk), :])
        cols = []
        for j in range(tk // LANES):
            cl = slice(j * LANES, (j + 1) * LANES)
            sc = wrep_sc[0] * jnp.maximum(rel[:tq, cl], 0.0)
            for h in range(1, IDX_HEADS):
                sc = sc + wrep_sc[h] * jnp.maximum(rel[h * tq:(h + 1) * tq, cl], 0.0)
            cols.append(sc)
        sc = jnp.concatenate(cols, axis=1)
        if masked:
            sc = jnp.where(_visible(r0, ks, tq, tk), sc, NEG_INF)
        s_sc[kb] = sc

    def score_full(kb, c):
        score_step(kb, False)
        return c

    def score_masked(kb, c):
        score_step(kb, True)
        return c

    lax.fori_loop(0, n_full, score_full, 0)
    lax.fori_loop(n_full, n_vis, score_masked, 0)

    nl = tk // LANES

    def search(sb, flag):
        rsl = pl.ds(pl.multiple_of(sb * rs, rs), rs)

        def fold(fn, init):
            def body(kb, part):
                s = s_sc[kb, rsl, :]
                for j in range(nl):
                    part = fn(part, s[:, j * LANES:(j + 1) * LANES])
                return part
            return lax.fori_loop(0, n_vis, body, init)

        def bc(x):
            return jnp.broadcast_to(x, (rs, LANES))

        def count(cmp, x):
            xb = bc(x)
            part = fold(lambda p, sj: p + jnp.where(cmp(sj, xb), 1.0, 0.0), jnp.zeros((rs, LANES), F32))
            return jnp.sum(part, axis=1, keepdims=True)

        def max_below(x, strict):
            xb = bc(x)
            if strict:
                part = fold(lambda p, sj: jnp.maximum(p, jnp.where(sj < xb, sj, BELOW_NEG)),
                            jnp.full((rs, LANES), BELOW_NEG, F32))
            else:
                part = fold(lambda p, sj: jnp.maximum(p, jnp.where(sj <= xb, sj, BELOW_NEG)),
                            jnp.full((rs, LANES), BELOW_NEG, F32))
            return jnp.max(part, axis=1, keepdims=True)

        ge = lambda a, b: a >= b
        gt = lambda a, b: a > b

        hi = jnp.max(fold(jnp.maximum, jnp.full((rs, LANES), BELOW_NEG, F32)), axis=1, keepdims=True)
        lo = jnp.min(fold(lambda p, sj: jnp.minimum(p, jnp.where(sj > HALF_NEG, sj, BIG_POS)),
                          jnp.full((rs, LANES), BIG_POS, F32)), axis=1, keepdims=True)
        rpos = r0 + sb * rs + lax.broadcasted_iota(jnp.int32, (rs, 1), 0)
        n_valid = jnp.minimum(((rpos >> 6) + 1) << 6, n_keys)
        small = n_valid < k_sel

        def bisect(_, c):
            lo_c, hi_c = c
            mid = 0.5 * (lo_c + hi_c)
            up = count(ge, mid) >= ksel_f
            return jnp.where(up, mid, lo_c), jnp.where(up, hi_c, mid)

        lo, hi = lax.fori_loop(0, n_bisect, bisect, (lo, hi))

        v0 = max_below(hi, False)
        g0 = count(ge, v0)
        done0 = jnp.where(jnp.logical_or(g0 >= ksel_f, small), 1.0, 0.0)

        def walk_cond(c):
            return jnp.min(c[2]) < 0.5

        def walk_body(c):
            t_c, g_c, done_c = c
            v = max_below(t_c, True)
            g = count(ge, v)
            keep = done_c > 0.5
            return (jnp.where(keep, t_c, v), jnp.where(keep, g_c, g),
                    jnp.where(jnp.logical_or(keep, g >= ksel_f), 1.0, 0.0))

        t, g_t, _ = lax.while_loop(walk_cond, walk_body, (v0, g0, done0))
        c_gt = count(gt, t)
        need = ksel_f - c_gt
        excess = jnp.logical_and(jnp.logical_and(g_t - c_gt > need, jnp.logical_not(small)), t > HALF_NEG)
        t_sc[rsl, :] = jnp.where(small, NEG_INF, t)
        need_sc[rsl, :] = need
        return jnp.maximum(flag, jnp.max(jnp.where(excess, 1.0, 0.0)))

    tie_flag = lax.fori_loop(0, tq // rs, search, jnp.float32(0.0))

    q = q_ref[...]
    q0a, q0b = _split_halves(q[:, :LANES], lane)
    q1a, q1b = _split_halves(q[:, LANES:], lane)
    qq = (jnp.concatenate([q0a, q0b], axis=0), jnp.concatenate([q1a, q1b], axis=0))
    carry_sc[...] = jnp.zeros(carry_sc.shape, F32)
    t_all = t_sc[...]

    def bias_step(kb, masked):
        ks = pl.multiple_of(kb * tk, tk)
        sc = s_sc[kb]
        vis = _visible(r0, ks, tq, tk) if masked else None

        def store(sel):
            if masked:
                sel = jnp.logical_and(sel, vis)
            s_sc[kb] = jnp.where(sel, 0.0, NEG_INF)

        def plain():
            store(sc >= t_all)

        def with_ties():
            eq = sc == t_all
            if masked:
                eq = jnp.logical_and(eq, vis)
            eqf = jnp.where(eq, 1.0, 0.0)
            upper = (lax.broadcasted_iota(jnp.int32, (tk, tk), 0)
                     < lax.broadcasted_iota(jnp.int32, (tk, tk), 1))
            before = _bdot(eqf.astype(BF16), jnp.where(upper, 1.0, 0.0).astype(BF16)) + carry_sc[...]
            carry_sc[...] = carry_sc[...] + jnp.sum(eqf, axis=1, keepdims=True)
            store(jnp.logical_or(sc > t_all, jnp.logical_and(eq, before < need_sc[...])))

        lax.cond(tie_flag > 0.5, with_ties, plain)

    def bias_full(kb, c):
        bias_step(kb, False)
        return c

    def bias_masked(kb, c):
        bias_step(kb, True)
        return c

    lax.fori_loop(0, n_full, bias_full, 0)
    lax.fori_loop(n_full, n_vis, bias_masked, 0)

    def score(kb, nblk, masked, g):
        ks = pl.multiple_of(kb * tk, tk)
        bias = jnp.concatenate([s_sc[kb + w] for w in range(nblk)], axis=1) if nblk > 1 else s_sc[kb]
        s = _dot_nt(qq[g], k_ref[pl.ds(ks, nblk * tk), g * LANES:(g + 1) * LANES])
        return s + jnp.concatenate([bias, bias], axis=0)

    def values(kb, nblk, g):
        return v_ref[pl.ds(pl.multiple_of(kb * tk, tk), nblk * tk), g * LANES:(g + 1) * LANES]

    o = _two_pass(n_full, n_vis, score, values, m_sc, l_sc, acc_sc, sbuf,
                  (slice(0, 2 * tq), slice(2 * tq, 4 * tq)))
    for pr in range(2):
        lo_h = o[(2 * pr) * tq:(2 * pr + 1) * tq]
        hi_h = o[(2 * pr + 1) * tq:(2 * pr + 2) * tq]
        o_ref[:, pr * LANES:(pr + 1) * LANES] = jnp.where(lane < 64, lo_h, hi_h).astype(BF16)


def _dsa_call(q, qi, w, k, v, ki, *, nb, nq, tq, tk, lp, q_blk0, k_blk0, q_pos0, n_keys, n_bisect):
    nkb = lp // tk
    k_sel = min(DSA_TOPK, n_keys // 4)
    rs = min(64, tq)
    kern = functools.partial(_dsa_kernel, tq=tq, tk=tk, nkb=nkb, q_pos0=q_pos0, n_keys=n_keys,
                             k_sel=k_sel, rs=rs, n_bisect=n_bisect)
    return pl.pallas_call(
        kern,
        grid=(nb, nq),
        in_specs=[pl.BlockSpec((tq, B_WIDTH), lambda b, i: (q_blk0 + b * nq + i, 0)),
                  pl.BlockSpec((tq, B_WIDTH), lambda b, i: (q_blk0 + b * nq + i, 0)),
                  pl.BlockSpec((tq, LANES), lambda b, i: (q_blk0 + b * nq + i, 0)),
                  pl.BlockSpec((lp, B_WIDTH), lambda b, i: (k_blk0 + b, 0)),
                  pl.BlockSpec((lp, B_WIDTH), lambda b, i: (k_blk0 + b, 0)),
                  pl.BlockSpec((lp, LANES), lambda b, i: (k_blk0 + b, 0))],
        out_specs=pl.BlockSpec((tq, B_WIDTH), lambda b, i: (b * nq + i, 0)),
        out_shape=jax.ShapeDtypeStruct((nb * nq * tq, B_WIDTH), BF16),
        scratch_shapes=[pltpu.VMEM((nkb, tq, tk), F32),
                        pltpu.VMEM((4 * tq, LANES), F32), pltpu.VMEM((4 * tq, LANES), F32),
                        pltpu.VMEM((4 * tq, LANES), F32),
                        pltpu.VMEM((tq, 1), F32), pltpu.VMEM((tq, 1), F32), pltpu.VMEM((tq, 1), F32),
                        pltpu.VMEM((IDX_HEADS, tq, LANES), F32),
                        pltpu.VMEM((nkb, 2 * tq, tk), F32)],
        compiler_params=_cparams(("arbitrary", "arbitrary")),
        name="dsa_attention",
    )(q, qi, w, k, v, ki)


def _kb_kernel(x_ref, oa_ref, ob_ref, oc_ref, sha_ref, sca_ref, gta_ref, shm_ref, scm_ref,
               nmix_ref, nffn_ref, wg_ref, wba_ref, wbb_ref, wbc_ref, wout_ref, rw_ref, rb_ref,
               x1_o, h2_o, e_o, g_o, cnt_o, cnt_sc, *, tm, ch):
    x = x_ref[...]
    hb = _modulate(_rms(x, nmix_ref[...]), sca_ref[...], sha_ref[...], tm, ch).astype(BF16)

    def gate(c0):
        return _sigmoid(_bdot(hb, wg_ref[:, c0:c0 + D_MODEL]))

    merged = gate(0) * _bdot(oa_ref[...], wba_ref[...])
    merged = merged + gate(D_MODEL) * _bdot(ob_ref[...], wbb_ref[...])
    merged = merged + gate(2 * D_MODEL) * _bdot(oc_ref[...], wbc_ref[...])
    y = _bdot(merged.astype(BF16), wout_ref[...])
    x1 = x + _scale_rows(y, gta_ref[...], tm, ch)
    x1_o[...] = x1
    h2 = _modulate(_rms(x1, nffn_ref[...]), scm_ref[...], shm_ref[...], tm, ch)
    h2_o[...] = h2.astype(BF16)
    lg = _dot_split(h2, rw_ref[...]) + rb_ref[...]
    lanef = lax.broadcasted_iota(jnp.int32, (tm, LANES), 1).astype(F32)
    e_acc = jnp.zeros((tm, LANES), F32)
    v_acc = jnp.full((tm, LANES), NEG_INF, F32)
    chosen = jnp.zeros((tm, LANES), F32)
    picks = []
    for k in range(TOP_K):
        mx = jnp.max(lg, axis=1, keepdims=True)
        idx = jnp.min(jnp.where(lg == mx, lanef, float(LANES)), axis=1, keepdims=True)
        hit = lanef == idx
        picks.append(hit)
        chosen = jnp.where(hit, 1.0, chosen)
        e_acc = jnp.where(lanef == float(k), idx, e_acc)
        v_acc = jnp.where(lanef == float(k), mx, v_acc)
        lg = jnp.where(hit, BELOW_NEG, lg)
    ex = jnp.where(lanef < float(TOP_K), jnp.exp(v_acc - jnp.max(v_acc, axis=1, keepdims=True)), 0.0)
    g_o[...] = ex / jnp.sum(ex, axis=1, keepdims=True)
    @pl.when(pl.program_id(0) == 0)
    def _():
        cnt_sc[...] = jnp.zeros(cnt_sc.shape, F32)

    earlier = (lax.broadcasted_iota(jnp.int32, (tm, tm), 1) < lax.broadcasted_iota(jnp.int32, (tm, tm), 0))
    before = _bdot(jnp.where(earlier, 1.0, 0.0).astype(BF16), chosen.astype(BF16)) + cnt_sc[0:1, :]
    for k in range(TOP_K):
        rank = jnp.sum(jnp.where(picks[k], before, 0.0), axis=1, keepdims=True)
        e_acc = jnp.where(lanef == float(TOP_K + k), rank, e_acc)
    e_o[...] = e_acc.astype(jnp.int32)
    cnt_sc[...] = cnt_sc[...] + jnp.sum(chosen, axis=0, keepdims=True)
    cnt_o[...] = cnt_sc[...]


def _kb_call(x, oa, ob, oc, mods, nmix, nffn, wg, wba, wbb, wbc, wout, rw, rb):
    n = x.shape[0]
    tm = TOKEN_BLOCK
    ch = tm // MOD_ROWS

    def row(i):
        return (i, 0)

    def full(a):
        return pl.BlockSpec(a.shape, lambda i: (0, 0))

    in_specs = ([pl.BlockSpec((tm, D_MODEL), row), pl.BlockSpec((tm, A_WIDTH), row),
                 pl.BlockSpec((tm, B_WIDTH), row), pl.BlockSpec((tm, 512), row)]
                + [pl.BlockSpec((MOD_ROWS, D_MODEL), row) for _ in range(5)]
                + [full(a) for a in (nmix, nffn, wg, wba, wbb, wbc, wout, rw, rb)])
    out_shape = [jax.ShapeDtypeStruct((n, D_MODEL), F32), jax.ShapeDtypeStruct((n, D_MODEL), BF16),
                 jax.ShapeDtypeStruct((n, LANES), jnp.int32), jax.ShapeDtypeStruct((n, LANES), F32),
                 jax.ShapeDtypeStruct((8, LANES), F32)]
    out_specs = [pl.BlockSpec((tm, D_MODEL), row), pl.BlockSpec((tm, D_MODEL), row),
                 pl.BlockSpec((tm, LANES), row), pl.BlockSpec((tm, LANES), row),
                 pl.BlockSpec((8, LANES), lambda i: (0, 0))]
    return pl.pallas_call(
        functools.partial(_kb_kernel, tm=tm, ch=ch),
        grid=(n // tm,),
        in_specs=in_specs,
        out_specs=out_specs,
        out_shape=out_shape,
        scratch_shapes=[pltpu.VMEM((8, LANES), F32)],
        compiler_params=_cparams(("arbitrary",)),
        name="post_attention",
    )(x, oa, ob, oc, *mods, nmix, nffn, wg, wba, wbb, wbc, wout, rw, rb)


def _ke_kernel(be_ref, nu_ref, x_ref, wg_ref, wl_ref, wd_ref, bg_ref, bl_ref, bd_ref, y_ref):
    i = pl.program_id(0)

    @pl.when(i < nu_ref[0])
    def _():
        x = x_ref[...]
        g = jnp.minimum(_bdot(x, wg_ref[0]) + bg_ref[0], SWIGLU_LIMIT)
        l = jnp.clip(_bdot(x, wl_ref[0]) + bl_ref[0], -SWIGLU_LIMIT, SWIGLU_LIMIT)
        act = g * _sigmoid(SWIGLU_ALPHA * g) * (l + 1.0)
        y_ref[...] = _bdot(act.astype(BF16), wd_ref[0]) + bd_ref[0]

    @pl.when(i >= nu_ref[0])
    def _():
        y_ref[...] = jnp.zeros(y_ref.shape, F32)


def _ke_call(blk_e, n_used, xg, wg, wl, wd, bg, bl, bd):
    n_rows = xg.shape[0]
    eb = EXPERT_ROWS
    n_blocks = n_rows // eb
    grid_spec = pltpu.PrefetchScalarGridSpec(
        num_scalar_prefetch=2,
        grid=(n_blocks,),
        in_specs=[pl.BlockSpec((eb, D_MODEL), lambda i, be, nu: (i, 0)),
                  pl.BlockSpec((1, D_MODEL, D_FF), lambda i, be, nu: (be[i], 0, 0)),
                  pl.BlockSpec((1, D_MODEL, D_FF), lambda i, be, nu: (be[i], 0, 0)),
                  pl.BlockSpec((1, D_FF, D_MODEL), lambda i, be, nu: (be[i], 0, 0)),
                  pl.BlockSpec((1, 1, D_FF), lambda i, be, nu: (be[i], 0, 0)),
                  pl.BlockSpec((1, 1, D_FF), lambda i, be, nu: (be[i], 0, 0)),
                  pl.BlockSpec((1, 1, D_MODEL), lambda i, be, nu: (be[i], 0, 0))],
        out_specs=pl.BlockSpec((eb, D_MODEL), lambda i, be, nu: (i, 0)),
    )
    return pl.pallas_call(
        _ke_kernel,
        grid_spec=grid_spec,
        out_shape=jax.ShapeDtypeStruct((n_rows, D_MODEL), F32),
        compiler_params=_cparams(("arbitrary",)),
        name="moe_experts",
    )(blk_e, n_used, xg, wg, wl, wd, bg, bl, bd)


def _kc_kernel(dcur_ref, dnext_ref, x1_ref, yr_hbm, g_ref, gtm_ref, fn_ref, x2_o, y_o, buf, sem,
               *, tm, ch):
    i = pl.program_id(0)
    nblk = pl.num_programs(0)
    slot = lax.rem(i, 2)
    nrow = TOP_K * tm

    def row_copy(d_ref, s, r):
        return pltpu.make_async_copy(yr_hbm.at[pl.ds(d_ref[0, 0, r], 1), :],
                                     buf.at[s, pl.ds(r, 1), :], sem.at[s])

    def issue(d_ref, s):
        def body(r, c):
            row_copy(d_ref, s, r).start()
            return c
        lax.fori_loop(0, nrow, body, 0)

    @pl.when(i == 0)
    def _():
        issue(dcur_ref, 0)

    @pl.when(i + 1 < nblk)
    def _():
        issue(dnext_ref, 1 - slot)

    pltpu.make_async_copy(buf.at[slot], buf.at[slot], sem.at[slot]).wait()
    gate = g_ref[...]
    ffn = gate[:, 0:1] * buf[slot, 0:tm, :]
    for k in range(1, TOP_K):
        ffn = ffn + gate[:, k:k + 1] * buf[slot, k * tm:(k + 1) * tm, :]
    x2 = x1_ref[...] + _scale_rows(ffn, gtm_ref[...], tm, ch)
    x2_o[...] = x2
    y_o[...] = _rms(x2, fn_ref[...])


def _kc_call(x1, yr, dest_blocks, gate, gtm, fnorm):
    n = x1.shape[0]
    tm = TOKEN_BLOCK
    ch = tm // MOD_ROWS
    nblk = n // tm
    smem = functools.partial(pl.BlockSpec, (1, 1, TOP_K * tm), memory_space=pltpu.SMEM)
    return pl.pallas_call(
        functools.partial(_kc_kernel, tm=tm, ch=ch),
        grid=(nblk,),
        in_specs=[smem(lambda i: (i, 0, 0)),
                  smem(lambda i: (jnp.minimum(i + 1, nblk - 1), 0, 0)),
                  pl.BlockSpec((tm, D_MODEL), lambda i: (i, 0)),
                  pl.BlockSpec(memory_space=pl.ANY),
                  pl.BlockSpec((tm, LANES), lambda i: (i, 0)),
                  pl.BlockSpec((MOD_ROWS, D_MODEL), lambda i: (i, 0)),
                  pl.BlockSpec((1, D_MODEL), lambda i: (0, 0))],
        out_specs=[pl.BlockSpec((tm, D_MODEL), lambda i: (i, 0)), pl.BlockSpec((tm, D_MODEL), lambda i: (i, 0))],
        out_shape=[jax.ShapeDtypeStruct((n, D_MODEL), F32), jax.ShapeDtypeStruct((n, D_MODEL), F32)],
        scratch_shapes=[pltpu.VMEM((2, TOP_K * tm, D_MODEL), F32), pltpu.SemaphoreType.DMA((2,))],
        compiler_params=_cparams(("arbitrary",)),
        name="moe_combine",
    )(dest_blocks, dest_blocks, x1, yr, gate, gtm, fnorm)


def _rope_tables(pos):
    lane = np.arange(LANES)
    inv32 = ROPE_THETA ** (-jnp.arange(32, dtype=F32) / 32)
    inv16 = ROPE_THETA ** (-jnp.arange(16, dtype=F32) / 16)
    ang64 = pos[:, None] * inv32[None, :][:, lane & 31]
    ang32 = pos[:, None] * inv16[None, :][:, lane & 15]
    sign64 = jnp.asarray(np.where((lane & 63) < 32, -1.0, 1.0), F32)[None, :]
    sign32 = jnp.asarray(np.where((lane & 31) < 16, -1.0, 1.0), F32)[None, :]
    in_m = jnp.asarray(lane < C_ROPE)[None, :]
    in_q = jnp.asarray((lane >= C_NOPE) & (lane < C_NOPE + C_ROPE))[None, :]
    cos64, sin64 = jnp.cos(ang64), jnp.sin(ang64) * sign64
    cos32, sin32 = jnp.cos(ang32), jnp.sin(ang32) * sign32
    return (cos64, sin64,
            jnp.where(in_m, cos32, 1.0), jnp.where(in_m, sin32, 0.0),
            jnp.where(in_q, cos32, 1.0), jnp.where(in_q, sin32, 0.0))


def _layer_weights(l, w_in, mla_w_uq, mla_w_ukv, w_br_c, router_w, router_b, exp_w_gu, exp_b_gu):
    wi = w_in[l]
    z = lambda n: jnp.zeros((D_MODEL, n), F32)
    w1 = jnp.concatenate([
        wi[:, _OFF_AQ:_OFF_BIK],
        wi[:, _OFF_BIK:_OFF_BIW], wi[:, _OFF_BIK:_OFF_BIW],
        wi[:, _OFF_CKR:_OFF_GATES], wi[:, _OFF_BIW:_OFF_CQ], z(LANES - C_ROPE - IDX_HEADS),
        wi[:, _OFF_CQ:_OFF_CKV], wi[:, _OFF_CKV:_OFF_CKR]], axis=1).astype(BF16)
    wg = wi[:, _OFF_GATES:].astype(BF16)
    wuq = mla_w_uq[l].reshape(Q_LORA, C_HEADS, C_NOPE + C_ROPE)
    wuq = jnp.pad(wuq, ((0, 0), (0, 0), (0, LANES - C_NOPE - C_ROPE))).reshape(Q_LORA, C_HEADS * LANES).astype(BF16)
    wukv = mla_w_ukv[l].reshape(KV_LORA, C_HEADS, C_NOPE + C_V)
    wk = jnp.pad(wukv[:, :, :C_NOPE], ((0, 0), (0, 0), (0, LANES - C_NOPE))).reshape(KV_LORA, C_HEADS * LANES).astype(BF16)
    wv = jnp.pad(wukv[:, :, C_NOPE:], ((0, 0), (0, 0), (0, LANES - C_V))).reshape(KV_LORA, C_HEADS * LANES).astype(BF16)
    e = np.zeros((LANES, C_HEADS * LANES), np.float32)
    for h in range(C_HEADS):
        e[np.arange(C_ROPE), h * LANES + C_NOPE + np.arange(C_ROPE)] = 1.0
    we = jnp.asarray(e, BF16)
    wbc = jnp.pad(w_br_c[l].reshape(C_HEADS, C_V, D_MODEL), ((0, 0), (0, LANES - C_V), (0, 0)))
    wbc = wbc.reshape(C_HEADS * LANES, D_MODEL).astype(BF16)
    rw = jnp.pad(router_w[l], ((0, 0), (0, LANES - N_EXPERTS)))
    rb = jnp.pad(router_b[l], (0, LANES - N_EXPERTS), constant_values=NEG_INF).reshape(1, LANES)
    wgl = exp_w_gu[l].reshape(N_EXPERTS, D_MODEL, D_FF, 2)
    bgl = exp_b_gu[l].reshape(N_EXPERTS, 1, D_FF, 2)
    return dict(w1=w1, wg=wg, wuq=wuq, wk=wk, wv=wv, we=we, wbc=wbc, rw=rw, rb=rb,
                e_wg=wgl[..., 0].astype(BF16), e_wl=wgl[..., 1].astype(BF16),
                e_bg=bgl[..., 0], e_bl=bgl[..., 1])


def _route(e_pad, cnt, n_rows_pad):
    eb = EXPERT_ROWS
    flat_e = e_pad[:, :TOP_K].reshape(-1)
    rank = e_pad[:, TOP_K:2 * TOP_K].reshape(-1)
    nk = flat_e.shape[0]
    counts = cnt[0, :N_EXPERTS].astype(jnp.int32)
    padded = (counts + eb - 1) // eb * eb
    pad_end = jnp.cumsum(padded)
    pad_start = pad_end - padded
    dest = pad_start[flat_e] + rank
    row_tok = jnp.zeros((n_rows_pad,), jnp.int32).at[dest].set(jnp.arange(nk, dtype=jnp.int32) // TOP_K)
    n_blocks = n_rows_pad // eb
    blk_start = jnp.arange(n_blocks, dtype=jnp.int32) * eb
    blk_e = jnp.minimum(jnp.searchsorted(pad_end, blk_start, side='right'), N_EXPERTS - 1).astype(jnp.int32)
    n_used = (pad_end[-1] // eb).astype(jnp.int32).reshape(1)
    return dest, row_tok, blk_e, n_used


def kernel(x_prompt, x_sample, cache_a_k, cache_a_v, cache_b_k, cache_b_v, cache_b_idx_k, cache_c_latent, cache_c_k_rope, c_prompt, c_sample, w_ada, b_ada, norm_mix, norm_ffn, w_in, diff_lq1, diff_lk1, diff_lq2, diff_lk2, diff_subln, mla_q_norm, mla_w_uq, mla_kv_norm, mla_w_ukv, w_br_a, w_br_b, w_br_c, w_out, router_w, router_b, exp_w_gu, exp_b_gu, exp_w_down, exp_b_down, final_norm):
    depth = w_ada.shape[0]
    bp, tp, _ = x_prompt.shape
    bs, ts, _ = x_sample.shape
    past = cache_c_latent.shape[2]
    n_p, n_s = bp * tp, bs * ts
    n = n_p + n_s
    tm = TOKEN_BLOCK
    ch = tm // MOD_ROWS
    assert ts == CHUNK and tp % tm == 0 and n_s % tm == 0 and past % CHUNK == 0

    x = jnp.concatenate([x_prompt.reshape(n_p, D_MODEL), x_sample.reshape(n_s, D_MODEL)], axis=0)

    n_seq = bp + bs
    c_all = jnp.concatenate([c_prompt, c_sample], axis=0)
    c_pad = jnp.pad(c_all, ((0, (-n_seq) % 8), (0, 0)))
    mod = _ada_call(c_pad, w_ada, b_ada)
    def per_chunk(m, reps):
        return jnp.broadcast_to(m[:, :, None, :], m.shape[:2] + (reps, m.shape[2])).reshape(depth, -1, m.shape[2])

    mod_rows = jnp.concatenate([per_chunk(mod[:, :bp], tp // ch), per_chunk(mod[:, bp:n_seq], ts // ch)],
                               axis=1)

    pos = jnp.concatenate([jnp.arange(tp, dtype=F32),
                           jnp.tile(past + jnp.arange(ts, dtype=F32), tm // ts)])
    tabs = _rope_tables(pos)

    tq_p = min(256, tp)
    tk_p = min(512, tp)
    nq_p = tp // tq_p
    tq_c = min(512, tp)
    l_s = past + ts
    tk_s = 384
    lp_s = -(-l_s // tk_s) * tk_s
    n_rows_pad = -(-(n * TOP_K + N_EXPERTS * (EXPERT_ROWS - 1)) // EXPERT_ROWS) * EXPERT_ROWS

    def with_cache(cache_l, new, width):
        parts = [cache_l.reshape(bs, past, width).astype(BF16), new.reshape(bs, ts, width)]
        if lp_s > l_s:
            parts.append(jnp.zeros((bs, lp_s - l_s, width), BF16))
        return jnp.concatenate(parts, axis=1).reshape(bs * lp_s, width)

    caches = [[] for _ in range(7)]
    y = None
    for l in range(depth):
        lam_init = 0.8 - 0.6 * math.exp(-0.3 * l)
        wl = _layer_weights(l, w_in, mla_w_uq, mla_w_ukv, w_br_c, router_w, router_b, exp_w_gu, exp_b_gu)
        m6 = [mod_rows[l, :, j * D_MODEL:(j + 1) * D_MODEL] for j in range(6)]
        sh_a, sc_a, gt_a, sh_m, sc_m, gt_m = m6
        nmix = norm_mix[l].reshape(1, D_MODEL)
        nffn = norm_ffn[l].reshape(1, D_MODEL)

        (ak, av, bk, bv, bik, clat, ckr, misc,
         aq_b, ak_b, av_b, bq_b, bk_b, bv_b, biq_b, bik2_b, cq_b, ck_b, cv_b) = _ka_call(
            x, sh_a, sc_a, tabs, nmix, wl['w1'], mla_q_norm[l].reshape(1, Q_LORA), wl['wuq'],
            mla_kv_norm[l].reshape(1, KV_LORA), wl['wk'], wl['we'], wl['wv'],
            n_prompt_blocks=n_p // tm, tab_blocks=tp // tm)
        for i, a in enumerate((ak, av, bk, bv, bik, clat, ckr)):
            caches[i].append(a)

        lqk = jnp.pad(jnp.stack([diff_lq1[l], diff_lk1[l], diff_lq2[l], diff_lk2[l]]),
                      ((0, 4), (0, LANES - A_HD)))
        sub = diff_subln[l].reshape(1, 2 * A_HD)
        oa_p = _diff_call(lqk, sub, aq_b, ak_b, av_b, nb=bp, nq=nq_p, tq=tq_p, tk=tk_p, lp=tp,
                          q_blk0=0, k_blk0=0, q_pos0=0, lam_init=lam_init)
        ak_s = with_cache(cache_a_k[l], ak_b[n_p:], A_WIDTH)
        av_s = with_cache(cache_a_v[l], av_b[n_p:], A_WIDTH)
        oa_s = _diff_call(lqk, sub, aq_b, ak_s, av_s, nb=bs, nq=1, tq=ts, tk=tk_s, lp=lp_s,
                          q_blk0=n_p // ts, k_blk0=0, q_pos0=past, lam_init=lam_init)
        ob_p = _dsa_call(bq_b, biq_b, misc, bk_b, bv_b, bik2_b, nb=bp, nq=nq_p, tq=tq_p, tk=tk_p, lp=tp,
                         q_blk0=0, k_blk0=0, q_pos0=0, n_keys=tp, n_bisect=14)
        bk_s = with_cache(cache_b_k[l], bk_b[n_p:], B_WIDTH)
        bv_s = with_cache(cache_b_v[l], bv_b[n_p:], B_WIDTH)
        cik = cache_b_idx_k[l].reshape(bs, past, IDX_DIM)
        bik_s = with_cache(jnp.concatenate([cik, cik], axis=-1), bik2_b[n_p:], LANES)
        ob_s = _dsa_call(bq_b, biq_b, misc, bk_s, bv_s, bik_s, nb=bs, nq=1, tq=ts, tk=tk_s, lp=lp_s,
                         q_blk0=n_p // ts, k_blk0=0, q_pos0=past, n_keys=l_s, n_bisect=12)
        oc_p = _mla_call(cq_b, ck_b, cv_b, nb=bp, nq=tp // tq_c, tq=tq_c, tk=tk_p, lp=tp,
                         q_blk0=0, k_blk0=0, q_pos0=0)
        lat_c = cache_c_latent[l].reshape(bs * past, KV_LORA)
        kr_c = jnp.pad(cache_c_k_rope[l].reshape(bs * past, C_ROPE), ((0, 0), (0, LANES - C_ROPE)))
        ck_c, cv_c = _mla_kv_call(lat_c, kr_c, wl['wk'], wl['we'], wl['wv'])
        ck_s = with_cache(ck_c, ck_b[n_p:], 512)
        cv_s = with_cache(cv_c, cv_b[n_p:], 512)
        oc_s = _mla_call(cq_b, ck_s, cv_s, nb=bs, nq=1, tq=ts, tk=tk_s, lp=lp_s,
                         q_blk0=n_p // ts, k_blk0=0, q_pos0=past)

        oa = jnp.concatenate([oa_p, oa_s], axis=0)
        ob = jnp.concatenate([ob_p, ob_s], axis=0)
        oc = jnp.concatenate([oc_p, oc_s], axis=0)

        x1, h2, e_pad, g_pad, cnt = _kb_call(
            x, oa, ob, oc, (sh_a, sc_a, gt_a, sh_m, sc_m), nmix, nffn, wl['wg'],
            w_br_a[l].astype(BF16), w_br_b[l].astype(BF16), wl['wbc'], w_out[l].astype(BF16),
            wl['rw'], wl['rb'])

        dest, row_tok, blk_e, n_used = _route(e_pad, cnt, n_rows_pad)
        xg = jnp.take(h2, row_tok, axis=0)
        yr = _ke_call(blk_e, n_used, xg, wl['e_wg'], wl['e_wl'], exp_w_down[l].astype(BF16),
                      wl['e_bg'], wl['e_bl'], exp_b_down[l].reshape(N_EXPERTS, 1, D_MODEL))
        dest_blocks = dest.reshape(n // tm, tm, TOP_K).transpose(0, 2, 1).reshape(n // tm, 1, TOP_K * tm)
        x, y = _kc_call(x1, yr, dest_blocks, g_pad, gt_m, final_norm.reshape(1, D_MODEL))

    def split(a, tail):
        a = jnp.stack(a, axis=0)
        return (a[:, :n_p].reshape((depth, bp, tp) + tail), a[:, n_p:].reshape((depth, bs, ts) + tail))

    tails = ((A_HEADS, 2 * A_HD), (A_HEADS, 2 * A_HD), (B_HEADS, B_HD), (B_HEADS, B_HD),
             (IDX_DIM,), (KV_LORA,), (C_ROPE,))
    ps = [split(c, t) for c, t in zip(caches, tails)]
    y_prompt = y[:n_p].reshape(bp, tp, D_MODEL)
    y_sample = y[n_p:].reshape(bs, ts, D_MODEL)
    return (y_prompt, y_sample) + tuple(p[0] for p in ps) + tuple(p[1] for p in ps)
```

```python
import functools
import math

import numpy as np
import jax
import jax.numpy as jnp
from jax import lax
from jax.experimental import pallas as pl
from jax.experimental.pallas import tpu as pltpu

F32 = jnp.float32
BF16 = jnp.bfloat16

D_MODEL = 1024
CHUNK = 64
ROPE_THETA = 10000.0
NORM_EPS = 1e-6
NEG_INF = -1e30
HALF_NEG = -5e29
BELOW_NEG = -3e38
BIG_POS = 3e38
LOG2E = 1.4426950408889634

A_HEADS, A_HD = 4, 64
B_HEADS, B_HD = 4, 64
IDX_HEADS, IDX_DIM = 4, 64
DSA_TOPK = 256
C_HEADS, C_NOPE, C_ROPE, C_V = 4, 64, 32, 64
Q_LORA, KV_LORA = 256, 128
N_EXPERTS, TOP_K = 32, 4
D_FF = D_MODEL
SWIGLU_LIMIT = 7.0
SWIGLU_ALPHA = 1.702

A_WIDTH = A_HEADS * 2 * A_HD
B_WIDTH = B_HEADS * B_HD
LANES = 128

_OFF_AQ, _OFF_AK, _OFF_AV = 0, 512, 1024
_OFF_BQ, _OFF_BK, _OFF_BV = 1536, 1792, 2048
_OFF_BIQ, _OFF_BIK, _OFF_BIW = 2304, 2560, 2624
_OFF_CQ, _OFF_CKV, _OFF_CKR = 2628, 2884, 3012
_OFF_GATES = 3044
_P_AQ, _P_AK, _P_AV = 0, 512, 1024
_P_BQ, _P_BK, _P_BV = 1536, 1792, 2048
_P_BIQ, _P_BIK2, _P_MISC, _P_CQ, _P_CKV, _P_END = 2304, 2560, 2688, 2816, 3072, 3200
_MISC_BIW = 32

TOKEN_BLOCK = 256
MOD_ROWS = 8
EXPERT_ROWS = 256
VMEM_LIMIT = 56 * 1024 * 1024


def _cparams(sem, vmem=VMEM_LIMIT):
    return pltpu.CompilerParams(dimension_semantics=sem, vmem_limit_bytes=vmem)


def _rms(xf, g):
    return xf * lax.rsqrt(jnp.mean(xf * xf, axis=-1, keepdims=True) + NORM_EPS) * g


def _sigmoid(x):
    return 1.0 / (1.0 + jnp.exp(-x))


def _bdot(a, b):
    return jnp.dot(a, b, preferred_element_type=F32)


def _dot_nt(a, b):
    return lax.dot_general(a, b, (((1,), (1,)), ((), ())), preferred_element_type=F32)


def _dot_split(a, b):
    a_hi = a.astype(BF16)
    b_hi = b.astype(BF16)
    a_lo = (a - a_hi.astype(F32)).astype(BF16)
    b_lo = (b - b_hi.astype(F32)).astype(BF16)
    return _bdot(a_hi, b_hi) + (_bdot(a_hi, b_lo) + _bdot(a_lo, b_hi))


def _modulate(xn, sc, sh, rows, ch):
    n = rows // ch
    y = xn.reshape(n, ch, D_MODEL) * (1.0 + sc)[:, None, :] + sh[:, None, :]
    return y.reshape(rows, D_MODEL)


def _scale_rows(y, g, rows, ch):
    n = rows // ch
    return (y.reshape(n, ch, D_MODEL) * g[:, None, :]).reshape(rows, D_MODEL)


def _ada_kernel(c_ref, w_ref, b_ref, o_ref):
    c = c_ref[...]
    s = c * _sigmoid(c)
    o_ref[0] = _dot_split(s, w_ref[0]) + b_ref[0]


def _ada_call(c_pad, w_ada, b_ada):
    depth = w_ada.shape[0]
    mp = c_pad.shape[0]
    return pl.pallas_call(
        _ada_kernel,
        grid=(depth, 6),
        in_specs=[
            pl.BlockSpec((mp, D_MODEL), lambda l, j: (0, 0)),
            pl.BlockSpec((1, D_MODEL, D_MODEL), lambda l, j: (l, 0, j)),
            pl.BlockSpec((1, 1, D_MODEL), lambda l, j: (l, 0, j)),
        ],
        out_specs=pl.BlockSpec((1, mp, D_MODEL), lambda l, j: (l, 0, j)),
        out_shape=jax.ShapeDtypeStruct((depth, mp, 6 * D_MODEL), F32),
        compiler_params=_cparams(("arbitrary", "arbitrary")),
        name="ada_mod",
    )(c_pad, w_ada, b_ada.reshape(depth, 1, 6 * D_MODEL))


def _rope_partner(x, lane, half):
    first = (lane & (2 * half - 1)) < half
    return jnp.where(first, pltpu.roll(x, LANES - half, 1), pltpu.roll(x, half, 1))


def _ka_kernel(x_ref, sh_ref, sc_ref, cos_ref, sin_ref, cosm_ref, sinm_ref, cosq_ref, sinq_ref,
               nmix_ref, w1_ref, qn_ref, wuq_ref, kvn_ref, wk_ref, we_ref, wv_ref,
               ak_o, av_o, bk_o, bv_o, bik_o, clat_o, ckr_o, misc_o,
               aq_b, ak_b, av_b, bq_b, bk_b, bv_b, biq_b, bik2_b, cq_b, ck_b, cv_b, *, tm, ch):
    x = x_ref[...]
    h = _modulate(_rms(x, nmix_ref[...]), sc_ref[...], sh_ref[...], tm, ch)
    hb = h.astype(BF16)
    lane = lax.broadcasted_iota(jnp.int32, (tm, LANES), 1)
    cos = cos_ref[...]
    sin = sin_ref[...]

    def proj(c0, c1):
        return _bdot(hb, w1_ref[:, c0:c1])

    def rope64(xb):
        return xb * cos + _rope_partner(xb, lane, 32) * sin

    a_scale = (A_HD ** -0.5) * LOG2E
    b_scale = (B_HD ** -0.5) * LOG2E
    i_scale = IDX_DIM ** -0.5
    c_scale = ((C_NOPE + C_ROPE) ** -0.5) * LOG2E

    p = proj(_P_AQ, _P_AK)
    for c in range(A_WIDTH // LANES):
        sl = slice(c * LANES, (c + 1) * LANES)
        aq_b[:, sl] = (rope64(p[:, sl]) * a_scale).astype(BF16)
    p = proj(_P_AK, _P_AV)
    for c in range(A_WIDTH // LANES):
        sl = slice(c * LANES, (c + 1) * LANES)
        r = rope64(p[:, sl])
        ak_o[:, sl] = r
        ak_b[:, sl] = r.astype(BF16)
    p = proj(_P_AV, _P_BQ)
    av_o[...] = p
    av_b[...] = p.astype(BF16)
    p = proj(_P_BQ, _P_BK)
    for c in range(B_WIDTH // LANES):
        sl = slice(c * LANES, (c + 1) * LANES)
        bq_b[:, sl] = (rope64(p[:, sl]) * b_scale).astype(BF16)
    p = proj(_P_BK, _P_BV)
    for c in range(B_WIDTH // LANES):
        sl = slice(c * LANES, (c + 1) * LANES)
        r = rope64(p[:, sl])
        bk_o[:, sl] = r
        bk_b[:, sl] = r.astype(BF16)
    p = proj(_P_BV, _P_BIQ)
    bv_o[...] = p
    bv_b[...] = p.astype(BF16)
    p = proj(_P_BIQ, _P_BIK2)
    for c in range(B_WIDTH // LANES):
        sl = slice(c * LANES, (c + 1) * LANES)
        biq_b[:, sl] = (rope64(p[:, sl]) * i_scale).astype(BF16)
    r = rope64(proj(_P_BIK2, _P_MISC))
    bik_o[...] = r[:, :IDX_DIM]
    bik2_b[...] = r.astype(BF16)
    pm = proj(_P_MISC, _P_CQ)
    misc = pm * cosm_ref[...] + _rope_partner(pm, lane, 16) * sinm_ref[...]
    misc_o[...] = misc
    ckr_o[...] = misc[:, :C_ROPE]
    qlat = _rms(proj(_P_CQ, _P_CKV), qn_ref[...]).astype(BF16)
    cqf = _bdot(qlat, wuq_ref[...])
    cosq = cosq_ref[...]
    sinq = sinq_ref[...]
    for c in range(C_HEADS):
        sl = slice(c * LANES, (c + 1) * LANES)
        xb = cqf[:, sl]
        cq_b[:, sl] = ((xb * cosq + _rope_partner(xb, lane, 16) * sinq) * c_scale).astype(BF16)
    clat = _rms(proj(_P_CKV, _P_END), kvn_ref[...])
    clat_o[...] = clat
    clb = clat.astype(BF16)
    ck_b[...] = (_bdot(clb, wk_ref[...]) + _bdot(misc.astype(BF16), we_ref[...])).astype(BF16)
    cv_b[...] = _bdot(clb, wv_ref[...]).astype(BF16)


def _ka_call(x, sh, sc, tabs, nmix, w1, qn, wuq, kvn, wk, we, wv, *, n_prompt_blocks, tab_blocks):
    n = x.shape[0]
    tm = TOKEN_BLOCK
    ch = tm // MOD_ROWS
    nblk = n // tm

    def row(i):
        return (i, 0)

    def tab(i):
        return (jnp.where(i < n_prompt_blocks, i % tab_blocks, tab_blocks), 0)

    def const(i):
        return (0, 0)

    def full(a):
        return pl.BlockSpec(a.shape, const)

    widths_f32 = (A_WIDTH, A_WIDTH, B_WIDTH, B_WIDTH, IDX_DIM, KV_LORA, C_ROPE, LANES)
    widths_b16 = (A_WIDTH, A_WIDTH, A_WIDTH, B_WIDTH, B_WIDTH, B_WIDTH, B_WIDTH, LANES, 512, 512, 512)
    out_shape = ([jax.ShapeDtypeStruct((n, w), F32) for w in widths_f32]
                 + [jax.ShapeDtypeStruct((n, w), BF16) for w in widths_b16])
    out_specs = [pl.BlockSpec((tm, w), row) for w in widths_f32 + widths_b16]
    in_specs = ([pl.BlockSpec((tm, D_MODEL), row),
                 pl.BlockSpec((MOD_ROWS, D_MODEL), row),
                 pl.BlockSpec((MOD_ROWS, D_MODEL), row)]
                + [pl.BlockSpec((tm, LANES), tab) for _ in range(6)]
                + [full(a) for a in (nmix, w1, qn, wuq, kvn, wk, we, wv)])
    return pl.pallas_call(
        functools.partial(_ka_kernel, tm=tm, ch=ch),
        grid=(nblk,),
        in_specs=in_specs,
        out_specs=out_specs,
        out_shape=out_shape,
        compiler_params=_cparams(("arbitrary",)),
        name="pre_attention",
    )(x, sh, sc, *tabs, nmix, w1, qn, wuq, kvn, wk, we, wv)


def _mla_kv_kernel(lat_ref, kr_ref, wk_ref, we_ref, wv_ref, ck_o, cv_o):
    lb = lat_ref[...].astype(BF16)
    ck_o[...] = (_bdot(lb, wk_ref[...]) + _bdot(kr_ref[...].astype(BF16), we_ref[...])).astype(BF16)
    cv_o[...] = _bdot(lb, wv_ref[...]).astype(BF16)


def _mla_kv_call(lat, krp, wk, we, wv):
    n = lat.shape[0]
    tm = 512
    return pl.pallas_call(
        _mla_kv_kernel,
        grid=(n // tm,),
        in_specs=[pl.BlockSpec((tm, KV_LORA), lambda i: (i, 0)),
                  pl.BlockSpec((tm, LANES), lambda i: (i, 0)),
                  pl.BlockSpec(wk.shape, lambda i: (0, 0)),
                  pl.BlockSpec(we.shape, lambda i: (0, 0)),
                  pl.BlockSpec(wv.shape, lambda i: (0, 0))],
        out_specs=[pl.BlockSpec((tm, 512), lambda i: (i, 0)), pl.BlockSpec((tm, 512), lambda i: (i, 0))],
        out_shape=[jax.ShapeDtypeStruct((n, 512), BF16), jax.ShapeDtypeStruct((n, 512), BF16)],
        compiler_params=_cparams(("arbitrary",)),
        name="mla_cache_kv",
    )(lat, krp, wk, we, wv)


def _block_range(r0, tq, tk, nkb):
    n_full = jnp.minimum((r0 + CHUNK) // tk, nkb)
    n_vis = jnp.minimum((r0 + tq + tk - 1) // tk, nkb)
    return n_full, n_vis


def _visible(r0, ks, rows, tk):
    rpos = r0 + lax.broadcasted_iota(jnp.int32, (rows, tk), 0)
    kpos = ks + lax.broadcasted_iota(jnp.int32, (rows, tk), 1)
    return kpos < (((rpos >> 6) + 1) << 6)


def _fold_max(s, m_sc, rows):
    mp = m_sc[rows, :]
    for j in range(s.shape[1] // LANES):
        mp = jnp.maximum(mp, s[:, j * LANES:(j + 1) * LANES])
    m_sc[rows, :] = mp


def _finish_max(m_sc, rows):
    mp = m_sc[rows, :]
    m_sc[rows, :] = jnp.broadcast_to(jnp.max(mp, axis=1, keepdims=True), mp.shape)


def _accumulate(s, vblk, m_sc, l_sc, acc_sc, rows):
    m = m_sc[rows, :]
    lp = l_sc[rows, :]
    ps = []
    for j in range(s.shape[1] // LANES):
        pj = jnp.exp2(s[:, j * LANES:(j + 1) * LANES] - m)
        lp = lp + pj
        ps.append(pj.astype(BF16))
    l_sc[rows, :] = lp
    acc_sc[rows, :] = acc_sc[rows, :] + _bdot(jnp.concatenate(ps, axis=1), vblk)


WIDE = 4


def _two_pass(n_full, n_vis, score, values, m_sc, l_sc, acc_sc, sbuf, groups):
    tk = sbuf.shape[2]
    m_sc[...] = jnp.full(m_sc.shape, NEG_INF, F32)
    l_sc[...] = jnp.zeros(l_sc.shape, F32)
    acc_sc[...] = jnp.zeros(acc_sc.shape, F32)

    for g, rows in enumerate(groups):
        def keep(kb, nblk, masked):
            s = score(kb, nblk, masked, g)
            for w in range(nblk):
                sbuf[kb + w] = s[:, w * tk:(w + 1) * tk]
            _fold_max(s, m_sc, rows)

        def wide_body(j, c):
            keep(j * WIDE, WIDE, False)
            return c

        def full_body(kb, c):
            keep(kb, 1, False)
            return c

        def masked_body(kb, c):
            keep(kb, 1, True)
            return c

        n_wide = n_full // WIDE
        lax.fori_loop(0, n_wide, wide_body, 0)
        lax.fori_loop(n_wide * WIDE, n_full, full_body, 0)
        lax.fori_loop(n_full, n_vis, masked_body, 0)
        _finish_max(m_sc, rows)

        def acc_wide(j, c):
            kb = j * WIDE
            s = jnp.concatenate([sbuf[kb + w] for w in range(WIDE)], axis=1)
            _accumulate(s, values(kb, WIDE, g), m_sc, l_sc, acc_sc, rows)
            return c

        def acc_body(kb, c):
            _accumulate(sbuf[kb], values(kb, 1, g), m_sc, l_sc, acc_sc, rows)
            return c

        n_wide = n_vis // WIDE
        lax.fori_loop(0, n_wide, acc_wide, 0)
        lax.fori_loop(n_wide * WIDE, n_vis, acc_body, 0)
    return acc_sc[...] / jnp.sum(l_sc[...], axis=1, keepdims=True)


def _split_halves(q, lane=None):
    lane1 = lax.broadcasted_iota(jnp.int32, (1, LANES), 1)
    lo = jnp.where(lane1 < 64, 1.0, 0.0).astype(q.dtype)
    return q * lo, q * (1.0 - lo).astype(q.dtype)


def _diff_kernel(lqk_ref, sub_ref, q_ref, k_ref, v_ref, o_ref, m_sc, l_sc, acc_sc, sbuf,
                 *, tq, tk, nkb, q_pos0, lam_init):
    i = pl.program_id(2)
    r0 = q_pos0 + i * tq
    n_full, n_vis = _block_range(r0, tq, tk, nkb)
    lane = lax.broadcasted_iota(jnp.int32, (tq, LANES), 1)
    qq = jnp.concatenate(_split_halves(q_ref[...], lane), axis=0)

    def score(kb, nblk, masked, g):
        ks = pl.multiple_of(kb * tk, tk)
        s = _dot_nt(qq, k_ref[pl.ds(ks, nblk * tk), :])
        if masked:
            vis = _visible(r0, ks, tq, nblk * tk)
            s = jnp.where(jnp.concatenate([vis, vis], axis=0), s, NEG_INF)
        return s

    def values(kb, nblk, g):
        return v_ref[pl.ds(pl.multiple_of(kb * tk, tk), nblk * tk), :]

    o = _two_pass(n_full, n_vis, score, values, m_sc, l_sc, acc_sc, sbuf, (slice(0, 2 * tq),))
    o1 = o[:tq]
    o2 = o[tq:]
    lq = lqk_ref[...]
    lam = (jnp.exp(jnp.sum(lq[0:1] * lq[1:2], axis=1, keepdims=True))
           - jnp.exp(jnp.sum(lq[2:3] * lq[3:4], axis=1, keepdims=True)) + lam_init)
    o = o1 - lam * o2
    o = o * lax.rsqrt(jnp.mean(o * o, axis=1, keepdims=True) + NORM_EPS)
    o_ref[...] = (o * sub_ref[...] * (1.0 - lam_init)).astype(BF16)


def _diff_call(lqk, sub, q, k, v, *, nb, nq, tq, tk, lp, q_blk0, k_blk0, q_pos0, lam_init):
    nkb = lp // tk
    kern = functools.partial(_diff_kernel, tq=tq, tk=tk, nkb=nkb, q_pos0=q_pos0, lam_init=lam_init)
    return pl.pallas_call(
        kern,
        grid=(nb, A_HEADS, nq),
        in_specs=[pl.BlockSpec(lqk.shape, lambda b, h, i: (0, 0)),
                  pl.BlockSpec((1, LANES), lambda b, h, i: (0, 0)),
                  pl.BlockSpec((tq, LANES), lambda b, h, i: (q_blk0 + b * nq + i, h)),
                  pl.BlockSpec((lp, LANES), lambda b, h, i: (k_blk0 + b, h)),
                  pl.BlockSpec((lp, LANES), lambda b, h, i: (k_blk0 + b, h))],
        out_specs=pl.BlockSpec((tq, LANES), lambda b, h, i: (b * nq + i, h)),
        out_shape=jax.ShapeDtypeStruct((nb * nq * tq, A_WIDTH), BF16),
        scratch_shapes=([pltpu.VMEM((2 * tq, LANES), F32) for _ in range(3)]
                        + [pltpu.VMEM((nkb, 2 * tq, tk), F32)]),
        compiler_params=_cparams(("arbitrary", "arbitrary", "arbitrary")),
        name="diff_attention",
    )(lqk, sub, q, k, v)


def _mla_kernel(q_ref, k_ref, v_ref, o_ref, m_sc, l_sc, acc_sc, sbuf, *, tq, tk, nkb, q_pos0):
    i = pl.program_id(2)
    r0 = q_pos0 + i * tq
    n_full, n_vis = _block_range(r0, tq, tk, nkb)
    q = q_ref[...]

    def score(kb, nblk, masked, g):
        ks = pl.multiple_of(kb * tk, tk)
        s = _dot_nt(q, k_ref[pl.ds(ks, nblk * tk), :])
        if masked:
            s = jnp.where(_visible(r0, ks, tq, nblk * tk), s, NEG_INF)
        return s

    def values(kb, nblk, g):
        return v_ref[pl.ds(pl.multiple_of(kb * tk, tk), nblk * tk), :]

    o = _two_pass(n_full, n_vis, score, values, m_sc, l_sc, acc_sc, sbuf, (slice(0, tq),))
    o_ref[...] = o.astype(BF16)


def _mla_call(q, k, v, *, nb, nq, tq, tk, lp, q_blk0, k_blk0, q_pos0):
    nkb = lp // tk
    kern = functools.partial(_mla_kernel, tq=tq, tk=tk, nkb=nkb, q_pos0=q_pos0)
    return pl.pallas_call(
        kern,
        grid=(nb, C_HEADS, nq),
        in_specs=[pl.BlockSpec((tq, LANES), lambda b, h, i: (q_blk0 + b * nq + i, h)),
                  pl.BlockSpec((lp, LANES), lambda b, h, i: (k_blk0 + b, h)),
                  pl.BlockSpec((lp, LANES), lambda b, h, i: (k_blk0 + b, h))],
        out_specs=pl.BlockSpec((tq, LANES), lambda b, h, i: (b * nq + i, h)),
        out_shape=jax.ShapeDtypeStruct((nb * nq * tq, C_HEADS * LANES), BF16),
        scratch_shapes=([pltpu.VMEM((tq, LANES), F32) for _ in range(3)]
                        + [pltpu.VMEM((nkb, tq, tk), F32)]),
        compiler_params=_cparams(("arbitrary", "arbitrary", "arbitrary")),
        name="mla_attention",
    )(q, k, v)


def _dsa_kernel(q_ref, qi_ref, w_ref, k_ref, v_ref, ki_ref, o_ref,
                s_sc, m_sc, l_sc, acc_sc, t_sc, need_sc, carry_sc, wrep_sc, sbuf,
                *, tq, tk, nkb, q_pos0, n_keys, k_sel, rs, n_bisect):
    i = pl.program_id(1)
    r0 = q_pos0 + i * tq
    n_full, n_vis = _block_range(r0, tq, tk, nkb)
    lane = lax.broadcasted_iota(jnp.int32, (tq, LANES), 1)
    ksel_f = float(k_sel)

    qi = qi_ref[...]
    qa, qb = _split_halves(qi[:, :LANES], lane)
    qc, qd = _split_halves(qi[:, LANES:], lane)
    qi4 = jnp.concatenate([qa, qb, qc, qd], axis=0)
    wm = w_ref[...]
    for h in range(IDX_HEADS):
        wcol = wm[:, _MISC_BIW + h:_MISC_BIW + h + 1] * (IDX_HEADS ** -0.5)
        wrep_sc[h] = jnp.broadcast_to(wcol, (tq, LANES))

    def score_step(kb, masked):
        ks = pl.multiple_of(kb * tk, tk)
        rel = _dot_nt(qi4, ki_ref[pl.ds(ks, tk), :])
        cols = []
        for j in range(tk // LANES):
            cl = slice(j * LANES, (j + 1) * LANES)
            sc = wrep_sc[0] * jnp.maximum(rel[:tq, cl], 0.0)
            for h in range(1, IDX_HEADS):
                sc = sc + wrep_sc[h] * jnp.maximum(rel[h * tq:(h + 1) * tq, cl], 0.0)
            cols.append(sc)
        sc = jnp.concatenate(cols, axis=1)
        if masked:
            sc = jnp.where(_visible(r0, ks, tq, tk), sc, NEG_INF)
        s_sc[kb] = sc

    def score_full(kb, c):
        score_step(kb, False)
        return c

    def score_masked(kb, c):
        score_step(kb, True)
        return c

    lax.fori_loop(0, n_full, score_full, 0)
    lax.fori_loop(n_full, n_vis, score_masked, 0)

    nl = tk // LANES

    def search(sb, flag):
        rsl = pl.ds(pl.multiple_of(sb * rs, rs), rs)

        def fold(fn, init):
            def body(kb, part):
                s = s_sc[kb, rsl, :]
                for j in range(nl):
                    part = fn(part, s[:, j * LANES:(j + 1) * LANES])
                return part
            return lax.fori_loop(0, n_vis, body, init)

        def bc(x):
            return jnp.broadcast_to(x, (rs, LANES))

        def count(cmp, x):
            xb = bc(x)
            part = fold(lambda p, sj: p + jnp.where(cmp(sj, xb), 1.0, 0.0), jnp.zeros((rs, LANES), F32))
            return jnp.sum(part, axis=1, keepdims=True)

        def max_below(x, strict):
            xb = bc(x)
            if strict:
                part = fold(lambda p, sj: jnp.maximum(p, jnp.where(sj < xb, sj, BELOW_NEG)),
                            jnp.full((rs, LANES), BELOW_NEG, F32))
            else:
                part = fold(lambda p, sj: jnp.maximum(p, jnp.where(sj <= xb, sj, BELOW_NEG)),
                            jnp.full((rs, LANES), BELOW_NEG, F32))
            return jnp.max(part, axis=1, keepdims=True)

        ge = lambda a, b: a >= b
        gt = lambda a, b: a > b

        hi = jnp.max(fold(jnp.maximum, jnp.full((rs, LANES), BELOW_NEG, F32)), axis=1, keepdims=True)
        lo = jnp.min(fold(lambda p, sj: jnp.minimum(p, jnp.where(sj > HALF_NEG, sj, BIG_POS)),
                          jnp.full((rs, LANES), BIG_POS, F32)), axis=1, keepdims=True)
        rpos = r0 + sb * rs + lax.broadcasted_iota(jnp.int32, (rs, 1), 0)
        n_valid = jnp.minimum(((rpos >> 6) + 1) << 6, n_keys)
        small = n_valid < k_sel

        def bisect(_, c):
            lo_c, hi_c = c
            mid = 0.5 * (lo_c + hi_c)
            up = count(ge, mid) >= ksel_f
            return jnp.where(up, mid, lo_c), jnp.where(up, hi_c, mid)

        lo, hi = lax.fori_loop(0, n_bisect, bisect, (lo, hi))

        v0 = max_below(hi, False)
        g0 = count(ge, v0)
        done0 = jnp.where(jnp.logical_or(g0 >= ksel_f, small), 1.0, 0.0)

        def walk_cond(c):
            return jnp.min(c[2]) < 0.5

        def walk_body(c):
            t_c, g_c, done_c = c
            v = max_below(t_c, True)
            g = count(ge, v)
            keep = done_c > 0.5
            return (jnp.where(keep, t_c, v), jnp.where(keep, g_c, g),
                    jnp.where(jnp.logical_or(keep, g >= ksel_f), 1.0, 0.0))

        t, g_t, _ = lax.while_loop(walk_cond, walk_body, (v0, g0, done0))
        c_gt = count(gt, t)
        need = ksel_f - c_gt
        excess = jnp.logical_and(jnp.logical_and(g_t - c_gt > need, jnp.logical_not(small)), t > HALF_NEG)
        t_sc[rsl, :] = jnp.where(small, NEG_INF, t)
        need_sc[rsl, :] = need
        return jnp.maximum(flag, jnp.max(jnp.where(excess, 1.0, 0.0)))

    tie_flag = lax.fori_loop(0, tq // rs, search, jnp.float32(0.0))

    q = q_ref[...]
    q0a, q0b = _split_halves(q[:, :LANES], lane)
    q1a, q1b = _split_halves(q[:, LANES:], lane)
    qq = (jnp.concatenate([q0a, q0b], axis=0), jnp.concatenate([q1a, q1b], axis=0))
    carry_sc[...] = jnp.zeros(carry_sc.shape, F32)
    t_all = t_sc[...]

    def bias_step(kb, masked):
        ks = pl.multiple_of(kb * tk, tk)
        sc = s_sc[kb]
        vis = _visible(r0, ks, tq, tk) if masked else None

        def store(sel):
            if masked:
                sel = jnp.logical_and(sel, vis)
            s_sc[kb] = jnp.where(sel, 0.0, NEG_INF)

        def plain():
            store(sc >= t_all)

        def with_ties():
            eq = sc == t_all
            if masked:
                eq = jnp.logical_and(eq, vis)
            eqf = jnp.where(eq, 1.0, 0.0)
            upper = (lax.broadcasted_iota(jnp.int32, (tk, tk), 0)
                     < lax.broadcasted_iota(jnp.int32, (tk, tk), 1))
            before = _bdot(eqf.astype(BF16), jnp.where(upper, 1.0, 0.0).astype(BF16)) + carry_sc[...]
            carry_sc[...] = carry_sc[...] + jnp.sum(eqf, axis=1, keepdims=True)
            store(jnp.logical_or(sc > t_all, jnp.logical_and(eq, before < need_sc[...])))

        lax.cond(tie_flag > 0.5, with_ties, plain)

    def bias_full(kb, c):
        bias_step(kb, False)
        return c

    def bias_masked(kb, c):
        bias_step(kb, True)
        return c

    lax.fori_loop(0, n_full, bias_full, 0)
    lax.fori_loop(n_full, n_vis, bias_masked, 0)

    def score(kb, nblk, masked, g):
        ks = pl.multiple_of(kb * tk, tk)
        bias = jnp.concatenate([s_sc[kb + w] for w in range(nblk)], axis=1) if nblk > 1 else s_sc[kb]
        s = _dot_nt(qq[g], k_ref[pl.ds(ks, nblk * tk), g * LANES:(g + 1) * LANES])
        return s + jnp.concatenate([bias, bias], axis=0)

    def values(kb, nblk, g):
        return v_ref[pl.ds(pl.multiple_of(kb * tk, tk), nblk * tk), g * LANES:(g + 1) * LANES]

    o = _two_pass(n_full, n_vis, score, values, m_sc, l_sc, acc_sc, sbuf,
                  (slice(0, 2 * tq), slice(2 * tq, 4 * tq)))
    for pr in range(2):
        lo_h = o[(2 * pr) * tq:(2 * pr + 1) * tq]
        hi_h = o[(2 * pr + 1) * tq:(2 * pr + 2) * tq]
        o_ref[:, pr * LANES:(pr + 1) * LANES] = jnp.where(lane < 64, lo_h, hi_h).astype(BF16)


def _dsa_call(q, qi, w, k, v, ki, *, nb, nq, tq, tk, lp, q_blk0, k_blk0, q_pos0, n_keys, n_bisect):
    nkb = lp // tk
    k_sel = min(DSA_TOPK, n_keys // 4)
    rs = min(64, tq)
    kern = functools.partial(_dsa_kernel, tq=tq, tk=tk, nkb=nkb, q_pos0=q_pos0, n_keys=n_keys,
                             k_sel=k_sel, rs=rs, n_bisect=n_bisect)
    return pl.pallas_call(
        kern,
        grid=(nb, nq),
        in_specs=[pl.BlockSpec((tq, B_WIDTH), lambda b, i: (q_blk0 + b * nq + i, 0)),
                  pl.BlockSpec((tq, B_WIDTH), lambda b, i: (q_blk0 + b * nq + i, 0)),
                  pl.BlockSpec((tq, LANES), lambda b, i: (q_blk0 + b * nq + i, 0)),
                  pl.BlockSpec((lp, B_WIDTH), lambda b, i: (k_blk0 + b, 0)),
                  pl.BlockSpec((lp, B_WIDTH), lambda b, i: (k_blk0 + b, 0)),
                  pl.BlockSpec((lp, LANES), lambda b, i: (k_blk0 + b, 0))],
        out_specs=pl.BlockSpec((tq, B_WIDTH), lambda b, i: (b * nq + i, 0)),
        out_shape=jax.ShapeDtypeStruct((nb * nq * tq, B_WIDTH), BF16),
        scratch_shapes=[pltpu.VMEM((nkb, tq, tk), F32),
                        pltpu.VMEM((4 * tq, LANES), F32), pltpu.VMEM((4 * tq, LANES), F32),
                        pltpu.VMEM((4 * tq, LANES), F32),
                        pltpu.VMEM((tq, 1), F32), pltpu.VMEM((tq, 1), F32), pltpu.VMEM((tq, 1), F32),
                        pltpu.VMEM((IDX_HEADS, tq, LANES), F32),
                        pltpu.VMEM((nkb, 2 * tq, tk), F32)],
        compiler_params=_cparams(("arbitrary", "arbitrary")),
        name="dsa_attention",
    )(q, qi, w, k, v, ki)


def _kb_kernel(x_ref, oa_ref, ob_ref, oc_ref, sha_ref, sca_ref, gta_ref, shm_ref, scm_ref,
               nmix_ref, nffn_ref, wg_ref, wba_ref, wbb_ref, wbc_ref, wout_ref, rw_ref, rb_ref,
               x1_o, h2_o, e_o, g_o, cnt_o, cnt_sc, *, tm, ch):
    x = x_ref[...]
    hb = _modulate(_rms(x, nmix_ref[...]), sca_ref[...], sha_ref[...], tm, ch).astype(BF16)

    def gate(c0):
        return _sigmoid(_bdot(hb, wg_ref[:, c0:c0 + D_MODEL]))

    merged = gate(0) * _bdot(oa_ref[...], wba_ref[...])
    merged = merged + gate(D_MODEL) * _bdot(ob_ref[...], wbb_ref[...])
    merged = merged + gate(2 * D_MODEL) * _bdot(oc_ref[...], wbc_ref[...])
    y = _bdot(merged.astype(BF16), wout_ref[...])
    x1 = x + _scale_rows(y, gta_ref[...], tm, ch)
    x1_o[...] = x1
    h2 = _modulate(_rms(x1, nffn_ref[...]), scm_ref[...], shm_ref[...], tm, ch)
    h2_o[...] = h2
    lg = _dot_split(h2, rw_ref[...]) + rb_ref[...]
    lanef = lax.broadcasted_iota(jnp.int32, (tm, LANES), 1).astype(F32)
    e_acc = jnp.zeros((tm, LANES), F32)
    v_acc = jnp.full((tm, LANES), NEG_INF, F32)
    chosen = jnp.zeros((tm, LANES), F32)
    picks = []
    for k in range(TOP_K):
        mx = jnp.max(lg, axis=1, keepdims=True)
        idx = jnp.min(jnp.where(lg == mx, lanef, float(LANES)), axis=1, keepdims=True)
        hit = lanef == idx
        picks.append(hit)
        chosen = jnp.where(hit, 1.0, chosen)
        e_acc = jnp.where(lanef == float(k), idx, e_acc)
        v_acc = jnp.where(lanef == float(k), mx, v_acc)
        lg = jnp.where(hit, BELOW_NEG, lg)
    ex = jnp.where(lanef < float(TOP_K), jnp.exp(v_acc - jnp.max(v_acc, axis=1, keepdims=True)), 0.0)
    g_o[...] = ex / jnp.sum(ex, axis=1, keepdims=True)
    @pl.when(pl.program_id(0) == 0)
    def _():
        cnt_sc[...] = jnp.zeros(cnt_sc.shape, F32)

    earlier = (lax.broadcasted_iota(jnp.int32, (tm, tm), 1) < lax.broadcasted_iota(jnp.int32, (tm, tm), 0))
    before = _bdot(jnp.where(earlier, 1.0, 0.0).astype(BF16), chosen.astype(BF16)) + cnt_sc[0:1, :]
    for k in range(TOP_K):
        rank = jnp.sum(jnp.where(picks[k], before, 0.0), axis=1, keepdims=True)
        e_acc = jnp.where(lanef == float(TOP_K + k), rank, e_acc)
    e_o[...] = e_acc.astype(jnp.int32)
    cnt_sc[...] = cnt_sc[...] + jnp.sum(chosen, axis=0, keepdims=True)
    cnt_o[...] = cnt_sc[...]


def _kb_call(x, oa, ob, oc, mods, nmix, nffn, wg, wba, wbb, wbc, wout, rw, rb):
    n = x.shape[0]
    tm = TOKEN_BLOCK
    ch = tm // MOD_ROWS

    def row(i):
        return (i, 0)

    def full(a):
        return pl.BlockSpec(a.shape, lambda i: (0, 0))

    in_specs = ([pl.BlockSpec((tm, D_MODEL), row), pl.BlockSpec((tm, A_WIDTH), row),
                 pl.BlockSpec((tm, B_WIDTH), row), pl.BlockSpec((tm, 512), row)]
                + [pl.BlockSpec((MOD_ROWS, D_MODEL), row) for _ in range(5)]
                + [full(a) for a in (nmix, nffn, wg, wba, wbb, wbc, wout, rw, rb)])
    out_shape = [jax.ShapeDtypeStruct((n, D_MODEL), F32), jax.ShapeDtypeStruct((n, D_MODEL), F32),
                 jax.ShapeDtypeStruct((n, LANES), jnp.int32), jax.ShapeDtypeStruct((n, LANES), F32),
                 jax.ShapeDtypeStruct((8, LANES), F32)]
    out_specs = [pl.BlockSpec((tm, D_MODEL), row), pl.BlockSpec((tm, D_MODEL), row),
                 pl.BlockSpec((tm, LANES), row), pl.BlockSpec((tm, LANES), row),
                 pl.BlockSpec((8, LANES), lambda i: (0, 0))]
    return pl.pallas_call(
        functools.partial(_kb_kernel, tm=tm, ch=ch),
        grid=(n // tm,),
        in_specs=in_specs,
        out_specs=out_specs,
        out_shape=out_shape,
        scratch_shapes=[pltpu.VMEM((8, LANES), F32)],
        compiler_params=_cparams(("arbitrary",)),
        name="post_attention",
    )(x, oa, ob, oc, *mods, nmix, nffn, wg, wba, wbb, wbc, wout, rw, rb)


def _ke_kernel(be_ref, nu_ref, x_ref, wg_ref, wl_ref, wd_ref, bg_ref, bl_ref, bd_ref, y_ref):
    i = pl.program_id(0)

    @pl.when(i < nu_ref[0])
    def _():
        x = x_ref[...].astype(BF16)
        g = jnp.minimum(_bdot(x, wg_ref[0]) + bg_ref[0], SWIGLU_LIMIT)
        l = jnp.clip(_bdot(x, wl_ref[0]) + bl_ref[0], -SWIGLU_LIMIT, SWIGLU_LIMIT)
        act = g * _sigmoid(SWIGLU_ALPHA * g) * (l + 1.0)
        y_ref[...] = _bdot(act.astype(BF16), wd_ref[0]) + bd_ref[0]

    @pl.when(i >= nu_ref[0])
    def _():
        y_ref[...] = jnp.zeros(y_ref.shape, F32)


def _ke_call(blk_e, n_used, xg, wg, wl, wd, bg, bl, bd):
    n_rows = xg.shape[0]
    eb = EXPERT_ROWS
    n_blocks = n_rows // eb
    grid_spec = pltpu.PrefetchScalarGridSpec(
        num_scalar_prefetch=2,
        grid=(n_blocks,),
        in_specs=[pl.BlockSpec((eb, D_MODEL), lambda i, be, nu: (i, 0)),
                  pl.BlockSpec((1, D_MODEL, D_FF), lambda i, be, nu: (be[i], 0, 0)),
                  pl.BlockSpec((1, D_MODEL, D_FF), lambda i, be, nu: (be[i], 0, 0)),
                  pl.BlockSpec((1, D_FF, D_MODEL), lambda i, be, nu: (be[i], 0, 0)),
                  pl.BlockSpec((1, 1, D_FF), lambda i, be, nu: (be[i], 0, 0)),
                  pl.BlockSpec((1, 1, D_FF), lambda i, be, nu: (be[i], 0, 0)),
                  pl.BlockSpec((1, 1, D_MODEL), lambda i, be, nu: (be[i], 0, 0))],
        out_specs=pl.BlockSpec((eb, D_MODEL), lambda i, be, nu: (i, 0)),
    )
    return pl.pallas_call(
        _ke_kernel,
        grid_spec=grid_spec,
        out_shape=jax.ShapeDtypeStruct((n_rows, D_MODEL), F32),
        compiler_params=_cparams(("arbitrary",)),
        name="moe_experts",
    )(blk_e, n_used, xg, wg, wl, wd, bg, bl, bd)


def _kc_kernel(dcur_ref, dnext_ref, x1_ref, yr_hbm, g_ref, gtm_ref, fn_ref, x2_o, y_o, buf, sem,
               *, tm, ch):
    i = pl.program_id(0)
    nblk = pl.num_programs(0)
    slot = lax.rem(i, 2)
    nrow = TOP_K * tm

    def row_copy(d_ref, s, r):
        return pltpu.make_async_copy(yr_hbm.at[pl.ds(d_ref[0, 0, r], 1), :],
                                     buf.at[s, pl.ds(r, 1), :], sem.at[s])

    def issue(d_ref, s):
        def body(r, c):
            row_copy(d_ref, s, r).start()
            return c
        lax.fori_loop(0, nrow, body, 0, unroll=8)

    @pl.when(i == 0)
    def _():
        issue(dcur_ref, 0)

    @pl.when(i + 1 < nblk)
    def _():
        issue(dnext_ref, 1 - slot)

    pltpu.make_async_copy(buf.at[slot], buf.at[slot], sem.at[slot]).wait()
    gate = g_ref[...]
    ffn = gate[:, 0:1] * buf[slot, 0:tm, :]
    for k in range(1, TOP_K):
        ffn = ffn + gate[:, k:k + 1] * buf[slot, k * tm:(k + 1) * tm, :]
    x2 = x1_ref[...] + _scale_rows(ffn, gtm_ref[...], tm, ch)
    x2_o[...] = x2
    y_o[...] = _rms(x2, fn_ref[...])


def _kc_call(x1, yr, dest_blocks, gate, gtm, fnorm):
    n = x1.shape[0]
    tm = TOKEN_BLOCK
    ch = tm // MOD_ROWS
    nblk = n // tm
    smem = functools.partial(pl.BlockSpec, (1, 1, TOP_K * tm), memory_space=pltpu.SMEM)
    return pl.pallas_call(
        functools.partial(_kc_kernel, tm=tm, ch=ch),
        grid=(nblk,),
        in_specs=[smem(lambda i: (i, 0, 0)),
                  smem(lambda i: (jnp.minimum(i + 1, nblk - 1), 0, 0)),
                  pl.BlockSpec((tm, D_MODEL), lambda i: (i, 0)),
                  pl.BlockSpec(memory_space=pl.ANY),
                  pl.BlockSpec((tm, LANES), lambda i: (i, 0)),
                  pl.BlockSpec((MOD_ROWS, D_MODEL), lambda i: (i, 0)),
                  pl.BlockSpec((1, D_MODEL), lambda i: (0, 0))],
        out_specs=[pl.BlockSpec((tm, D_MODEL), lambda i: (i, 0)), pl.BlockSpec((tm, D_MODEL), lambda i: (i, 0))],
        out_shape=[jax.ShapeDtypeStruct((n, D_MODEL), F32), jax.ShapeDtypeStruct((n, D_MODEL), F32)],
        scratch_shapes=[pltpu.VMEM((2, TOP_K * tm, D_MODEL), F32), pltpu.SemaphoreType.DMA((2,))],
        compiler_params=_cparams(("arbitrary",)),
        name="moe_combine",
    )(dest_blocks, dest_blocks, x1, yr, gate, gtm, fnorm)


def _rope_tables(pos):
    lane = np.arange(LANES)
    inv32 = ROPE_THETA ** (-jnp.arange(32, dtype=F32) / 32)
    inv16 = ROPE_THETA ** (-jnp.arange(16, dtype=F32) / 16)
    ang64 = pos[:, None] * inv32[None, :][:, lane & 31]
    ang32 = pos[:, None] * inv16[None, :][:, lane & 15]
    sign64 = jnp.asarray(np.where((lane & 63) < 32, -1.0, 1.0), F32)[None, :]
    sign32 = jnp.asarray(np.where((lane & 31) < 16, -1.0, 1.0), F32)[None, :]
    in_m = jnp.asarray(lane < C_ROPE)[None, :]
    in_q = jnp.asarray((lane >= C_NOPE) & (lane < C_NOPE + C_ROPE))[None, :]
    cos64, sin64 = jnp.cos(ang64), jnp.sin(ang64) * sign64
    cos32, sin32 = jnp.cos(ang32), jnp.sin(ang32) * sign32
    return (cos64, sin64,
            jnp.where(in_m, cos32, 1.0), jnp.where(in_m, sin32, 0.0),
            jnp.where(in_q, cos32, 1.0), jnp.where(in_q, sin32, 0.0))


def _layer_weights(l, w_in, mla_w_uq, mla_w_ukv, w_br_c, router_w, router_b, exp_w_gu, exp_b_gu):
    wi = w_in[l]
    z = lambda n: jnp.zeros((D_MODEL, n), F32)
    w1 = jnp.concatenate([
        wi[:, _OFF_AQ:_OFF_BIK],
        wi[:, _OFF_BIK:_OFF_BIW], wi[:, _OFF_BIK:_OFF_BIW],
        wi[:, _OFF_CKR:_OFF_GATES], wi[:, _OFF_BIW:_OFF_CQ], z(LANES - C_ROPE - IDX_HEADS),
        wi[:, _OFF_CQ:_OFF_CKV], wi[:, _OFF_CKV:_OFF_CKR]], axis=1).astype(BF16)
    wg = wi[:, _OFF_GATES:].astype(BF16)
    wuq = mla_w_uq[l].reshape(Q_LORA, C_HEADS, C_NOPE + C_ROPE)
    wuq = jnp.pad(wuq, ((0, 0), (0, 0), (0, LANES - C_NOPE - C_ROPE))).reshape(Q_LORA, C_HEADS * LANES).astype(BF16)
    wukv = mla_w_ukv[l].reshape(KV_LORA, C_HEADS, C_NOPE + C_V)
    wk = jnp.pad(wukv[:, :, :C_NOPE], ((0, 0), (0, 0), (0, LANES - C_NOPE))).reshape(KV_LORA, C_HEADS * LANES).astype(BF16)
    wv = jnp.pad(wukv[:, :, C_NOPE:], ((0, 0), (0, 0), (0, LANES - C_V))).reshape(KV_LORA, C_HEADS * LANES).astype(BF16)
    e = np.zeros((LANES, C_HEADS * LANES), np.float32)
    for h in range(C_HEADS):
        e[np.arange(C_ROPE), h * LANES + C_NOPE + np.arange(C_ROPE)] = 1.0
    we = jnp.asarray(e, BF16)
    wbc = jnp.pad(w_br_c[l].reshape(C_HEADS, C_V, D_MODEL), ((0, 0), (0, LANES - C_V), (0, 0)))
    wbc = wbc.reshape(C_HEADS * LANES, D_MODEL).astype(BF16)
    rw = jnp.pad(router_w[l], ((0, 0), (0, LANES - N_EXPERTS)))
    rb = jnp.pad(router_b[l], (0, LANES - N_EXPERTS), constant_values=NEG_INF).reshape(1, LANES)
    wgl = exp_w_gu[l].reshape(N_EXPERTS, D_MODEL, D_FF, 2)
    bgl = exp_b_gu[l].reshape(N_EXPERTS, 1, D_FF, 2)
    return dict(w1=w1, wg=wg, wuq=wuq, wk=wk, wv=wv, we=we, wbc=wbc, rw=rw, rb=rb,
                e_wg=wgl[..., 0].astype(BF16), e_wl=wgl[..., 1].astype(BF16),
                e_bg=bgl[..., 0], e_bl=bgl[..., 1])


def _route(e_pad, cnt, n_rows_pad):
    eb = EXPERT_ROWS
    flat_e = e_pad[:, :TOP_K].reshape(-1)
    rank = e_pad[:, TOP_K:2 * TOP_K].reshape(-1)
    nk = flat_e.shape[0]
    counts = cnt[0, :N_EXPERTS].astype(jnp.int32)
    padded = (counts + eb - 1) // eb * eb
    pad_end = jnp.cumsum(padded)
    pad_start = pad_end - padded
    dest = pad_start[flat_e] + rank
    row_tok = jnp.zeros((n_rows_pad,), jnp.int32).at[dest].set(jnp.arange(nk, dtype=jnp.int32) // TOP_K)
    n_blocks = n_rows_pad // eb
    blk_start = jnp.arange(n_blocks, dtype=jnp.int32) * eb
    blk_e = jnp.minimum(jnp.searchsorted(pad_end, blk_start, side='right'), N_EXPERTS - 1).astype(jnp.int32)
    n_used = (pad_end[-1] // eb).astype(jnp.int32).reshape(1)
    return dest, row_tok, blk_e, n_used


def kernel(x_prompt, x_sample, cache_a_k, cache_a_v, cache_b_k, cache_b_v, cache_b_idx_k, cache_c_latent, cache_c_k_rope, c_prompt, c_sample, w_ada, b_ada, norm_mix, norm_ffn, w_in, diff_lq1, diff_lk1, diff_lq2, diff_lk2, diff_subln, mla_q_norm, mla_w_uq, mla_kv_norm, mla_w_ukv, w_br_a, w_br_b, w_br_c, w_out, router_w, router_b, exp_w_gu, exp_b_gu, exp_w_down, exp_b_down, final_norm):
    depth = w_ada.shape[0]
    bp, tp, _ = x_prompt.shape
    bs, ts, _ = x_sample.shape
    past = cache_c_latent.shape[2]
    n_p, n_s = bp * tp, bs * ts
    n = n_p + n_s
    tm = TOKEN_BLOCK
    ch = tm // MOD_ROWS
    assert ts == CHUNK and tp % tm == 0 and n_s % tm == 0 and past % CHUNK == 0

    x = jnp.concatenate([x_prompt.reshape(n_p, D_MODEL), x_sample.reshape(n_s, D_MODEL)], axis=0)

    n_seq = bp + bs
    c_all = jnp.concatenate([c_prompt, c_sample], axis=0)
    c_pad = jnp.pad(c_all, ((0, (-n_seq) % 8), (0, 0)))
    mod = _ada_call(c_pad, w_ada, b_ada)
    def per_chunk(m, reps):
        return jnp.broadcast_to(m[:, :, None, :], m.shape[:2] + (reps, m.shape[2])).reshape(depth, -1, m.shape[2])

    mod_rows = jnp.concatenate([per_chunk(mod[:, :bp], tp // ch), per_chunk(mod[:, bp:n_seq], ts // ch)],
                               axis=1)

    pos = jnp.concatenate([jnp.arange(tp, dtype=F32),
                           jnp.tile(past + jnp.arange(ts, dtype=F32), tm // ts)])
    tabs = _rope_tables(pos)

    tq_p = min(256, tp)
    tk_p = min(512, tp)
    nq_p = tp // tq_p
    tq_c = min(512, tp)
    l_s = past + ts
    tk_s = 384
    lp_s = -(-l_s // tk_s) * tk_s
    n_rows_pad = -(-(n * TOP_K + N_EXPERTS * (EXPERT_ROWS - 1)) // EXPERT_ROWS) * EXPERT_ROWS

    def with_cache(cache_l, new, width):
        parts = [cache_l.reshape(bs, past, width).astype(BF16), new.reshape(bs, ts, width)]
        if lp_s > l_s:
            parts.append(jnp.zeros((bs, lp_s - l_s, width), BF16))
        return jnp.concatenate(parts, axis=1).reshape(bs * lp_s, width)

    caches = [[] for _ in range(7)]
    y = None
    for l in range(depth):
        lam_init = 0.8 - 0.6 * math.exp(-0.3 * l)
        wl = _layer_weights(l, w_in, mla_w_uq, mla_w_ukv, w_br_c, router_w, router_b, exp_w_gu, exp_b_gu)
        m6 = [mod_rows[l, :, j * D_MODEL:(j + 1) * D_MODEL] for j in range(6)]
        sh_a, sc_a, gt_a, sh_m, sc_m, gt_m = m6
        nmix = norm_mix[l].reshape(1, D_MODEL)
        nffn = norm_ffn[l].reshape(1, D_MODEL)

        (ak, av, bk, bv, bik, clat, ckr, misc,
         aq_b, ak_b, av_b, bq_b, bk_b, bv_b, biq_b, bik2_b, cq_b, ck_b, cv_b) = _ka_call(
            x, sh_a, sc_a, tabs, nmix, wl['w1'], mla_q_norm[l].reshape(1, Q_LORA), wl['wuq'],
            mla_kv_norm[l].reshape(1, KV_LORA), wl['wk'], wl['we'], wl['wv'],
            n_prompt_blocks=n_p // tm, tab_blocks=tp // tm)
        for i, a in enumerate((ak, av, bk, bv, bik, clat, ckr)):
            caches[i].append(a)

        lqk = jnp.pad(jnp.stack([diff_lq1[l], diff_lk1[l], diff_lq2[l], diff_lk2[l]]),
                      ((0, 4), (0, LANES - A_HD)))
        sub = diff_subln[l].reshape(1, 2 * A_HD)
        oa_p = _diff_call(lqk, sub, aq_b, ak_b, av_b, nb=bp, nq=nq_p, tq=tq_p, tk=tk_p, lp=tp,
                          q_blk0=0, k_blk0=0, q_pos0=0, lam_init=lam_init)
        ak_s = with_cache(cache_a_k[l], ak_b[n_p:], A_WIDTH)
        av_s = with_cache(cache_a_v[l], av_b[n_p:], A_WIDTH)
        oa_s = _diff_call(lqk, sub, aq_b, ak_s, av_s, nb=bs, nq=1, tq=ts, tk=tk_s, lp=lp_s,
                          q_blk0=n_p // ts, k_blk0=0, q_pos0=past, lam_init=lam_init)
        ob_p = _dsa_call(bq_b, biq_b, misc, bk_b, bv_b, bik2_b, nb=bp, nq=nq_p, tq=tq_p, tk=tk_p, lp=tp,
                         q_blk0=0, k_blk0=0, q_pos0=0, n_keys=tp, n_bisect=14)
        bk_s = with_cache(cache_b_k[l], bk_b[n_p:], B_WIDTH)
        bv_s = with_cache(cache_b_v[l], bv_b[n_p:], B_WIDTH)
        cik = cache_b_idx_k[l].reshape(bs, past, IDX_DIM)
        bik_s = with_cache(jnp.concatenate([cik, cik], axis=-1), bik2_b[n_p:], LANES)
        ob_s = _dsa_call(bq_b, biq_b, misc, bk_s, bv_s, bik_s, nb=bs, nq=1, tq=ts, tk=tk_s, lp=lp_s,
                         q_blk0=n_p // ts, k_blk0=0, q_pos0=past, n_keys=l_s, n_bisect=12)
        oc_p = _mla_call(cq_b, ck_b, cv_b, nb=bp, nq=tp // tq_c, tq=tq_c, tk=tk_p, lp=tp,
                         q_blk0=0, k_blk0=0, q_pos0=0)
        lat_c = cache_c_latent[l].reshape(bs * past, KV_LORA)
        kr_c = jnp.pad(cache_c_k_rope[l].reshape(bs * past, C_ROPE), ((0, 0), (0, LANES - C_ROPE)))
        ck_c, cv_c = _mla_kv_call(lat_c, kr_c, wl['wk'], wl['we'], wl['wv'])
        ck_s = with_cache(ck_c, ck_b[n_p:], 512)
        cv_s = with_cache(cv_c, cv_b[n_p:], 512)
        oc_s = _mla_call(cq_b, ck_s, cv_s, nb=bs, nq=1, tq=ts, tk=tk_s, lp=lp_s,
                         q_blk0=n_p // ts, k_blk0=0, q_pos0=past)

        oa = jnp.concatenate([oa_p, oa_s], axis=0)
        ob = jnp.concatenate([ob_p, ob_s], axis=0)
        oc = jnp.concatenate([oc_p, oc_s], axis=0)

        x1, h2, e_pad, g_pad, cnt = _kb_call(
            x, oa, ob, oc, (sh_a, sc_a, gt_a, sh_m, sc_m), nmix, nffn, wl['wg'],
            w_br_a[l].astype(BF16), w_br_b[l].astype(BF16), wl['wbc'], w_out[l].astype(BF16),
            wl['rw'], wl['rb'])

        dest, row_tok, blk_e, n_used = _route(e_pad, cnt, n_rows_pad)
        xg = jnp.take(h2, row_tok, axis=0)
        yr = _ke_call(blk_e, n_used, xg, wl['e_wg'], wl['e_wl'], exp_w_down[l].astype(BF16),
                      wl['e_bg'], wl['e_bl'], exp_b_down[l].reshape(N_EXPERTS, 1, D_MODEL))
        dest_blocks = dest.reshape(n // tm, tm, TOP_K).transpose(0, 2, 1).reshape(n // tm, 1, TOP_K * tm)
        x, y = _kc_call(x1, yr, dest_blocks, g_pad, gt_m, final_norm.reshape(1, D_MODEL))

    def split(a, tail):
        a = jnp.stack(a, axis=0)
        return (a[:, :n_p].reshape((depth, bp, tp) + tail), a[:, n_p:].reshape((depth, bs, ts) + tail))

    tails = ((A_HEADS, 2 * A_HD), (A_HEADS, 2 * A_HD), (B_HEADS, B_HD), (B_HEADS, B_HD),
             (IDX_DIM,), (KV_LORA,), (C_ROPE,))
    ps = [split(c, t) for c, t in zip(caches, tails)]
    y_prompt = y[:n_p].reshape(bp, tp, D_MODEL)
    y_sample = y[n_p:].reshape(bs, ts, D_MODEL)
    return (y_prompt, y_sample) + tuple(p[0] for p in ps) + tuple(p[1] for p in ps)
```

```python
import functools
import math

import numpy as np
import jax
import jax.numpy as jnp
from jax import lax
from jax.experimental import pallas as pl
from jax.experimental.pallas import tpu as pltpu

F32 = jnp.float32
BF16 = jnp.bfloat16

D_MODEL = 1024
CHUNK = 64
ROPE_THETA = 10000.0
NORM_EPS = 1e-6
NEG_INF = -1e30
HALF_NEG = -5e29
BELOW_NEG = -3e38
BIG_POS = 3e38
LOG2E = 1.4426950408889634

A_HEADS, A_HD = 4, 64
B_HEADS, B_HD = 4, 64
IDX_HEADS, IDX_DIM = 4, 64
DSA_TOPK = 256
C_HEADS, C_NOPE, C_ROPE, C_V = 4, 64, 32, 64
Q_LORA, KV_LORA = 256, 128
N_EXPERTS, TOP_K = 32, 4
D_FF = D_MODEL
SWIGLU_LIMIT = 7.0
SWIGLU_ALPHA = 1.702

A_WIDTH = A_HEADS * 2 * A_HD
B_WIDTH = B_HEADS * B_HD
LANES = 128

_OFF_AQ, _OFF_AK, _OFF_AV = 0, 512, 1024
_OFF_BQ, _OFF_BK, _OFF_BV = 1536, 1792, 2048
_OFF_BIQ, _OFF_BIK, _OFF_BIW = 2304, 2560, 2624
_OFF_CQ, _OFF_CKV, _OFF_CKR = 2628, 2884, 3012
_OFF_GATES = 3044
_P_AQ, _P_AK, _P_AV = 0, 512, 1024
_P_BQ, _P_BK, _P_BV = 1536, 1792, 2048
_P_BIQ, _P_BIK2, _P_MISC, _P_CQ, _P_CKV, _P_END = 2304, 2560, 2688, 2816, 3072, 3200
_MISC_BIW = 32

TOKEN_BLOCK = 256
MOD_ROWS = 8
EXPERT_ROWS = 256
VMEM_LIMIT = 56 * 1024 * 1024


def _cparams(sem, vmem=VMEM_LIMIT):
    return pltpu.CompilerParams(dimension_semantics=sem, vmem_limit_bytes=vmem)


def _rms(xf, g):
    return xf * lax.rsqrt(jnp.mean(xf * xf, axis=-1, keepdims=True) + NORM_EPS) * g


def _sigmoid(x):
    return 1.0 / (1.0 + jnp.exp(-x))


def _bdot(a, b):
    return jnp.dot(a, b, preferred_element_type=F32)


def _dot_nt(a, b):
    return lax.dot_general(a, b, (((1,), (1,)), ((), ())), preferred_element_type=F32)


def _dot_split(a, b):
    a_hi = a.astype(BF16)
    b_hi = b.astype(BF16)
    a_lo = (a - a_hi.astype(F32)).astype(BF16)
    b_lo = (b - b_hi.astype(F32)).astype(BF16)
    return _bdot(a_hi, b_hi) + (_bdot(a_hi, b_lo) + _bdot(a_lo, b_hi))


def _modulate(xn, sc, sh, rows, ch):
    n = rows // ch
    y = xn.reshape(n, ch, D_MODEL) * (1.0 + sc)[:, None, :] + sh[:, None, :]
    return y.reshape(rows, D_MODEL)


def _scale_rows(y, g, rows, ch):
    n = rows // ch
    return (y.reshape(n, ch, D_MODEL) * g[:, None, :]).reshape(rows, D_MODEL)


def _ada_kernel(c_ref, w_ref, b_ref, o_ref):
    c = c_ref[...]
    s = c * _sigmoid(c)
    o_ref[0] = _dot_split(s, w_ref[0]) + b_ref[0]


def _ada_call(c_pad, w_ada, b_ada):
    depth = w_ada.shape[0]
    mp = c_pad.shape[0]
    return pl.pallas_call(
        _ada_kernel,
        grid=(depth, 6),
        in_specs=[
            pl.BlockSpec((mp, D_MODEL), lambda l, j: (0, 0)),
            pl.BlockSpec((1, D_MODEL, D_MODEL), lambda l, j: (l, 0, j)),
            pl.BlockSpec((1, 1, D_MODEL), lambda l, j: (l, 0, j)),
        ],
        out_specs=pl.BlockSpec((1, mp, D_MODEL), lambda l, j: (l, 0, j)),
        out_shape=jax.ShapeDtypeStruct((depth, mp, 6 * D_MODEL), F32),
        compiler_params=_cparams(("arbitrary", "arbitrary")),
        name="ada_mod",
    )(c_pad, w_ada, b_ada.reshape(depth, 1, 6 * D_MODEL))


def _rope_partner(x, lane, half):
    first = (lane & (2 * half - 1)) < half
    return jnp.where(first, pltpu.roll(x, LANES - half, 1), pltpu.roll(x, half, 1))


def _ka_kernel(x_ref, sh_ref, sc_ref, cos_ref, sin_ref, cosm_ref, sinm_ref, cosq_ref, sinq_ref,
               nmix_ref, w1_ref, qn_ref, wuq_ref, kvn_ref, wk_ref, we_ref, wv_ref,
               ak_o, av_o, bk_o, bv_o, bik_o, clat_o, ckr_o, misc_o,
               aq_b, ak_b, av_b, bq_b, bk_b, bv_b, biq_b, bik2_b, cq_b, ck_b, cv_b, *, tm, ch):
    x = x_ref[...]
    h = _modulate(_rms(x, nmix_ref[...]), sc_ref[...], sh_ref[...], tm, ch)
    hb = h.astype(BF16)
    lane = lax.broadcasted_iota(jnp.int32, (tm, LANES), 1)
    cos = cos_ref[...]
    sin = sin_ref[...]

    def proj(c0, c1):
        return _bdot(hb, w1_ref[:, c0:c1])

    def rope64(xb):
        return xb * cos + _rope_partner(xb, lane, 32) * sin

    a_scale = (A_HD ** -0.5) * LOG2E
    b_scale = (B_HD ** -0.5) * LOG2E
    i_scale = IDX_DIM ** -0.5
    c_scale = ((C_NOPE + C_ROPE) ** -0.5) * LOG2E

    p = proj(_P_AQ, _P_AK)
    for c in range(A_WIDTH // LANES):
        sl = slice(c * LANES, (c + 1) * LANES)
        aq_b[:, sl] = (rope64(p[:, sl]) * a_scale).astype(BF16)
    p = proj(_P_AK, _P_AV)
    for c in range(A_WIDTH // LANES):
        sl = slice(c * LANES, (c + 1) * LANES)
        r = rope64(p[:, sl])
        ak_o[:, sl] = r
        ak_b[:, sl] = r.astype(BF16)
    p = proj(_P_AV, _P_BQ)
    av_o[...] = p
    av_b[...] = p.astype(BF16)
    p = proj(_P_BQ, _P_BK)
    for c in range(B_WIDTH // LANES):
        sl = slice(c * LANES, (c + 1) * LANES)
        bq_b[:, sl] = (rope64(p[:, sl]) * b_scale).astype(BF16)
    p = proj(_P_BK, _P_BV)
    for c in range(B_WIDTH // LANES):
        sl = slice(c * LANES, (c + 1) * LANES)
        r = rope64(p[:, sl])
        bk_o[:, sl] = r
        bk_b[:, sl] = r.astype(BF16)
    p = proj(_P_BV, _P_BIQ)
    bv_o[...] = p
    bv_b[...] = p.astype(BF16)
    p = proj(_P_BIQ, _P_BIK2)
    for c in range(B_WIDTH // LANES):
        sl = slice(c * LANES, (c + 1) * LANES)
        biq_b[:, sl] = (rope64(p[:, sl]) * i_scale).astype(BF16)
    r = rope64(proj(_P_BIK2, _P_MISC))
    bik_o[...] = r[:, :IDX_DIM]
    bik2_b[...] = r.astype(BF16)
    pm = proj(_P_MISC, _P_CQ)
    misc = pm * cosm_ref[...] + _rope_partner(pm, lane, 16) * sinm_ref[...]
    misc_o[...] = misc
    ckr_o[...] = misc[:, :C_ROPE]
    qlat = _rms(proj(_P_CQ, _P_CKV), qn_ref[...]).astype(BF16)
    cqf = _bdot(qlat, wuq_ref[...])
    cosq = cosq_ref[...]
    sinq = sinq_ref[...]
    for c in range(C_HEADS):
        sl = slice(c * LANES, (c + 1) * LANES)
        xb = cqf[:, sl]
        cq_b[:, sl] = ((xb * cosq + _rope_partner(xb, lane, 16) * sinq) * c_scale).astype(BF16)
    clat = _rms(proj(_P_CKV, _P_END), kvn_ref[...])
    clat_o[...] = clat
    clb = clat.astype(BF16)
    ck_b[...] = (_bdot(clb, wk_ref[...]) + _bdot(misc.astype(BF16), we_ref[...])).astype(BF16)
    cv_b[...] = _bdot(clb, wv_ref[...]).astype(BF16)


def _ka_call(x, sh, sc, tabs, nmix, w1, qn, wuq, kvn, wk, we, wv, *, n_prompt_blocks, tab_blocks):
    n = x.shape[0]
    tm = TOKEN_BLOCK
    ch = tm // MOD_ROWS
    nblk = n // tm

    def row(i):
        return (i, 0)

    def tab(i):
        return (jnp.where(i < n_prompt_blocks, i % tab_blocks, tab_blocks), 0)

    def const(i):
        return (0, 0)

    def full(a):
        return pl.BlockSpec(a.shape, const)

    widths_f32 = (A_WIDTH, A_WIDTH, B_WIDTH, B_WIDTH, IDX_DIM, KV_LORA, C_ROPE, LANES)
    widths_b16 = (A_WIDTH, A_WIDTH, A_WIDTH, B_WIDTH, B_WIDTH, B_WIDTH, B_WIDTH, LANES, 512, 512, 512)
    out_shape = ([jax.ShapeDtypeStruct((n, w), F32) for w in widths_f32]
                 + [jax.ShapeDtypeStruct((n, w), BF16) for w in widths_b16])
    out_specs = [pl.BlockSpec((tm, w), row) for w in widths_f32 + widths_b16]
    in_specs = ([pl.BlockSpec((tm, D_MODEL), row),
                 pl.BlockSpec((MOD_ROWS, D_MODEL), row),
                 pl.BlockSpec((MOD_ROWS, D_MODEL), row)]
                + [pl.BlockSpec((tm, LANES), tab) for _ in range(6)]
                + [full(a) for a in (nmix, w1, qn, wuq, kvn, wk, we, wv)])
    return pl.pallas_call(
        functools.partial(_ka_kernel, tm=tm, ch=ch),
        grid=(nblk,),
        in_specs=in_specs,
        out_specs=out_specs,
        out_shape=out_shape,
        compiler_params=_cparams(("arbitrary",)),
        name="pre_attention",
    )(x, sh, sc, *tabs, nmix, w1, qn, wuq, kvn, wk, we, wv)


def _mla_kv_kernel(lat_ref, kr_ref, wk_ref, we_ref, wv_ref, ck_o, cv_o):
    lb = lat_ref[...].astype(BF16)
    ck_o[...] = (_bdot(lb, wk_ref[...]) + _bdot(kr_ref[...].astype(BF16), we_ref[...])).astype(BF16)
    cv_o[...] = _bdot(lb, wv_ref[...]).astype(BF16)


def _mla_kv_call(lat, krp, wk, we, wv):
    n = lat.shape[0]
    tm = 512
    return pl.pallas_call(
        _mla_kv_kernel,
        grid=(n // tm,),
        in_specs=[pl.BlockSpec((tm, KV_LORA), lambda i: (i, 0)),
                  pl.BlockSpec((tm, LANES), lambda i: (i, 0)),
                  pl.BlockSpec(wk.shape, lambda i: (0, 0)),
                  pl.BlockSpec(we.shape, lambda i: (0, 0)),
                  pl.BlockSpec(wv.shape, lambda i: (0, 0))],
        out_specs=[pl.BlockSpec((tm, 512), lambda i: (i, 0)), pl.BlockSpec((tm, 512), lambda i: (i, 0))],
        out_shape=[jax.ShapeDtypeStruct((n, 512), BF16), jax.ShapeDtypeStruct((n, 512), BF16)],
        compiler_params=_cparams(("arbitrary",)),
        name="mla_cache_kv",
    )(lat, krp, wk, we, wv)


def _block_range(r0, tq, tk, nkb):
    n_full = jnp.minimum((r0 + CHUNK) // tk, nkb)
    n_vis = jnp.minimum((r0 + tq + tk - 1) // tk, nkb)
    return n_full, n_vis


def _visible(r0, ks, rows, tk):
    rpos = r0 + lax.broadcasted_iota(jnp.int32, (rows, tk), 0)
    kpos = ks + lax.broadcasted_iota(jnp.int32, (rows, tk), 1)
    return kpos < (((rpos >> 6) + 1) << 6)


def _fold_max(s, m_sc, rows):
    mp = m_sc[rows, :]
    for j in range(s.shape[1] // LANES):
        mp = jnp.maximum(mp, s[:, j * LANES:(j + 1) * LANES])
    m_sc[rows, :] = mp


def _finish_max(m_sc, rows):
    mp = m_sc[rows, :]
    m_sc[rows, :] = jnp.broadcast_to(jnp.max(mp, axis=1, keepdims=True), mp.shape)


def _accumulate(s, vblk, m_sc, l_sc, acc_sc, rows):
    m = m_sc[rows, :]
    lp = l_sc[rows, :]
    ps = []
    for j in range(s.shape[1] // LANES):
        pj = jnp.exp2(s[:, j * LANES:(j + 1) * LANES] - m)
        lp = lp + pj
        ps.append(pj.astype(BF16))
    l_sc[rows, :] = lp
    acc_sc[rows, :] = acc_sc[rows, :] + _bdot(jnp.concatenate(ps, axis=1), vblk)


WIDE = 4
BISECT_EXTRA = 6


def _two_pass(n_full, n_vis, score, values, m_sc, l_sc, acc_sc, sbuf, groups):
    tk = sbuf.shape[2]
    m_sc[...] = jnp.full(m_sc.shape, NEG_INF, F32)
    l_sc[...] = jnp.zeros(l_sc.shape, F32)
    acc_sc[...] = jnp.zeros(acc_sc.shape, F32)

    for g, rows in enumerate(groups):
        def keep(kb, nblk, masked):
            s = score(kb, nblk, masked, g)
            for w in range(nblk):
                sbuf[kb + w] = s[:, w * tk:(w + 1) * tk]
            _fold_max(s, m_sc, rows)

        def wide_body(j, c):
            keep(j * WIDE, WIDE, False)
            return c

        def full_body(kb, c):
            keep(kb, 1, False)
            return c

        def masked_body(kb, c):
            keep(kb, 1, True)
            return c

        n_wide = n_full // WIDE
        lax.fori_loop(0, n_wide, wide_body, 0)
        lax.fori_loop(n_wide * WIDE, n_full, full_body, 0)
        lax.fori_loop(n_full, n_vis, masked_body, 0)
        _finish_max(m_sc, rows)

        def acc_wide(j, c):
            kb = j * WIDE
            s = jnp.concatenate([sbuf[kb + w] for w in range(WIDE)], axis=1)
            _accumulate(s, values(kb, WIDE, g), m_sc, l_sc, acc_sc, rows)
            return c

        def acc_body(kb, c):
            _accumulate(sbuf[kb], values(kb, 1, g), m_sc, l_sc, acc_sc, rows)
            return c

        n_wide = n_vis // WIDE
        lax.fori_loop(0, n_wide, acc_wide, 0)
        lax.fori_loop(n_wide * WIDE, n_vis, acc_body, 0)
    return acc_sc[...] / jnp.sum(l_sc[...], axis=1, keepdims=True)


def _split_halves(q, lane=None):
    lane1 = lax.broadcasted_iota(jnp.int32, (1, LANES), 1)
    lo = jnp.where(lane1 < 64, 1.0, 0.0).astype(q.dtype)
    return q * lo, q * (1.0 - lo).astype(q.dtype)


def _diff_kernel(lqk_ref, sub_ref, q_ref, k_ref, v_ref, o_ref, m_sc, l_sc, acc_sc, sbuf,
                 *, tq, tk, nkb, q_pos0, lam_init, heads):
    i = pl.program_id(2)
    r0 = q_pos0 + i * tq
    n_full, n_vis = _block_range(r0, tq, tk, nkb)
    lane = lax.broadcasted_iota(jnp.int32, (tq, LANES), 1)
    lq = lqk_ref[...]
    lam = (jnp.exp(jnp.sum(lq[0:1] * lq[1:2], axis=1, keepdims=True))
           - jnp.exp(jnp.sum(lq[2:3] * lq[3:4], axis=1, keepdims=True)) + lam_init)

    for h in range(heads):
        hl = slice(h * LANES, (h + 1) * LANES)
        qq = jnp.concatenate(_split_halves(q_ref[:, hl], lane), axis=0)

        def score(kb, nblk, masked, g, qq=qq, hl=hl):
            ks = pl.multiple_of(kb * tk, tk)
            s = _dot_nt(qq, k_ref[pl.ds(ks, nblk * tk), hl])
            if masked:
                vis = _visible(r0, ks, tq, nblk * tk)
                s = jnp.where(jnp.concatenate([vis, vis], axis=0), s, NEG_INF)
            return s

        def values(kb, nblk, g, hl=hl):
            return v_ref[pl.ds(pl.multiple_of(kb * tk, tk), nblk * tk), hl]

        o = _two_pass(n_full, n_vis, score, values, m_sc, l_sc, acc_sc, sbuf, (slice(0, 2 * tq),))
        o = o[:tq] - lam * o[tq:]
        o = o * lax.rsqrt(jnp.mean(o * o, axis=1, keepdims=True) + NORM_EPS)
        o_ref[:, hl] = (o * sub_ref[...] * (1.0 - lam_init)).astype(BF16)


def _diff_call(lqk, sub, q, k, v, *, nb, nq, tq, tk, lp, q_blk0, k_blk0, q_pos0, lam_init, heads=1):
    nkb = lp // tk
    kern = functools.partial(_diff_kernel, tq=tq, tk=tk, nkb=nkb, q_pos0=q_pos0, lam_init=lam_init,
                             heads=heads)
    hw = heads * LANES
    return pl.pallas_call(
        kern,
        grid=(nb, A_HEADS // heads, nq),
        in_specs=[pl.BlockSpec(lqk.shape, lambda b, h, i: (0, 0)),
                  pl.BlockSpec((1, LANES), lambda b, h, i: (0, 0)),
                  pl.BlockSpec((tq, hw), lambda b, h, i: (q_blk0 + b * nq + i, h)),
                  pl.BlockSpec((lp, hw), lambda b, h, i: (k_blk0 + b, h)),
                  pl.BlockSpec((lp, hw), lambda b, h, i: (k_blk0 + b, h))],
        out_specs=pl.BlockSpec((tq, hw), lambda b, h, i: (b * nq + i, h)),
        out_shape=jax.ShapeDtypeStruct((nb * nq * tq, A_WIDTH), BF16),
        scratch_shapes=([pltpu.VMEM((2 * tq, LANES), F32) for _ in range(3)]
                        + [pltpu.VMEM((nkb, 2 * tq, tk), F32)]),
        compiler_params=_cparams(("arbitrary", "arbitrary", "arbitrary")),
        name="diff_attention",
    )(lqk, sub, q, k, v)


def _mla_kernel(q_ref, k_ref, v_ref, o_ref, m_sc, l_sc, acc_sc, sbuf, *, tq, tk, nkb, q_pos0, heads):
    i = pl.program_id(2)
    r0 = q_pos0 + i * tq
    n_full, n_vis = _block_range(r0, tq, tk, nkb)

    for h in range(heads):
        hl = slice(h * LANES, (h + 1) * LANES)
        q = q_ref[:, hl]

        def score(kb, nblk, masked, g, q=q, hl=hl):
            ks = pl.multiple_of(kb * tk, tk)
            s = _dot_nt(q, k_ref[pl.ds(ks, nblk * tk), hl])
            if masked:
                s = jnp.where(_visible(r0, ks, tq, nblk * tk), s, NEG_INF)
            return s

        def values(kb, nblk, g, hl=hl):
            return v_ref[pl.ds(pl.multiple_of(kb * tk, tk), nblk * tk), hl]

        o = _two_pass(n_full, n_vis, score, values, m_sc, l_sc, acc_sc, sbuf, (slice(0, tq),))
        o_ref[:, hl] = o.astype(BF16)


def _mla_call(q, k, v, *, nb, nq, tq, tk, lp, q_blk0, k_blk0, q_pos0, heads=1):
    nkb = lp // tk
    kern = functools.partial(_mla_kernel, tq=tq, tk=tk, nkb=nkb, q_pos0=q_pos0, heads=heads)
    hw = heads * LANES
    return pl.pallas_call(
        kern,
        grid=(nb, C_HEADS // heads, nq),
        in_specs=[pl.BlockSpec((tq, hw), lambda b, h, i: (q_blk0 + b * nq + i, h)),
                  pl.BlockSpec((lp, hw), lambda b, h, i: (k_blk0 + b, h)),
                  pl.BlockSpec((lp, hw), lambda b, h, i: (k_blk0 + b, h))],
        out_specs=pl.BlockSpec((tq, hw), lambda b, h, i: (b * nq + i, h)),
        out_shape=jax.ShapeDtypeStruct((nb * nq * tq, C_HEADS * LANES), BF16),
        scratch_shapes=([pltpu.VMEM((tq, LANES), F32) for _ in range(3)]
                        + [pltpu.VMEM((nkb, tq, tk), F32)]),
        compiler_params=_cparams(("arbitrary", "arbitrary", "arbitrary")),
        name="mla_attention",
    )(q, k, v)


def _dsa_kernel(q_ref, qi_ref, w_ref, k_ref, v_ref, ki_ref, o_ref,
                s_sc, m_sc, l_sc, acc_sc, t_sc, need_sc, carry_sc, wrep_sc, sbuf,
                *, tq, tk, nkb, q_pos0, n_keys, k_sel, rs, n_bisect):
    i = pl.program_id(1)
    r0 = q_pos0 + i * tq
    n_full, n_vis = _block_range(r0, tq, tk, nkb)
    lane = lax.broadcasted_iota(jnp.int32, (tq, LANES), 1)
    ksel_f = float(k_sel)

    qi = qi_ref[...]
    qa, qb = _split_halves(qi[:, :LANES], lane)
    qc, qd = _split_halves(qi[:, LANES:], lane)
    qi4 = jnp.concatenate([qa, qb, qc, qd], axis=0)
    wm = w_ref[...]
    for h in range(IDX_HEADS):
        wcol = wm[:, _MISC_BIW + h:_MISC_BIW + h + 1] * (IDX_HEADS ** -0.5)
        wrep_sc[h] = jnp.broadcast_to(wcol, (tq, LANES))

    def score_step(kb, masked):
        ks = pl.multiple_of(kb * tk, tk)
        rel = _dot_nt(qi4, ki_ref[pl.ds(ks, tk), :])
        cols = []
        for j in range(tk // LANES):
            cl = slice(j * LANES, (j + 1) * LANES)
            sc = wrep_sc[0] * jnp.maximum(rel[:tq, cl], 0.0)
            for h in range(1, IDX_HEADS):
                sc = sc + wrep_sc[h] * jnp.maximum(rel[h * tq:(h + 1) * tq, cl], 0.0)
            cols.append(sc)
        sc = jnp.concatenate(cols, axis=1)
        if masked:
            sc = jnp.where(_visible(r0, ks, tq, tk), sc, NEG_INF)
        s_sc[kb] = sc
        mx = m_sc[0:tq, :]
        mn = l_sc[0:tq, :]
        for j in range(tk // LANES):
            sj = sc[:, j * LANES:(j + 1) * LANES]
            mx = jnp.maximum(mx, sj)
            mn = jnp.minimum(mn, jnp.where(sj > HALF_NEG, sj, BIG_POS) if masked else sj)
        m_sc[0:tq, :] = mx
        l_sc[0:tq, :] = mn

    def score_full(kb, c):
        score_step(kb, False)
        return c

    def score_masked(kb, c):
        score_step(kb, True)
        return c

    m_sc[0:tq, :] = jnp.full((tq, LANES), BELOW_NEG, F32)
    l_sc[0:tq, :] = jnp.full((tq, LANES), BIG_POS, F32)
    lax.fori_loop(0, n_full, score_full, 0)
    lax.fori_loop(n_full, n_vis, score_masked, 0)

    nl = tk // LANES

    def search(sb, flag):
        rsl = pl.ds(pl.multiple_of(sb * rs, rs), rs)

        def fold(fn, init):
            def body(kb, part):
                s = s_sc[kb, rsl, :]
                for j in range(nl):
                    part = fn(part, s[:, j * LANES:(j + 1) * LANES])
                return part
            return lax.fori_loop(0, n_vis, body, init)

        def bc(x):
            return jnp.broadcast_to(x, (rs, LANES))

        def count(cmp, x):
            xb = bc(x)
            part = fold(lambda p, sj: p + jnp.where(cmp(sj, xb), 1.0, 0.0), jnp.zeros((rs, LANES), F32))
            return jnp.sum(part, axis=1, keepdims=True)

        def max_below(x, strict):
            xb = bc(x)
            if strict:
                part = fold(lambda p, sj: jnp.maximum(p, jnp.where(sj < xb, sj, BELOW_NEG)),
                            jnp.full((rs, LANES), BELOW_NEG, F32))
            else:
                part = fold(lambda p, sj: jnp.maximum(p, jnp.where(sj <= xb, sj, BELOW_NEG)),
                            jnp.full((rs, LANES), BELOW_NEG, F32))
            return jnp.max(part, axis=1, keepdims=True)

        ge = lambda a, b: a >= b
        gt = lambda a, b: a > b

        hi = jnp.max(m_sc[rsl, :], axis=1, keepdims=True)
        lo = jnp.min(l_sc[rsl, :], axis=1, keepdims=True)
        rpos = r0 + sb * rs + lax.broadcasted_iota(jnp.int32, (rs, 1), 0)
        n_valid = jnp.minimum(((rpos >> 6) + 1) << 6, n_keys)
        small = n_valid < k_sel

        def bisect_step(c):
            lo_c, hi_c, glo_c, ghi_c = c
            mid = 0.5 * (lo_c + hi_c)
            g = count(ge, mid)
            up = g >= ksel_f
            return (jnp.where(up, mid, lo_c), jnp.where(up, hi_c, mid),
                    jnp.where(up, g, glo_c), jnp.where(up, ghi_c, g))

        state = (lo, hi, n_valid.astype(F32), jnp.zeros((rs, 1), F32))
        state = lax.fori_loop(0, n_bisect, lambda _, c: bisect_step(c), state)

        def more_cond(c):
            return jnp.logical_and(c[0] < BISECT_EXTRA, jnp.max(c[1][2] - c[1][3]) > 1.5)

        def more_body(c):
            return c[0] + 1, bisect_step(c[1])

        _, state = lax.while_loop(more_cond, more_body, (jnp.int32(0), state))
        hi = state[1]

        v0 = max_below(hi, False)
        g0 = count(ge, v0)
        done0 = jnp.where(jnp.logical_or(g0 >= ksel_f, small), 1.0, 0.0)

        def walk_cond(c):
            return jnp.min(c[2]) < 0.5

        def walk_body(c):
            t_c, g_c, done_c = c
            v = max_below(t_c, True)
            g = count(ge, v)
            keep = done_c > 0.5
            return (jnp.where(keep, t_c, v), jnp.where(keep, g_c, g),
                    jnp.where(jnp.logical_or(keep, g >= ksel_f), 1.0, 0.0))

        t, g_t, _ = lax.while_loop(walk_cond, walk_body, (v0, g0, done0))
        c_gt = count(gt, t)
        need = ksel_f - c_gt
        excess = jnp.logical_and(jnp.logical_and(g_t - c_gt > need, jnp.logical_not(small)), t > HALF_NEG)
        t_sc[rsl, :] = jnp.where(small, NEG_INF, t)
        need_sc[rsl, :] = need
        return jnp.maximum(flag, jnp.max(jnp.where(excess, 1.0, 0.0)))

    tie_flag = lax.fori_loop(0, tq // rs, search, jnp.float32(0.0))

    q = q_ref[...]
    q0a, q0b = _split_halves(q[:, :LANES], lane)
    q1a, q1b = _split_halves(q[:, LANES:], lane)
    qq = (jnp.concatenate([q0a, q0b], axis=0), jnp.concatenate([q1a, q1b], axis=0))
    carry_sc[...] = jnp.zeros(carry_sc.shape, F32)
    t_all = t_sc[...]

    def bias_step(kb, masked):
        ks = pl.multiple_of(kb * tk, tk)
        sc = s_sc[kb]
        vis = _visible(r0, ks, tq, tk) if masked else None

        def store(sel):
            if masked:
                sel = jnp.logical_and(sel, vis)
            s_sc[kb] = jnp.where(sel, 0.0, NEG_INF)

        def plain():
            store(sc >= t_all)

        def with_ties():
            eq = sc == t_all
            if masked:
                eq = jnp.logical_and(eq, vis)
            eqf = jnp.where(eq, 1.0, 0.0)
            upper = (lax.broadcasted_iota(jnp.int32, (tk, tk), 0)
                     < lax.broadcasted_iota(jnp.int32, (tk, tk), 1))
            before = _bdot(eqf.astype(BF16), jnp.where(upper, 1.0, 0.0).astype(BF16)) + carry_sc[...]
            carry_sc[...] = carry_sc[...] + jnp.sum(eqf, axis=1, keepdims=True)
            store(jnp.logical_or(sc > t_all, jnp.logical_and(eq, before < need_sc[...])))

        lax.cond(tie_flag > 0.5, with_ties, plain)

    def bias_full(kb, c):
        bias_step(kb, False)
        return c

    def bias_masked(kb, c):
        bias_step(kb, True)
        return c

    lax.fori_loop(0, n_full, bias_full, 0)
    lax.fori_loop(n_full, n_vis, bias_masked, 0)

    def score(kb, nblk, masked, g):
        ks = pl.multiple_of(kb * tk, tk)
        bias = jnp.concatenate([s_sc[kb + w] for w in range(nblk)], axis=1) if nblk > 1 else s_sc[kb]
        s = _dot_nt(qq[g], k_ref[pl.ds(ks, nblk * tk), g * LANES:(g + 1) * LANES])
        return s + jnp.concatenate([bias, bias], axis=0)

    def values(kb, nblk, g):
        return v_ref[pl.ds(pl.multiple_of(kb * tk, tk), nblk * tk), g * LANES:(g + 1) * LANES]

    o = _two_pass(n_full, n_vis, score, values, m_sc, l_sc, acc_sc, sbuf,
                  (slice(0, 2 * tq), slice(2 * tq, 4 * tq)))
    for pr in range(2):
        lo_h = o[(2 * pr) * tq:(2 * pr + 1) * tq]
        hi_h = o[(2 * pr + 1) * tq:(2 * pr + 2) * tq]
        o_ref[:, pr * LANES:(pr + 1) * LANES] = jnp.where(lane < 64, lo_h, hi_h).astype(BF16)


def _dsa_call(q, qi, w, k, v, ki, *, nb, nq, tq, tk, lp, q_blk0, k_blk0, q_pos0, n_keys, n_bisect):
    nkb = lp // tk
    k_sel = min(DSA_TOPK, n_keys // 4)
    rs = min(64, tq)
    kern = functools.partial(_dsa_kernel, tq=tq, tk=tk, nkb=nkb, q_pos0=q_pos0, n_keys=n_keys,
                             k_sel=k_sel, rs=rs, n_bisect=n_bisect)
    return pl.pallas_call(
        kern,
        grid=(nb, nq),
        in_specs=[pl.BlockSpec((tq, B_WIDTH), lambda b, i: (q_blk0 + b * nq + i, 0)),
                  pl.BlockSpec((tq, B_WIDTH), lambda b, i: (q_blk0 + b * nq + i, 0)),
                  pl.BlockSpec((tq, LANES), lambda b, i: (q_blk0 + b * nq + i, 0)),
                  pl.BlockSpec((lp, B_WIDTH), lambda b, i: (k_blk0 + b, 0)),
                  pl.BlockSpec((lp, B_WIDTH), lambda b, i: (k_blk0 + b, 0)),
                  pl.BlockSpec((lp, LANES), lambda b, i: (k_blk0 + b, 0))],
        out_specs=pl.BlockSpec((tq, B_WIDTH), lambda b, i: (b * nq + i, 0)),
        out_shape=jax.ShapeDtypeStruct((nb * nq * tq, B_WIDTH), BF16),
        scratch_shapes=[pltpu.VMEM((nkb, tq, tk), F32),
                        pltpu.VMEM((4 * tq, LANES), F32), pltpu.VMEM((4 * tq, LANES), F32),
                        pltpu.VMEM((4 * tq, LANES), F32),
                        pltpu.VMEM((tq, 1), F32), pltpu.VMEM((tq, 1), F32), pltpu.VMEM((tq, 1), F32),
                        pltpu.VMEM((IDX_HEADS, tq, LANES), F32),
                        pltpu.VMEM((nkb, 2 * tq, tk), F32)],
        compiler_params=_cparams(("arbitrary", "arbitrary")),
        name="dsa_attention",
    )(q, qi, w, k, v, ki)


def _kb_kernel(x_ref, oa_ref, ob_ref, oc_ref, sha_ref, sca_ref, gta_ref, shm_ref, scm_ref,
               nmix_ref, nffn_ref, wg_ref, wba_ref, wbb_ref, wbc_ref, wout_ref, rw_ref, rb_ref,
               x1_o, h2_o, e_o, g_o, cnt_o, cnt_sc, *, tm, ch):
    x = x_ref[...]
    hb = _modulate(_rms(x, nmix_ref[...]), sca_ref[...], sha_ref[...], tm, ch).astype(BF16)

    def gate(c0):
        return _sigmoid(_bdot(hb, wg_ref[:, c0:c0 + D_MODEL]))

    merged = gate(0) * _bdot(oa_ref[...], wba_ref[...])
    merged = merged + gate(D_MODEL) * _bdot(ob_ref[...], wbb_ref[...])
    merged = merged + gate(2 * D_MODEL) * _bdot(oc_ref[...], wbc_ref[...])
    y = _bdot(merged.astype(BF16), wout_ref[...])
    x1 = x + _scale_rows(y, gta_ref[...], tm, ch)
    x1_o[...] = x1
    h2 = _modulate(_rms(x1, nffn_ref[...]), scm_ref[...], shm_ref[...], tm, ch)
    h2_o[...] = h2
    lg = _dot_split(h2, rw_ref[...]) + rb_ref[...]
    lanef = lax.broadcasted_iota(jnp.int32, (tm, LANES), 1).astype(F32)
    e_acc = jnp.zeros((tm, LANES), F32)
    v_acc = jnp.full((tm, LANES), NEG_INF, F32)
    chosen = jnp.zeros((tm, LANES), F32)
    picks = []
    for k in range(TOP_K):
        mx = jnp.max(lg, axis=1, keepdims=True)
        idx = jnp.min(jnp.where(lg == mx, lanef, float(LANES)), axis=1, keepdims=True)
        hit = lanef == idx
        picks.append(hit)
        chosen = jnp.where(hit, 1.0, chosen)
        e_acc = jnp.where(lanef == float(k), idx, e_acc)
        v_acc = jnp.where(lanef == float(k), mx, v_acc)
        lg = jnp.where(hit, BELOW_NEG, lg)
    ex = jnp.where(lanef < float(TOP_K), jnp.exp(v_acc - jnp.max(v_acc, axis=1, keepdims=True)), 0.0)
    g_o[...] = ex / jnp.sum(ex, axis=1, keepdims=True)
    @pl.when(pl.program_id(0) == 0)
    def _():
        cnt_sc[...] = jnp.zeros(cnt_sc.shape, F32)

    earlier = (lax.broadcasted_iota(jnp.int32, (tm, tm), 1) < lax.broadcasted_iota(jnp.int32, (tm, tm), 0))
    before = _bdot(jnp.where(earlier, 1.0, 0.0).astype(BF16), chosen.astype(BF16)) + cnt_sc[0:1, :]
    for k in range(TOP_K):
        rank = jnp.sum(jnp.where(picks[k], before, 0.0), axis=1, keepdims=True)
        e_acc = jnp.where(lanef == float(TOP_K + k), rank, e_acc)
    e_o[...] = e_acc.astype(jnp.int32)
    cnt_sc[...] = cnt_sc[...] + jnp.sum(chosen, axis=0, keepdims=True)
    cnt_o[...] = cnt_sc[...]


def _kb_call(x, oa, ob, oc, mods, nmix, nffn, wg, wba, wbb, wbc, wout, rw, rb):
    n = x.shape[0]
    tm = TOKEN_BLOCK
    ch = tm // MOD_ROWS

    def row(i):
        return (i, 0)

    def full(a):
        return pl.BlockSpec(a.shape, lambda i: (0, 0))

    in_specs = ([pl.BlockSpec((tm, D_MODEL), row), pl.BlockSpec((tm, A_WIDTH), row),
                 pl.BlockSpec((tm, B_WIDTH), row), pl.BlockSpec((tm, 512), row)]
                + [pl.BlockSpec((MOD_ROWS, D_MODEL), row) for _ in range(5)]
                + [full(a) for a in (nmix, nffn, wg, wba, wbb, wbc, wout, rw, rb)])
    out_shape = [jax.ShapeDtypeStruct((n, D_MODEL), F32), jax.ShapeDtypeStruct((n, D_MODEL), F32),
                 jax.ShapeDtypeStruct((n, LANES), jnp.int32), jax.ShapeDtypeStruct((n, LANES), F32),
                 jax.ShapeDtypeStruct((8, LANES), F32)]
    out_specs = [pl.BlockSpec((tm, D_MODEL), row), pl.BlockSpec((tm, D_MODEL), row),
                 pl.BlockSpec((tm, LANES), row), pl.BlockSpec((tm, LANES), row),
                 pl.BlockSpec((8, LANES), lambda i: (0, 0))]
    return pl.pallas_call(
        functools.partial(_kb_kernel, tm=tm, ch=ch),
        grid=(n // tm,),
        in_specs=in_specs,
        out_specs=out_specs,
        out_shape=out_shape,
        scratch_shapes=[pltpu.VMEM((8, LANES), F32)],
        compiler_params=_cparams(("arbitrary",)),
        name="post_attention",
    )(x, oa, ob, oc, *mods, nmix, nffn, wg, wba, wbb, wbc, wout, rw, rb)


def _ke_kernel(be_ref, nu_ref, x_ref, wg_ref, wl_ref, wd_ref, bg_ref, bl_ref, bd_ref, y_ref):
    i = pl.program_id(0)

    @pl.when(i < nu_ref[0])
    def _():
        x = x_ref[...].astype(BF16)
        g = jnp.minimum(_bdot(x, wg_ref[0]) + bg_ref[0], SWIGLU_LIMIT)
        l = jnp.clip(_bdot(x, wl_ref[0]) + bl_ref[0], -SWIGLU_LIMIT, SWIGLU_LIMIT)
        act = g * _sigmoid(SWIGLU_ALPHA * g) * (l + 1.0)
        y_ref[...] = _bdot(act.astype(BF16), wd_ref[0]) + bd_ref[0]

    @pl.when(i >= nu_ref[0])
    def _():
        y_ref[...] = jnp.zeros(y_ref.shape, F32)


def _ke_call(blk_e, n_used, xg, wg, wl, wd, bg, bl, bd):
    n_rows = xg.shape[0]
    eb = EXPERT_ROWS
    n_blocks = n_rows // eb
    grid_spec = pltpu.PrefetchScalarGridSpec(
        num_scalar_prefetch=2,
        grid=(n_blocks,),
        in_specs=[pl.BlockSpec((eb, D_MODEL), lambda i, be, nu: (i, 0)),
                  pl.BlockSpec((1, D_MODEL, D_FF), lambda i, be, nu: (be[i], 0, 0)),
                  pl.BlockSpec((1, D_MODEL, D_FF), lambda i, be, nu: (be[i], 0, 0)),
                  pl.BlockSpec((1, D_FF, D_MODEL), lambda i, be, nu: (be[i], 0, 0)),
                  pl.BlockSpec((1, 1, D_FF), lambda i, be, nu: (be[i], 0, 0)),
                  pl.BlockSpec((1, 1, D_FF), lambda i, be, nu: (be[i], 0, 0)),
                  pl.BlockSpec((1, 1, D_MODEL), lambda i, be, nu: (be[i], 0, 0))],
        out_specs=pl.BlockSpec((eb, D_MODEL), lambda i, be, nu: (i, 0)),
    )
    return pl.pallas_call(
        _ke_kernel,
        grid_spec=grid_spec,
        out_shape=jax.ShapeDtypeStruct((n_rows, D_MODEL), F32),
        compiler_params=_cparams(("arbitrary",)),
        name="moe_experts",
    )(blk_e, n_used, xg, wg, wl, wd, bg, bl, bd)


def _kc_kernel(dcur_ref, dnext_ref, x1_ref, yr_hbm, g_ref, gtm_ref, fn_ref, x2_o, y_o, buf, sem,
               *, tm, ch):
    i = pl.program_id(0)
    nblk = pl.num_programs(0)
    slot = lax.rem(i, 2)
    nrow = TOP_K * tm

    def row_copy(d_ref, s, r):
        return pltpu.make_async_copy(yr_hbm.at[pl.ds(d_ref[0, 0, r], 1), :],
                                     buf.at[s, pl.ds(r, 1), :], sem.at[s])

    def issue(d_ref, s):
        def body(r, c):
            row_copy(d_ref, s, r).start()
            return c
        lax.fori_loop(0, nrow, body, 0, unroll=8)

    @pl.when(i == 0)
    def _():
        issue(dcur_ref, 0)

    @pl.when(i + 1 < nblk)
    def _():
        issue(dnext_ref, 1 - slot)

    pltpu.make_async_copy(buf.at[slot], buf.at[slot], sem.at[slot]).wait()
    gate = g_ref[...]
    ffn = gate[:, 0:1] * buf[slot, 0:tm, :]
    for k in range(1, TOP_K):
        ffn = ffn + gate[:, k:k + 1] * buf[slot, k * tm:(k + 1) * tm, :]
    x2 = x1_ref[...] + _scale_rows(ffn, gtm_ref[...], tm, ch)
    x2_o[...] = x2
    y_o[...] = _rms(x2, fn_ref[...])


def _kc_call(x1, yr, dest_blocks, gate, gtm, fnorm):
    n = x1.shape[0]
    tm = TOKEN_BLOCK
    ch = tm // MOD_ROWS
    nblk = n // tm
    smem = functools.partial(pl.BlockSpec, (1, 1, TOP_K * tm), memory_space=pltpu.SMEM)
    return pl.pallas_call(
        functools.partial(_kc_kernel, tm=tm, ch=ch),
        grid=(nblk,),
        in_specs=[smem(lambda i: (i, 0, 0)),
                  smem(lambda i: (jnp.minimum(i + 1, nblk - 1), 0, 0)),
                  pl.BlockSpec((tm, D_MODEL), lambda i: (i, 0)),
                  pl.BlockSpec(memory_space=pl.ANY),
                  pl.BlockSpec((tm, LANES), lambda i: (i, 0)),
                  pl.BlockSpec((MOD_ROWS, D_MODEL), lambda i: (i, 0)),
                  pl.BlockSpec((1, D_MODEL), lambda i: (0, 0))],
        out_specs=[pl.BlockSpec((tm, D_MODEL), lambda i: (i, 0)), pl.BlockSpec((tm, D_MODEL), lambda i: (i, 0))],
        out_shape=[jax.ShapeDtypeStruct((n, D_MODEL), F32), jax.ShapeDtypeStruct((n, D_MODEL), F32)],
        scratch_shapes=[pltpu.VMEM((2, TOP_K * tm, D_MODEL), F32), pltpu.SemaphoreType.DMA((2,))],
        compiler_params=_cparams(("arbitrary",)),
        name="moe_combine",
    )(dest_blocks, dest_blocks, x1, yr, gate, gtm, fnorm)


def _rope_tables(pos):
    lane = np.arange(LANES)
    inv32 = ROPE_THETA ** (-jnp.arange(32, dtype=F32) / 32)
    inv16 = ROPE_THETA ** (-jnp.arange(16, dtype=F32) / 16)
    ang64 = pos[:, None] * inv32[None, :][:, lane & 31]
    ang32 = pos[:, None] * inv16[None, :][:, lane & 15]
    sign64 = jnp.asarray(np.where((lane & 63) < 32, -1.0, 1.0), F32)[None, :]
    sign32 = jnp.asarray(np.where((lane & 31) < 16, -1.0, 1.0), F32)[None, :]
    in_m = jnp.asarray(lane < C_ROPE)[None, :]
    in_q = jnp.asarray((lane >= C_NOPE) & (lane < C_NOPE + C_ROPE))[None, :]
    cos64, sin64 = jnp.cos(ang64), jnp.sin(ang64) * sign64
    cos32, sin32 = jnp.cos(ang32), jnp.sin(ang32) * sign32
    return (cos64, sin64,
            jnp.where(in_m, cos32, 1.0), jnp.where(in_m, sin32, 0.0),
            jnp.where(in_q, cos32, 1.0), jnp.where(in_q, sin32, 0.0))


def _layer_weights(l, w_in, mla_w_uq, mla_w_ukv, w_br_c, router_w, router_b, exp_w_gu, exp_b_gu):
    wi = w_in[l]
    z = lambda n: jnp.zeros((D_MODEL, n), F32)
    w1 = jnp.concatenate([
        wi[:, _OFF_AQ:_OFF_BIK],
        wi[:, _OFF_BIK:_OFF_BIW], wi[:, _OFF_BIK:_OFF_BIW],
        wi[:, _OFF_CKR:_OFF_GATES], wi[:, _OFF_BIW:_OFF_CQ], z(LANES - C_ROPE - IDX_HEADS),
        wi[:, _OFF_CQ:_OFF_CKV], wi[:, _OFF_CKV:_OFF_CKR]], axis=1).astype(BF16)
    wg = wi[:, _OFF_GATES:].astype(BF16)
    wuq = mla_w_uq[l].reshape(Q_LORA, C_HEADS, C_NOPE + C_ROPE)
    wuq = jnp.pad(wuq, ((0, 0), (0, 0), (0, LANES - C_NOPE - C_ROPE))).reshape(Q_LORA, C_HEADS * LANES).astype(BF16)
    wukv = mla_w_ukv[l].reshape(KV_LORA, C_HEADS, C_NOPE + C_V)
    wk = jnp.pad(wukv[:, :, :C_NOPE], ((0, 0), (0, 0), (0, LANES - C_NOPE))).reshape(KV_LORA, C_HEADS * LANES).astype(BF16)
    wv = jnp.pad(wukv[:, :, C_NOPE:], ((0, 0), (0, 0), (0, LANES - C_V))).reshape(KV_LORA, C_HEADS * LANES).astype(BF16)
    e = np.zeros((LANES, C_HEADS * LANES), np.float32)
    for h in range(C_HEADS):
        e[np.arange(C_ROPE), h * LANES + C_NOPE + np.arange(C_ROPE)] = 1.0
    we = jnp.asarray(e, BF16)
    wbc = jnp.pad(w_br_c[l].reshape(C_HEADS, C_V, D_MODEL), ((0, 0), (0, LANES - C_V), (0, 0)))
    wbc = wbc.reshape(C_HEADS * LANES, D_MODEL).astype(BF16)
    rw = jnp.pad(router_w[l], ((0, 0), (0, LANES - N_EXPERTS)))
    rb = jnp.pad(router_b[l], (0, LANES - N_EXPERTS), constant_values=NEG_INF).reshape(1, LANES)
    wgl = exp_w_gu[l].reshape(N_EXPERTS, D_MODEL, D_FF, 2)
    bgl = exp_b_gu[l].reshape(N_EXPERTS, 1, D_FF, 2)
    return dict(w1=w1, wg=wg, wuq=wuq, wk=wk, wv=wv, we=we, wbc=wbc, rw=rw, rb=rb,
                e_wg=wgl[..., 0].astype(BF16), e_wl=wgl[..., 1].astype(BF16),
                e_bg=bgl[..., 0], e_bl=bgl[..., 1])


def _route(e_pad, cnt, n_rows_pad):
    eb = EXPERT_ROWS
    flat_e = e_pad[:, :TOP_K].reshape(-1)
    rank = e_pad[:, TOP_K:2 * TOP_K].reshape(-1)
    nk = flat_e.shape[0]
    counts = cnt[0, :N_EXPERTS].astype(jnp.int32)
    padded = (counts + eb - 1) // eb * eb
    pad_end = jnp.cumsum(padded)
    pad_start = pad_end - padded
    dest = pad_start[flat_e] + rank
    row_tok = jnp.zeros((n_rows_pad,), jnp.int32).at[dest].set(jnp.arange(nk, dtype=jnp.int32) // TOP_K)
    n_blocks = n_rows_pad // eb
    blk_start = jnp.arange(n_blocks, dtype=jnp.int32) * eb
    blk_e = jnp.sum((pad_end[None, :] <= blk_start[:, None]).astype(jnp.int32), axis=1)
    blk_e = jnp.minimum(blk_e, N_EXPERTS - 1)
    n_used = (pad_end[-1] // eb).astype(jnp.int32).reshape(1)
    return dest, row_tok, blk_e, n_used


def kernel(x_prompt, x_sample, cache_a_k, cache_a_v, cache_b_k, cache_b_v, cache_b_idx_k, cache_c_latent, cache_c_k_rope, c_prompt, c_sample, w_ada, b_ada, norm_mix, norm_ffn, w_in, diff_lq1, diff_lk1, diff_lq2, diff_lk2, diff_subln, mla_q_norm, mla_w_uq, mla_kv_norm, mla_w_ukv, w_br_a, w_br_b, w_br_c, w_out, router_w, router_b, exp_w_gu, exp_b_gu, exp_w_down, exp_b_down, final_norm):
    depth = w_ada.shape[0]
    bp, tp, _ = x_prompt.shape
    bs, ts, _ = x_sample.shape
    past = cache_c_latent.shape[2]
    n_p, n_s = bp * tp, bs * ts
    n = n_p + n_s
    tm = TOKEN_BLOCK
    ch = tm // MOD_ROWS
    assert ts == CHUNK and tp % tm == 0 and n_s % tm == 0 and past % CHUNK == 0

    x = jnp.concatenate([x_prompt.reshape(n_p, D_MODEL), x_sample.reshape(n_s, D_MODEL)], axis=0)

    n_seq = bp + bs
    c_all = jnp.concatenate([c_prompt, c_sample], axis=0)
    c_pad = jnp.pad(c_all, ((0, (-n_seq) % 8), (0, 0)))
    mod = _ada_call(c_pad, w_ada, b_ada)
    def per_chunk(m, reps):
        return jnp.broadcast_to(m[:, :, None, :], m.shape[:2] + (reps, m.shape[2])).reshape(depth, -1, m.shape[2])

    mod_rows = jnp.concatenate([per_chunk(mod[:, :bp], tp // ch), per_chunk(mod[:, bp:n_seq], ts // ch)],
                               axis=1)

    pos = jnp.concatenate([jnp.arange(tp, dtype=F32),
                           jnp.tile(past + jnp.arange(ts, dtype=F32), tm // ts)])
    tabs = _rope_tables(pos)

    tq_p = min(256, tp)
    tk_p = min(512, tp)
    nq_p = tp // tq_p
    tq_c = min(512, tp)
    l_s = past + ts
    tk_s = 384
    lp_s = -(-l_s // tk_s) * tk_s
    n_rows_pad = -(-(n * TOP_K + N_EXPERTS * (EXPERT_ROWS - 1)) // EXPERT_ROWS) * EXPERT_ROWS

    def with_cache(cache_l, new, width):
        parts = [cache_l.reshape(bs, past, width).astype(BF16), new.reshape(bs, ts, width)]
        if lp_s > l_s:
            parts.append(jnp.zeros((bs, lp_s - l_s, width), BF16))
        return jnp.concatenate(parts, axis=1).reshape(bs * lp_s, width)

    caches = [[] for _ in range(7)]
    y = None
    for l in range(depth):
        lam_init = 0.8 - 0.6 * math.exp(-0.3 * l)
        wl = _layer_weights(l, w_in, mla_w_uq, mla_w_ukv, w_br_c, router_w, router_b, exp_w_gu, exp_b_gu)
        m6 = [mod_rows[l, :, j * D_MODEL:(j + 1) * D_MODEL] for j in range(6)]
        sh_a, sc_a, gt_a, sh_m, sc_m, gt_m = m6
        nmix = norm_mix[l].reshape(1, D_MODEL)
        nffn = norm_ffn[l].reshape(1, D_MODEL)

        (ak, av, bk, bv, bik, clat, ckr, misc,
         aq_b, ak_b, av_b, bq_b, bk_b, bv_b, biq_b, bik2_b, cq_b, ck_b, cv_b) = _ka_call(
            x, sh_a, sc_a, tabs, nmix, wl['w1'], mla_q_norm[l].reshape(1, Q_LORA), wl['wuq'],
            mla_kv_norm[l].reshape(1, KV_LORA), wl['wk'], wl['we'], wl['wv'],
            n_prompt_blocks=n_p // tm, tab_blocks=tp // tm)
        for i, a in enumerate((ak, av, bk, bv, bik, clat, ckr)):
            caches[i].append(a)

        lqk = jnp.pad(jnp.stack([diff_lq1[l], diff_lk1[l], diff_lq2[l], diff_lk2[l]]),
                      ((0, 4), (0, LANES - A_HD)))
        sub = diff_subln[l].reshape(1, 2 * A_HD)
        oa_p = _diff_call(lqk, sub, aq_b, ak_b, av_b, nb=bp, nq=nq_p, tq=tq_p, tk=tk_p, lp=tp,
                          q_blk0=0, k_blk0=0, q_pos0=0, lam_init=lam_init)
        ak_s = with_cache(cache_a_k[l], ak_b[n_p:], A_WIDTH)
        av_s = with_cache(cache_a_v[l], av_b[n_p:], A_WIDTH)
        oa_s = _diff_call(lqk, sub, aq_b, ak_s, av_s, nb=bs, nq=1, tq=ts, tk=tk_s, lp=lp_s,
                          q_blk0=n_p // ts, k_blk0=0, q_pos0=past, lam_init=lam_init, heads=A_HEADS)
        ob_p = _dsa_call(bq_b, biq_b, misc, bk_b, bv_b, bik2_b, nb=bp, nq=nq_p, tq=tq_p, tk=tk_p, lp=tp,
                         q_blk0=0, k_blk0=0, q_pos0=0, n_keys=tp, n_bisect=12)
        bk_s = with_cache(cache_b_k[l], bk_b[n_p:], B_WIDTH)
        bv_s = with_cache(cache_b_v[l], bv_b[n_p:], B_WIDTH)
        cik = cache_b_idx_k[l].reshape(bs, past, IDX_DIM)
        bik_s = with_cache(jnp.concatenate([cik, cik], axis=-1), bik2_b[n_p:], LANES)
        ob_s = _dsa_call(bq_b, biq_b, misc, bk_s, bv_s, bik_s, nb=bs, nq=1, tq=ts, tk=tk_s, lp=lp_s,
                         q_blk0=n_p // ts, k_blk0=0, q_pos0=past, n_keys=l_s, n_bisect=10)
        oc_p = _mla_call(cq_b, ck_b, cv_b, nb=bp, nq=tp // tq_c, tq=tq_c, tk=tk_p, lp=tp,
                         q_blk0=0, k_blk0=0, q_pos0=0)
        lat_c = cache_c_latent[l].reshape(bs * past, KV_LORA)
        kr_c = jnp.pad(cache_c_k_rope[l].reshape(bs * past, C_ROPE), ((0, 0), (0, LANES - C_ROPE)))
        ck_c, cv_c = _mla_kv_call(lat_c, kr_c, wl['wk'], wl['we'], wl['wv'])
        ck_s = with_cache(ck_c, ck_b[n_p:], 512)
        cv_s = with_cache(cv_c, cv_b[n_p:], 512)
        oc_s = _mla_call(cq_b, ck_s, cv_s, nb=bs, nq=1, tq=ts, tk=tk_s, lp=lp_s,
                         q_blk0=n_p // ts, k_blk0=0, q_pos0=past, heads=C_HEADS)

        oa = jnp.concatenate([oa_p, oa_s], axis=0)
        ob = jnp.concatenate([ob_p, ob_s], axis=0)
        oc = jnp.concatenate([oc_p, oc_s], axis=0)

        x1, h2, e_pad, g_pad, cnt = _kb_call(
            x, oa, ob, oc, (sh_a, sc_a, gt_a, sh_m, sc_m), nmix, nffn, wl['wg'],
            w_br_a[l].astype(BF16), w_br_b[l].astype(BF16), wl['wbc'], w_out[l].astype(BF16),
            wl['rw'], wl['rb'])

        dest, row_tok, blk_e, n_used = _route(e_pad, cnt, n_rows_pad)
        xg = jnp.take(h2, row_tok, axis=0)
        yr = _ke_call(blk_e, n_used, xg, wl['e_wg'], wl['e_wl'], exp_w_down[l].astype(BF16),
                      wl['e_bg'], wl['e_bl'], exp_b_down[l].reshape(N_EXPERTS, 1, D_MODEL))
        dest_blocks = dest.reshape(n // tm, tm, TOP_K).transpose(0, 2, 1).reshape(n // tm, 1, TOP_K * tm)
        x, y = _kc_call(x1, yr, dest_blocks, g_pad, gt_m, final_norm.reshape(1, D_MODEL))

    def split(a, tail):
        a = jnp.stack(a, axis=0)
        return (a[:, :n_p].reshape((depth, bp, tp) + tail), a[:, n_p:].reshape((depth, bs, ts) + tail))

    tails = ((A_HEADS, 2 * A_HD), (A_HEADS, 2 * A_HD), (B_HEADS, B_HD), (B_HEADS, B_HD),
             (IDX_DIM,), (KV_LORA,), (C_ROPE,))
    ps = [split(c, t) for c, t in zip(caches, tails)]
    y_prompt = y[:n_p].reshape(bp, tp, D_MODEL)
    y_sample = y[n_p:].reshape(bs, ts, D_MODEL)
    return (y_prompt, y_sample) + tuple(p[0] for p in ps) + tuple(p[1] for p in ps)
```

```python
import functools
import math

import numpy as np
import jax
import jax.numpy as jnp
from jax import lax
from jax.experimental import pallas as pl
from jax.experimental.pallas import tpu as pltpu

F32 = jnp.float32
BF16 = jnp.bfloat16

D_MODEL = 1024
CHUNK = 64
ROPE_THETA = 10000.0
NORM_EPS = 1e-6
NEG_INF = -1e30
HALF_NEG = -5e29
BELOW_NEG = -3e38
BIG_POS = 3e38
LOG2E = 1.4426950408889634

A_HEADS, A_HD = 4, 64
B_HEADS, B_HD = 4, 64
IDX_HEADS, IDX_DIM = 4, 64
DSA_TOPK = 256
C_HEADS, C_NOPE, C_ROPE, C_V = 4, 64, 32, 64
Q_LORA, KV_LORA = 256, 128
N_EXPERTS, TOP_K = 32, 4
D_FF = D_MODEL
SWIGLU_LIMIT = 7.0
SWIGLU_ALPHA = 1.702

A_WIDTH = A_HEADS * 2 * A_HD
B_WIDTH = B_HEADS * B_HD
LANES = 128

_OFF_AQ, _OFF_AK, _OFF_AV = 0, 512, 1024
_OFF_BQ, _OFF_BK, _OFF_BV = 1536, 1792, 2048
_OFF_BIQ, _OFF_BIK, _OFF_BIW = 2304, 2560, 2624
_OFF_CQ, _OFF_CKV, _OFF_CKR = 2628, 2884, 3012
_OFF_GATES = 3044
_P_AQ, _P_AK, _P_AV = 0, 512, 1024
_P_BQ, _P_BK, _P_BV = 1536, 1792, 2048
_P_BIQ, _P_BIK2, _P_MISC, _P_CQ, _P_CKV, _P_END = 2304, 2560, 2688, 2816, 3072, 3200
_MISC_BIW = 32

TOKEN_BLOCK = 256
MOD_ROWS = 8
EXPERT_ROWS = 256
VMEM_LIMIT = 56 * 1024 * 1024


def _cparams(sem, vmem=VMEM_LIMIT):
    return pltpu.CompilerParams(dimension_semantics=sem, vmem_limit_bytes=vmem)


def _rms(xf, g):
    return xf * lax.rsqrt(jnp.mean(xf * xf, axis=-1, keepdims=True) + NORM_EPS) * g


def _sigmoid(x):
    return 1.0 / (1.0 + jnp.exp(-x))


def _bdot(a, b):
    return jnp.dot(a, b, preferred_element_type=F32)


def _dot_nt(a, b):
    return lax.dot_general(a, b, (((1,), (1,)), ((), ())), preferred_element_type=F32)


def _dot_split(a, b):
    a_hi = a.astype(BF16)
    b_hi = b.astype(BF16)
    a_lo = (a - a_hi.astype(F32)).astype(BF16)
    b_lo = (b - b_hi.astype(F32)).astype(BF16)
    return _bdot(a_hi, b_hi) + (_bdot(a_hi, b_lo) + _bdot(a_lo, b_hi))


def _modulate(xn, sc, sh, rows, ch):
    n = rows // ch
    y = xn.reshape(n, ch, D_MODEL) * (1.0 + sc)[:, None, :] + sh[:, None, :]
    return y.reshape(rows, D_MODEL)


def _scale_rows(y, g, rows, ch):
    n = rows // ch
    return (y.reshape(n, ch, D_MODEL) * g[:, None, :]).reshape(rows, D_MODEL)


def _ada_kernel(c_ref, w_ref, b_ref, o_ref):
    c = c_ref[...]
    s = c * _sigmoid(c)
    o_ref[0] = _dot_split(s, w_ref[0]) + b_ref[0]


def _ada_call(c_pad, w_ada, b_ada):
    depth = w_ada.shape[0]
    mp = c_pad.shape[0]
    return pl.pallas_call(
        _ada_kernel,
        grid=(depth, 6),
        in_specs=[
            pl.BlockSpec((mp, D_MODEL), lambda l, j: (0, 0)),
            pl.BlockSpec((1, D_MODEL, D_MODEL), lambda l, j: (l, 0, j)),
            pl.BlockSpec((1, 1, D_MODEL), lambda l, j: (l, 0, j)),
        ],
        out_specs=pl.BlockSpec((1, mp, D_MODEL), lambda l, j: (l, 0, j)),
        out_shape=jax.ShapeDtypeStruct((depth, mp, 6 * D_MODEL), F32),
        compiler_params=_cparams(("arbitrary", "arbitrary")),
        name="ada_mod",
    )(c_pad, w_ada, b_ada.reshape(depth, 1, 6 * D_MODEL))


def _rope_partner(x, lane, half):
    first = (lane & (2 * half - 1)) < half
    return jnp.where(first, pltpu.roll(x, LANES - half, 1), pltpu.roll(x, half, 1))


def _ka_kernel(x_ref, sh_ref, sc_ref, cos_ref, sin_ref, cosm_ref, sinm_ref, cosq_ref, sinq_ref,
               nmix_ref, w1_ref, qn_ref, wuq_ref, kvn_ref, wk_ref, we_ref, wv_ref,
               ak_o, av_o, bk_o, bv_o, bik_o, clat_o, ckr_o, misc_o,
               aq_b, ak_b, av_b, bq_b, bk_b, bv_b, biq_b, bik2_b, cq_b, ck_b, cv_b, *, tm, ch):
    x = x_ref[...]
    h = _modulate(_rms(x, nmix_ref[...]), sc_ref[...], sh_ref[...], tm, ch)
    hb = h.astype(BF16)
    lane = lax.broadcasted_iota(jnp.int32, (tm, LANES), 1)
    cos = cos_ref[...]
    sin = sin_ref[...]

    def proj(c0, c1):
        return _bdot(hb, w1_ref[:, c0:c1])

    def rope64(xb):
        return xb * cos + _rope_partner(xb, lane, 32) * sin

    a_scale = (A_HD ** -0.5) * LOG2E
    b_scale = (B_HD ** -0.5) * LOG2E
    i_scale = IDX_DIM ** -0.5
    c_scale = ((C_NOPE + C_ROPE) ** -0.5) * LOG2E

    p = proj(_P_AQ, _P_AK)
    for c in range(A_WIDTH // LANES):
        sl = slice(c * LANES, (c + 1) * LANES)
        aq_b[:, sl] = (rope64(p[:, sl]) * a_scale).astype(BF16)
    p = proj(_P_AK, _P_AV)
    for c in range(A_WIDTH // LANES):
        sl = slice(c * LANES, (c + 1) * LANES)
        r = rope64(p[:, sl])
        ak_o[:, c, :] = r
        ak_b[:, sl] = r.astype(BF16)
    p = proj(_P_AV, _P_BQ)
    for c in range(A_HEADS):
        av_o[:, c, :] = p[:, c * LANES:(c + 1) * LANES]
    av_b[...] = p.astype(BF16)
    p = proj(_P_BQ, _P_BK)
    for c in range(B_WIDTH // LANES):
        sl = slice(c * LANES, (c + 1) * LANES)
        bq_b[:, sl] = (rope64(p[:, sl]) * b_scale).astype(BF16)
    p = proj(_P_BK, _P_BV)
    for c in range(B_WIDTH // LANES):
        sl = slice(c * LANES, (c + 1) * LANES)
        r = rope64(p[:, sl])
        bk_o[:, 2 * c, :] = r[:, :B_HD]
        bk_o[:, 2 * c + 1, :] = r[:, B_HD:]
        bk_b[:, sl] = r.astype(BF16)
    p = proj(_P_BV, _P_BIQ)
    for h in range(B_HEADS):
        bv_o[:, h, :] = p[:, h * B_HD:(h + 1) * B_HD]
    bv_b[...] = p.astype(BF16)
    p = proj(_P_BIQ, _P_BIK2)
    for c in range(B_WIDTH // LANES):
        sl = slice(c * LANES, (c + 1) * LANES)
        biq_b[:, sl] = (rope64(p[:, sl]) * i_scale).astype(BF16)
    r = rope64(proj(_P_BIK2, _P_MISC))
    bik_o[...] = r[:, :IDX_DIM]
    bik2_b[...] = r.astype(BF16)
    pm = proj(_P_MISC, _P_CQ)
    misc = pm * cosm_ref[...] + _rope_partner(pm, lane, 16) * sinm_ref[...]
    misc_o[...] = misc
    ckr_o[...] = misc[:, :C_ROPE]
    qlat = _rms(proj(_P_CQ, _P_CKV), qn_ref[...]).astype(BF16)
    cqf = _bdot(qlat, wuq_ref[...])
    cosq = cosq_ref[...]
    sinq = sinq_ref[...]
    for c in range(C_HEADS):
        sl = slice(c * LANES, (c + 1) * LANES)
        xb = cqf[:, sl]
        cq_b[:, sl] = ((xb * cosq + _rope_partner(xb, lane, 16) * sinq) * c_scale).astype(BF16)
    clat = _rms(proj(_P_CKV, _P_END), kvn_ref[...])
    clat_o[...] = clat
    clb = clat.astype(BF16)
    ck_b[...] = (_bdot(clb, wk_ref[...]) + _bdot(misc.astype(BF16), we_ref[...])).astype(BF16)
    cv_b[...] = _bdot(clb, wv_ref[...]).astype(BF16)


def _ka_call(x, sh, sc, tabs, nmix, w1, qn, wuq, kvn, wk, we, wv, *, n_prompt_blocks, tab_blocks):
    n = x.shape[0]
    tm = TOKEN_BLOCK
    ch = tm // MOD_ROWS
    nblk = n // tm

    def row(i):
        return (i, 0)

    def tab(i):
        return (jnp.where(i < n_prompt_blocks, i % tab_blocks, tab_blocks), 0)

    def const(i):
        return (0, 0)

    def full(a):
        return pl.BlockSpec(a.shape, const)

    heads_f32 = ((A_HEADS, 2 * A_HD), (A_HEADS, 2 * A_HD), (B_HEADS, B_HD), (B_HEADS, B_HD))
    widths_f32 = (IDX_DIM, KV_LORA, C_ROPE, LANES)
    widths_b16 = (A_WIDTH, A_WIDTH, A_WIDTH, B_WIDTH, B_WIDTH, B_WIDTH, B_WIDTH, LANES, 512, 512, 512)
    out_shape = ([jax.ShapeDtypeStruct((n,) + hd, F32) for hd in heads_f32]
                 + [jax.ShapeDtypeStruct((n, w), F32) for w in widths_f32]
                 + [jax.ShapeDtypeStruct((n, w), BF16) for w in widths_b16])
    out_specs = ([pl.BlockSpec((tm,) + hd, lambda i: (i, 0, 0)) for hd in heads_f32]
                 + [pl.BlockSpec((tm, w), row) for w in widths_f32 + widths_b16])
    in_specs = ([pl.BlockSpec((tm, D_MODEL), row),
                 pl.BlockSpec((MOD_ROWS, D_MODEL), row),
                 pl.BlockSpec((MOD_ROWS, D_MODEL), row)]
                + [pl.BlockSpec((tm, LANES), tab) for _ in range(6)]
                + [full(a) for a in (nmix, w1, qn, wuq, kvn, wk, we, wv)])
    return pl.pallas_call(
        functools.partial(_ka_kernel, tm=tm, ch=ch),
        grid=(nblk,),
        in_specs=in_specs,
        out_specs=out_specs,
        out_shape=out_shape,
        compiler_params=_cparams(("arbitrary",)),
        name="pre_attention",
    )(x, sh, sc, *tabs, nmix, w1, qn, wuq, kvn, wk, we, wv)


def _mla_kv_kernel(lat_ref, kr_ref, wk_ref, we_ref, wv_ref, ck_o, cv_o):
    lb = lat_ref[...].astype(BF16)
    ck_o[...] = (_bdot(lb, wk_ref[...]) + _bdot(kr_ref[...].astype(BF16), we_ref[...])).astype(BF16)
    cv_o[...] = _bdot(lb, wv_ref[...]).astype(BF16)


def _mla_kv_call(lat, krp, wk, we, wv):
    n = lat.shape[0]
    tm = 512
    return pl.pallas_call(
        _mla_kv_kernel,
        grid=(n // tm,),
        in_specs=[pl.BlockSpec((tm, KV_LORA), lambda i: (i, 0)),
                  pl.BlockSpec((tm, LANES), lambda i: (i, 0)),
                  pl.BlockSpec(wk.shape, lambda i: (0, 0)),
                  pl.BlockSpec(we.shape, lambda i: (0, 0)),
                  pl.BlockSpec(wv.shape, lambda i: (0, 0))],
        out_specs=[pl.BlockSpec((tm, 512), lambda i: (i, 0)), pl.BlockSpec((tm, 512), lambda i: (i, 0))],
        out_shape=[jax.ShapeDtypeStruct((n, 512), BF16), jax.ShapeDtypeStruct((n, 512), BF16)],
        compiler_params=_cparams(("arbitrary",)),
        name="mla_cache_kv",
    )(lat, krp, wk, we, wv)


def _block_range(r0, tq, tk, nkb):
    if isinstance(r0, int):
        return min((r0 + CHUNK) // tk, nkb), min((r0 + tq + tk - 1) // tk, nkb)
    n_full = jnp.minimum((r0 + CHUNK) // tk, nkb)
    n_vis = jnp.minimum((r0 + tq + tk - 1) // tk, nkb)
    return n_full, n_vis


def _visible(r0, ks, rows, tk):
    rpos = r0 + lax.broadcasted_iota(jnp.int32, (rows, tk), 0)
    kpos = ks + lax.broadcasted_iota(jnp.int32, (rows, tk), 1)
    return kpos < (((rpos >> 6) + 1) << 6)


def _fold_max(s, m_sc, rows):
    mp = m_sc[rows, :]
    for j in range(s.shape[1] // LANES):
        mp = jnp.maximum(mp, s[:, j * LANES:(j + 1) * LANES])
    m_sc[rows, :] = mp


def _finish_max(m_sc, rows):
    mp = m_sc[rows, :]
    m_sc[rows, :] = jnp.broadcast_to(jnp.max(mp, axis=1, keepdims=True), mp.shape)


def _accumulate(s, vblk, m_sc, l_sc, acc_sc, rows):
    m = m_sc[rows, :]
    lp = l_sc[rows, :]
    ps = []
    for j in range(s.shape[1] // LANES):
        pj = jnp.exp2(s[:, j * LANES:(j + 1) * LANES] - m)
        lp = lp + pj
        ps.append(pj.astype(BF16))
    l_sc[rows, :] = lp
    acc_sc[rows, :] = acc_sc[rows, :] + _bdot(jnp.concatenate(ps, axis=1), vblk)


WIDE = 4
BISECT_EXTRA = 6


def _two_pass(n_full, n_vis, score, values, m_sc, l_sc, acc_sc, sbuf, groups):
    tk = sbuf.shape[2]
    m_sc[...] = jnp.full(m_sc.shape, NEG_INF, F32)
    l_sc[...] = jnp.zeros(l_sc.shape, F32)
    acc_sc[...] = jnp.zeros(acc_sc.shape, F32)

    for g, rows in enumerate(groups):
        def keep(kb, nblk, masked):
            s = score(kb, nblk, masked, g)
            for w in range(nblk):
                sbuf[kb + w] = s[:, w * tk:(w + 1) * tk]
            _fold_max(s, m_sc, rows)

        def wide_body(j, c):
            keep(j * WIDE, WIDE, False)
            return c

        def full_body(kb, c):
            keep(kb, 1, False)
            return c

        def masked_body(kb, c):
            keep(kb, 1, True)
            return c

        n_wide = n_full // WIDE
        lax.fori_loop(0, n_wide, wide_body, 0)
        lax.fori_loop(n_wide * WIDE, n_full, full_body, 0)
        lax.fori_loop(n_full, n_vis, masked_body, 0)
        _finish_max(m_sc, rows)

        def acc_wide(j, c):
            kb = j * WIDE
            s = jnp.concatenate([sbuf[kb + w] for w in range(WIDE)], axis=1)
            _accumulate(s, values(kb, WIDE, g), m_sc, l_sc, acc_sc, rows)
            return c

        def acc_body(kb, c):
            _accumulate(sbuf[kb], values(kb, 1, g), m_sc, l_sc, acc_sc, rows)
            return c

        n_wide = n_vis // WIDE
        lax.fori_loop(0, n_wide, acc_wide, 0)
        lax.fori_loop(n_wide * WIDE, n_vis, acc_body, 0)
    return acc_sc[...] / jnp.sum(l_sc[...], axis=1, keepdims=True)


def _split_halves(q, lane=None):
    lane1 = lax.broadcasted_iota(jnp.int32, (1, LANES), 1)
    lo = jnp.where(lane1 < 64, 1.0, 0.0).astype(q.dtype)
    return q * lo, q * (1.0 - lo).astype(q.dtype)


def _diff_kernel(lqk_ref, sub_ref, q_ref, k_ref, v_ref, o_ref, m_sc, l_sc, acc_sc, sbuf,
                 *, tq, tk, nkb, q_pos0, lam_init, heads, nq):
    i = pl.program_id(2) if nq > 1 else 0
    r0 = q_pos0 + i * tq
    n_full, n_vis = _block_range(r0, tq, tk, nkb)
    lane = lax.broadcasted_iota(jnp.int32, (tq, LANES), 1)
    lq = lqk_ref[...]
    lam = (jnp.exp(jnp.sum(lq[0:1] * lq[1:2], axis=1, keepdims=True))
           - jnp.exp(jnp.sum(lq[2:3] * lq[3:4], axis=1, keepdims=True)) + lam_init)

    for h in range(heads):
        hl = slice(h * LANES, (h + 1) * LANES)
        qq = jnp.concatenate(_split_halves(q_ref[:, hl], lane), axis=0)

        def score(kb, nblk, masked, g, qq=qq, hl=hl):
            ks = pl.multiple_of(kb * tk, tk)
            s = _dot_nt(qq, k_ref[pl.ds(ks, nblk * tk), hl])
            if masked:
                vis = _visible(r0, ks, tq, nblk * tk)
                s = jnp.where(jnp.concatenate([vis, vis], axis=0), s, NEG_INF)
            return s

        def values(kb, nblk, g, hl=hl):
            return v_ref[pl.ds(pl.multiple_of(kb * tk, tk), nblk * tk), hl]

        o = _two_pass(n_full, n_vis, score, values, m_sc, l_sc, acc_sc, sbuf, (slice(0, 2 * tq),))
        o = o[:tq] - lam * o[tq:]
        o = o * lax.rsqrt(jnp.mean(o * o, axis=1, keepdims=True) + NORM_EPS)
        o_ref[:, hl] = (o * sub_ref[...] * (1.0 - lam_init)).astype(BF16)


def _diff_call(lqk, sub, q, k, v, *, nb, nq, tq, tk, lp, q_blk0, k_blk0, q_pos0, lam_init, heads=1):
    nkb = lp // tk
    kern = functools.partial(_diff_kernel, tq=tq, tk=tk, nkb=nkb, q_pos0=q_pos0, lam_init=lam_init,
                             heads=heads, nq=nq)
    hw = heads * LANES
    return pl.pallas_call(
        kern,
        grid=(nb, A_HEADS // heads, nq),
        in_specs=[pl.BlockSpec(lqk.shape, lambda b, h, i: (0, 0)),
                  pl.BlockSpec((1, LANES), lambda b, h, i: (0, 0)),
                  pl.BlockSpec((tq, hw), lambda b, h, i: (q_blk0 + b * nq + i, h)),
                  pl.BlockSpec((lp, hw), lambda b, h, i: (k_blk0 + b, h)),
                  pl.BlockSpec((lp, hw), lambda b, h, i: (k_blk0 + b, h))],
        out_specs=pl.BlockSpec((tq, hw), lambda b, h, i: (b * nq + i, h)),
        out_shape=jax.ShapeDtypeStruct((nb * nq * tq, A_WIDTH), BF16),
        scratch_shapes=([pltpu.VMEM((2 * tq, LANES), F32) for _ in range(3)]
                        + [pltpu.VMEM((nkb, 2 * tq, tk), F32)]),
        compiler_params=_cparams(("arbitrary", "arbitrary", "arbitrary")),
        name="diff_attention",
    )(lqk, sub, q, k, v)


def _mla_kernel(q_ref, k_ref, v_ref, o_ref, m_sc, l_sc, acc_sc, sbuf, *, tq, tk, nkb, q_pos0, heads, nq):
    i = pl.program_id(2) if nq > 1 else 0
    r0 = q_pos0 + i * tq
    n_full, n_vis = _block_range(r0, tq, tk, nkb)

    for h in range(heads):
        hl = slice(h * LANES, (h + 1) * LANES)
        q = q_ref[:, hl]

        def score(kb, nblk, masked, g, q=q, hl=hl):
            ks = pl.multiple_of(kb * tk, tk)
            s = _dot_nt(q, k_ref[pl.ds(ks, nblk * tk), hl])
            if masked:
                s = jnp.where(_visible(r0, ks, tq, nblk * tk), s, NEG_INF)
            return s

        def values(kb, nblk, g, hl=hl):
            return v_ref[pl.ds(pl.multiple_of(kb * tk, tk), nblk * tk), hl]

        o = _two_pass(n_full, n_vis, score, values, m_sc, l_sc, acc_sc, sbuf, (slice(0, tq),))
        o_ref[:, hl] = o.astype(BF16)


def _mla_call(q, k, v, *, nb, nq, tq, tk, lp, q_blk0, k_blk0, q_pos0, heads=1):
    nkb = lp // tk
    kern = functools.partial(_mla_kernel, tq=tq, tk=tk, nkb=nkb, q_pos0=q_pos0, heads=heads, nq=nq)
    hw = heads * LANES
    return pl.pallas_call(
        kern,
        grid=(nb, C_HEADS // heads, nq),
        in_specs=[pl.BlockSpec((tq, hw), lambda b, h, i: (q_blk0 + b * nq + i, h)),
                  pl.BlockSpec((lp, hw), lambda b, h, i: (k_blk0 + b, h)),
                  pl.BlockSpec((lp, hw), lambda b, h, i: (k_blk0 + b, h))],
        out_specs=pl.BlockSpec((tq, hw), lambda b, h, i: (b * nq + i, h)),
        out_shape=jax.ShapeDtypeStruct((nb * nq * tq, C_HEADS * LANES), BF16),
        scratch_shapes=([pltpu.VMEM((tq, LANES), F32) for _ in range(3)]
                        + [pltpu.VMEM((nkb, tq, tk), F32)]),
        compiler_params=_cparams(("arbitrary", "arbitrary", "arbitrary")),
        name="mla_attention",
    )(q, k, v)


def _dsa_kernel(q_ref, qi_ref, w_ref, k_ref, v_ref, ki_ref, o_ref,
                s_sc, m_sc, l_sc, acc_sc, t_sc, need_sc, carry_sc, wrep_sc, sbuf,
                *, tq, tk, nkb, q_pos0, n_keys, k_sel, rs, n_bisect, nq):
    i = pl.program_id(1) if nq > 1 else 0
    r0 = q_pos0 + i * tq
    n_full, n_vis = _block_range(r0, tq, tk, nkb)
    lane = lax.broadcasted_iota(jnp.int32, (tq, LANES), 1)
    ksel_f = float(k_sel)

    qi = qi_ref[...]
    qa, qb = _split_halves(qi[:, :LANES], lane)
    qc, qd = _split_halves(qi[:, LANES:], lane)
    qi4 = jnp.concatenate([qa, qb, qc, qd], axis=0)
    wm = w_ref[...]
    for h in range(IDX_HEADS):
        wcol = wm[:, _MISC_BIW + h:_MISC_BIW + h + 1] * (IDX_HEADS ** -0.5)
        wrep_sc[h] = jnp.broadcast_to(wcol, (tq, LANES))

    def score_step(kb, masked):
        ks = pl.multiple_of(kb * tk, tk)
        rel = _dot_nt(qi4, ki_ref[pl.ds(ks, tk), :])
        cols = []
        for j in range(tk // LANES):
            cl = slice(j * LANES, (j + 1) * LANES)
            sc = wrep_sc[0] * jnp.maximum(rel[:tq, cl], 0.0)
            for h in range(1, IDX_HEADS):
                sc = sc + wrep_sc[h] * jnp.maximum(rel[h * tq:(h + 1) * tq, cl], 0.0)
            cols.append(sc)
        sc = jnp.concatenate(cols, axis=1)
        if masked:
            sc = jnp.where(_visible(r0, ks, tq, tk), sc, NEG_INF)
        s_sc[kb] = sc
        mx = m_sc[0:tq, :]
        mn = l_sc[0:tq, :]
        for j in range(tk // LANES):
            sj = sc[:, j * LANES:(j + 1) * LANES]
            mx = jnp.maximum(mx, sj)
            mn = jnp.minimum(mn, jnp.where(sj > HALF_NEG, sj, BIG_POS) if masked else sj)
        m_sc[0:tq, :] = mx
        l_sc[0:tq, :] = mn

    def score_full(kb, c):
        score_step(kb, False)
        return c

    def score_masked(kb, c):
        score_step(kb, True)
        return c

    m_sc[0:tq, :] = jnp.full((tq, LANES), BELOW_NEG, F32)
    l_sc[0:tq, :] = jnp.full((tq, LANES), BIG_POS, F32)
    lax.fori_loop(0, n_full, score_full, 0)
    lax.fori_loop(n_full, n_vis, score_masked, 0)

    nl = tk // LANES

    def search(sb, flag):
        rsl = pl.ds(pl.multiple_of(sb * rs, rs), rs)

        def fold(fn, init):
            def body(kb, part):
                s = s_sc[kb, rsl, :]
                for j in range(nl):
                    part = fn(part, s[:, j * LANES:(j + 1) * LANES])
                return part
            return lax.fori_loop(0, n_vis, body, init)

        def bc(x):
            return jnp.broadcast_to(x, (rs, LANES))

        def count(cmp, x):
            xb = bc(x)
            part = fold(lambda p, sj: p + jnp.where(cmp(sj, xb), 1.0, 0.0), jnp.zeros((rs, LANES), F32))
            return jnp.sum(part, axis=1, keepdims=True)

        def max_below(x, strict):
            xb = bc(x)
            if strict:
                part = fold(lambda p, sj: jnp.maximum(p, jnp.where(sj < xb, sj, BELOW_NEG)),
                            jnp.full((rs, LANES), BELOW_NEG, F32))
            else:
                part = fold(lambda p, sj: jnp.maximum(p, jnp.where(sj <= xb, sj, BELOW_NEG)),
                            jnp.full((rs, LANES), BELOW_NEG, F32))
            return jnp.max(part, axis=1, keepdims=True)

        ge = lambda a, b: a >= b
        gt = lambda a, b: a > b

        hi = jnp.max(m_sc[rsl, :], axis=1, keepdims=True)
        lo = jnp.min(l_sc[rsl, :], axis=1, keepdims=True)
        rpos = r0 + sb * rs + lax.broadcasted_iota(jnp.int32, (rs, 1), 0)
        n_valid = jnp.minimum(((rpos >> 6) + 1) << 6, n_keys)
        small = n_valid < k_sel

        def bisect_step(c):
            lo_c, hi_c, glo_c, ghi_c = c
            mid = 0.5 * (lo_c + hi_c)
            g = count(ge, mid)
            up = g >= ksel_f
            return (jnp.where(up, mid, lo_c), jnp.where(up, hi_c, mid),
                    jnp.where(up, g, glo_c), jnp.where(up, ghi_c, g))

        state = (lo, hi, n_valid.astype(F32), jnp.zeros((rs, 1), F32))
        state = lax.fori_loop(0, n_bisect, lambda _, c: bisect_step(c), state)

        def more_cond(c):
            return jnp.logical_and(c[0] < BISECT_EXTRA, jnp.max(c[1][2] - c[1][3]) > 1.5)

        def more_body(c):
            return c[0] + 1, bisect_step(c[1])

        _, state = lax.while_loop(more_cond, more_body, (jnp.int32(0), state))
        hi = state[1]

        v0 = max_below(hi, False)
        g0 = count(ge, v0)
        done0 = jnp.where(jnp.logical_or(g0 >= ksel_f, small), 1.0, 0.0)

        def walk_cond(c):
            return jnp.min(c[2]) < 0.5

        def walk_body(c):
            t_c, g_c, done_c = c
            v = max_below(t_c, True)
            g = count(ge, v)
            keep = done_c > 0.5
            return (jnp.where(keep, t_c, v), jnp.where(keep, g_c, g),
                    jnp.where(jnp.logical_or(keep, g >= ksel_f), 1.0, 0.0))

        t, g_t, _ = lax.while_loop(walk_cond, walk_body, (v0, g0, done0))
        c_gt = count(gt, t)
        need = ksel_f - c_gt
        excess = jnp.logical_and(jnp.logical_and(g_t - c_gt > need, jnp.logical_not(small)), t > HALF_NEG)
        t_sc[rsl, :] = jnp.where(small, NEG_INF, t)
        need_sc[rsl, :] = need
        return jnp.maximum(flag, jnp.max(jnp.where(excess, 1.0, 0.0)))

    tie_flag = lax.fori_loop(0, tq // rs, search, jnp.float32(0.0))

    q = q_ref[...]
    q0a, q0b = _split_halves(q[:, :LANES], lane)
    q1a, q1b = _split_halves(q[:, LANES:], lane)
    qq = (jnp.concatenate([q0a, q0b], axis=0), jnp.concatenate([q1a, q1b], axis=0))
    carry_sc[...] = jnp.zeros(carry_sc.shape, F32)
    t_all = t_sc[...]

    def bias_step(kb, masked):
        ks = pl.multiple_of(kb * tk, tk)
        sc = s_sc[kb]
        vis = _visible(r0, ks, tq, tk) if masked else None

        def store(sel):
            if masked:
                sel = jnp.logical_and(sel, vis)
            s_sc[kb] = jnp.where(sel, 0.0, NEG_INF)

        def plain():
            store(sc >= t_all)

        def with_ties():
            eq = sc == t_all
            if masked:
                eq = jnp.logical_and(eq, vis)
            eqf = jnp.where(eq, 1.0, 0.0)
            upper = (lax.broadcasted_iota(jnp.int32, (tk, tk), 0)
                     < lax.broadcasted_iota(jnp.int32, (tk, tk), 1))
            before = _bdot(eqf.astype(BF16), jnp.where(upper, 1.0, 0.0).astype(BF16)) + carry_sc[...]
            carry_sc[...] = carry_sc[...] + jnp.sum(eqf, axis=1, keepdims=True)
            store(jnp.logical_or(sc > t_all, jnp.logical_and(eq, before < need_sc[...])))

        lax.cond(tie_flag > 0.5, with_ties, plain)

    def bias_full(kb, c):
        bias_step(kb, False)
        return c

    def bias_masked(kb, c):
        bias_step(kb, True)
        return c

    lax.fori_loop(0, n_full, bias_full, 0)
    lax.fori_loop(n_full, n_vis, bias_masked, 0)

    def score(kb, nblk, masked, g):
        ks = pl.multiple_of(kb * tk, tk)
        bias = jnp.concatenate([s_sc[kb + w] for w in range(nblk)], axis=1) if nblk > 1 else s_sc[kb]
        s = _dot_nt(qq[g], k_ref[pl.ds(ks, nblk * tk), g * LANES:(g + 1) * LANES])
        return s + jnp.concatenate([bias, bias], axis=0)

    def values(kb, nblk, g):
        return v_ref[pl.ds(pl.multiple_of(kb * tk, tk), nblk * tk), g * LANES:(g + 1) * LANES]

    o = _two_pass(n_full, n_vis, score, values, m_sc, l_sc, acc_sc, sbuf,
                  (slice(0, 2 * tq), slice(2 * tq, 4 * tq)))
    for pr in range(2):
        lo_h = o[(2 * pr) * tq:(2 * pr + 1) * tq]
        hi_h = o[(2 * pr + 1) * tq:(2 * pr + 2) * tq]
        o_ref[:, pr * LANES:(pr + 1) * LANES] = jnp.where(lane < 64, lo_h, hi_h).astype(BF16)


def _dsa_call(q, qi, w, k, v, ki, *, nb, nq, tq, tk, lp, q_blk0, k_blk0, q_pos0, n_keys, n_bisect):
    nkb = lp // tk
    k_sel = min(DSA_TOPK, n_keys // 4)
    rs = min(64, tq)
    kern = functools.partial(_dsa_kernel, tq=tq, tk=tk, nkb=nkb, q_pos0=q_pos0, n_keys=n_keys,
                             k_sel=k_sel, rs=rs, n_bisect=n_bisect, nq=nq)
    return pl.pallas_call(
        kern,
        grid=(nb, nq),
        in_specs=[pl.BlockSpec((tq, B_WIDTH), lambda b, i: (q_blk0 + b * nq + i, 0)),
                  pl.BlockSpec((tq, B_WIDTH), lambda b, i: (q_blk0 + b * nq + i, 0)),
                  pl.BlockSpec((tq, LANES), lambda b, i: (q_blk0 + b * nq + i, 0)),
                  pl.BlockSpec((lp, B_WIDTH), lambda b, i: (k_blk0 + b, 0)),
                  pl.BlockSpec((lp, B_WIDTH), lambda b, i: (k_blk0 + b, 0)),
                  pl.BlockSpec((lp, LANES), lambda b, i: (k_blk0 + b, 0))],
        out_specs=pl.BlockSpec((tq, B_WIDTH), lambda b, i: (b * nq + i, 0)),
        out_shape=jax.ShapeDtypeStruct((nb * nq * tq, B_WIDTH), BF16),
        scratch_shapes=[pltpu.VMEM((nkb, tq, tk), F32),
                        pltpu.VMEM((4 * tq, LANES), F32), pltpu.VMEM((4 * tq, LANES), F32),
                        pltpu.VMEM((4 * tq, LANES), F32),
                        pltpu.VMEM((tq, 1), F32), pltpu.VMEM((tq, 1), F32), pltpu.VMEM((tq, 1), F32),
                        pltpu.VMEM((IDX_HEADS, tq, LANES), F32),
                        pltpu.VMEM((nkb, 2 * tq, tk), F32)],
        compiler_params=_cparams(("arbitrary", "arbitrary")),
        name="dsa_attention",
    )(q, qi, w, k, v, ki)


def _kb_kernel(x_ref, oa_ref, ob_ref, oc_ref, sha_ref, sca_ref, gta_ref, shm_ref, scm_ref,
               nmix_ref, nffn_ref, wg_ref, wba_ref, wbb_ref, wbc_ref, wout_ref, rw_ref, rb_ref,
               x1_o, h2_o, e_o, g_o, cnt_o, cnt_sc, *, tm, ch):
    x = x_ref[...]
    hb = _modulate(_rms(x, nmix_ref[...]), sca_ref[...], sha_ref[...], tm, ch).astype(BF16)

    def gate(c0):
        return _sigmoid(_bdot(hb, wg_ref[:, c0:c0 + D_MODEL]))

    merged = gate(0) * _bdot(oa_ref[...], wba_ref[...])
    merged = merged + gate(D_MODEL) * _bdot(ob_ref[...], wbb_ref[...])
    merged = merged + gate(2 * D_MODEL) * _bdot(oc_ref[...], wbc_ref[...])
    y = _bdot(merged.astype(BF16), wout_ref[...])
    x1 = x + _scale_rows(y, gta_ref[...], tm, ch)
    x1_o[...] = x1
    h2 = _modulate(_rms(x1, nffn_ref[...]), scm_ref[...], shm_ref[...], tm, ch)
    h2_o[...] = h2
    lg = _dot_split(h2, rw_ref[...]) + rb_ref[...]
    lanef = lax.broadcasted_iota(jnp.int32, (tm, LANES), 1).astype(F32)
    e_acc = jnp.zeros((tm, LANES), F32)
    v_acc = jnp.full((tm, LANES), NEG_INF, F32)
    chosen = jnp.zeros((tm, LANES), F32)
    picks = []
    for k in range(TOP_K):
        mx = jnp.max(lg, axis=1, keepdims=True)
        idx = jnp.min(jnp.where(lg == mx, lanef, float(LANES)), axis=1, keepdims=True)
        hit = lanef == idx
        picks.append(hit)
        chosen = jnp.where(hit, 1.0, chosen)
        e_acc = jnp.where(lanef == float(k), idx, e_acc)
        v_acc = jnp.where(lanef == float(k), mx, v_acc)
        lg = jnp.where(hit, BELOW_NEG, lg)
    ex = jnp.where(lanef < float(TOP_K), jnp.exp(v_acc - jnp.max(v_acc, axis=1, keepdims=True)), 0.0)
    g_o[...] = ex / jnp.sum(ex, axis=1, keepdims=True)
    @pl.when(pl.program_id(0) == 0)
    def _():
        cnt_sc[...] = jnp.zeros(cnt_sc.shape, F32)

    earlier = (lax.broadcasted_iota(jnp.int32, (tm, tm), 1) < lax.broadcasted_iota(jnp.int32, (tm, tm), 0))
    before = _bdot(jnp.where(earlier, 1.0, 0.0).astype(BF16), chosen.astype(BF16)) + cnt_sc[0:1, :]
    for k in range(TOP_K):
        rank = jnp.sum(jnp.where(picks[k], before, 0.0), axis=1, keepdims=True)
        e_acc = jnp.where(lanef == float(TOP_K + k), rank, e_acc)
    e_o[...] = e_acc.astype(jnp.int32)
    cnt_sc[...] = cnt_sc[...] + jnp.sum(chosen, axis=0, keepdims=True)
    cnt_o[...] = cnt_sc[...]


def _kb_call(x, oa, ob, oc, mods, nmix, nffn, wg, wba, wbb, wbc, wout, rw, rb):
    n = x.shape[0]
    tm = TOKEN_BLOCK
    ch = tm // MOD_ROWS

    def row(i):
        return (i, 0)

    def full(a):
        return pl.BlockSpec(a.shape, lambda i: (0, 0))

    in_specs = ([pl.BlockSpec((tm, D_MODEL), row), pl.BlockSpec((tm, A_WIDTH), row),
                 pl.BlockSpec((tm, B_WIDTH), row), pl.BlockSpec((tm, 512), row)]
                + [pl.BlockSpec((MOD_ROWS, D_MODEL), row) for _ in range(5)]
                + [full(a) for a in (nmix, nffn, wg, wba, wbb, wbc, wout, rw, rb)])
    out_shape = [jax.ShapeDtypeStruct((n, D_MODEL), F32), jax.ShapeDtypeStruct((n, D_MODEL), F32),
                 jax.ShapeDtypeStruct((n, LANES), jnp.int32), jax.ShapeDtypeStruct((n, LANES), F32),
                 jax.ShapeDtypeStruct((8, LANES), F32)]
    out_specs = [pl.BlockSpec((tm, D_MODEL), row), pl.BlockSpec((tm, D_MODEL), row),
                 pl.BlockSpec((tm, LANES), row), pl.BlockSpec((tm, LANES), row),
                 pl.BlockSpec((8, LANES), lambda i: (0, 0))]
    return pl.pallas_call(
        functools.partial(_kb_kernel, tm=tm, ch=ch),
        grid=(n // tm,),
        in_specs=in_specs,
        out_specs=out_specs,
        out_shape=out_shape,
        scratch_shapes=[pltpu.VMEM((8, LANES), F32)],
        compiler_params=_cparams(("arbitrary",)),
        name="post_attention",
    )(x, oa, ob, oc, *mods, nmix, nffn, wg, wba, wbb, wbc, wout, rw, rb)


def _ke_kernel(be_ref, nu_ref, x_ref, wg_ref, wl_ref, wd_ref, bg_ref, bl_ref, bd_ref, y_ref):
    i = pl.program_id(0)

    @pl.when(i < nu_ref[0])
    def _():
        x = x_ref[...].astype(BF16)
        g = jnp.minimum(_bdot(x, wg_ref[0]) + bg_ref[0], SWIGLU_LIMIT)
        l = jnp.clip(_bdot(x, wl_ref[0]) + bl_ref[0], -SWIGLU_LIMIT, SWIGLU_LIMIT)
        act = g * _sigmoid(SWIGLU_ALPHA * g) * (l + 1.0)
        y_ref[...] = _bdot(act.astype(BF16), wd_ref[0]) + bd_ref[0]

    @pl.when(i >= nu_ref[0])
    def _():
        y_ref[...] = jnp.zeros(y_ref.shape, F32)


def _ke_call(blk_e, n_used, xg, wg, wl, wd, bg, bl, bd):
    n_rows = xg.shape[0]
    eb = EXPERT_ROWS
    n_blocks = n_rows // eb
    grid_spec = pltpu.PrefetchScalarGridSpec(
        num_scalar_prefetch=2,
        grid=(n_blocks,),
        in_specs=[pl.BlockSpec((eb, D_MODEL), lambda i, be, nu: (i, 0)),
                  pl.BlockSpec((1, D_MODEL, D_FF), lambda i, be, nu: (be[i], 0, 0)),
                  pl.BlockSpec((1, D_MODEL, D_FF), lambda i, be, nu: (be[i], 0, 0)),
                  pl.BlockSpec((1, D_FF, D_MODEL), lambda i, be, nu: (be[i], 0, 0)),
                  pl.BlockSpec((1, 1, D_FF), lambda i, be, nu: (be[i], 0, 0)),
                  pl.BlockSpec((1, 1, D_FF), lambda i, be, nu: (be[i], 0, 0)),
                  pl.BlockSpec((1, 1, D_MODEL), lambda i, be, nu: (be[i], 0, 0))],
        out_specs=pl.BlockSpec((eb, D_MODEL), lambda i, be, nu: (i, 0)),
    )
    return pl.pallas_call(
        _ke_kernel,
        grid_spec=grid_spec,
        out_shape=jax.ShapeDtypeStruct((n_rows, D_MODEL), F32),
        compiler_params=_cparams(("arbitrary",)),
        name="moe_experts",
    )(blk_e, n_used, xg, wg, wl, wd, bg, bl, bd)


def _kc_kernel(dcur_ref, dnext_ref, x1_ref, yr_hbm, g_ref, gtm_ref, fn_ref, x2_o, y_o, buf, sem,
               *, tm, ch):
    i = pl.program_id(0)
    nblk = pl.num_programs(0)
    slot = lax.rem(i, 2)
    nrow = TOP_K * tm

    def row_copy(d_ref, s, r):
        return pltpu.make_async_copy(yr_hbm.at[pl.ds(d_ref[0, 0, r], 1), :],
                                     buf.at[s, pl.ds(r, 1), :], sem.at[s])

    def issue(d_ref, s):
        def body(r, c):
            row_copy(d_ref, s, r).start()
            return c
        lax.fori_loop(0, nrow, body, 0, unroll=8)

    @pl.when(i == 0)
    def _():
        issue(dcur_ref, 0)

    @pl.when(i + 1 < nblk)
    def _():
        issue(dnext_ref, 1 - slot)

    pltpu.make_async_copy(buf.at[slot], buf.at[slot], sem.at[slot]).wait()
    gate = g_ref[...]
    ffn = gate[:, 0:1] * buf[slot, 0:tm, :]
    for k in range(1, TOP_K):
        ffn = ffn + gate[:, k:k + 1] * buf[slot, k * tm:(k + 1) * tm, :]
    x2 = x1_ref[...] + _scale_rows(ffn, gtm_ref[...], tm, ch)
    x2_o[...] = x2
    y_o[...] = _rms(x2, fn_ref[...])


def _kc_call(x1, yr, dest_blocks, gate, gtm, fnorm):
    n = x1.shape[0]
    tm = TOKEN_BLOCK
    ch = tm // MOD_ROWS
    nblk = n // tm
    smem = functools.partial(pl.BlockSpec, (1, 1, TOP_K * tm), memory_space=pltpu.SMEM)
    return pl.pallas_call(
        functools.partial(_kc_kernel, tm=tm, ch=ch),
        grid=(nblk,),
        in_specs=[smem(lambda i: (i, 0, 0)),
                  smem(lambda i: (jnp.minimum(i + 1, nblk - 1), 0, 0)),
                  pl.BlockSpec((tm, D_MODEL), lambda i: (i, 0)),
                  pl.BlockSpec(memory_space=pl.ANY),
                  pl.BlockSpec((tm, LANES), lambda i: (i, 0)),
                  pl.BlockSpec((MOD_ROWS, D_MODEL), lambda i: (i, 0)),
                  pl.BlockSpec((1, D_MODEL), lambda i: (0, 0))],
        out_specs=[pl.BlockSpec((tm, D_MODEL), lambda i: (i, 0)), pl.BlockSpec((tm, D_MODEL), lambda i: (i, 0))],
        out_shape=[jax.ShapeDtypeStruct((n, D_MODEL), F32), jax.ShapeDtypeStruct((n, D_MODEL), F32)],
        scratch_shapes=[pltpu.VMEM((2, TOP_K * tm, D_MODEL), F32), pltpu.SemaphoreType.DMA((2,))],
        compiler_params=_cparams(("arbitrary",)),
        name="moe_combine",
    )(dest_blocks, dest_blocks, x1, yr, gate, gtm, fnorm)


def _rope_tables(pos):
    lane = np.arange(LANES)
    inv32 = ROPE_THETA ** (-jnp.arange(32, dtype=F32) / 32)
    inv16 = ROPE_THETA ** (-jnp.arange(16, dtype=F32) / 16)
    ang64 = pos[:, None] * inv32[None, :][:, lane & 31]
    ang32 = pos[:, None] * inv16[None, :][:, lane & 15]
    sign64 = jnp.asarray(np.where((lane & 63) < 32, -1.0, 1.0), F32)[None, :]
    sign32 = jnp.asarray(np.where((lane & 31) < 16, -1.0, 1.0), F32)[None, :]
    in_m = jnp.asarray(lane < C_ROPE)[None, :]
    in_q = jnp.asarray((lane >= C_NOPE) & (lane < C_NOPE + C_ROPE))[None, :]
    cos64, sin64 = jnp.cos(ang64), jnp.sin(ang64) * sign64
    cos32, sin32 = jnp.cos(ang32), jnp.sin(ang32) * sign32
    return (cos64, sin64,
            jnp.where(in_m, cos32, 1.0), jnp.where(in_m, sin32, 0.0),
            jnp.where(in_q, cos32, 1.0), jnp.where(in_q, sin32, 0.0))


def _layer_weights(l, w_in, mla_w_uq, mla_w_ukv, w_br_c, router_w, router_b, exp_w_gu, exp_b_gu):
    wi = w_in[l]
    z = lambda n: jnp.zeros((D_MODEL, n), F32)
    w1 = jnp.concatenate([
        wi[:, _OFF_AQ:_OFF_BIK],
        wi[:, _OFF_BIK:_OFF_BIW], wi[:, _OFF_BIK:_OFF_BIW],
        wi[:, _OFF_CKR:_OFF_GATES], wi[:, _OFF_BIW:_OFF_CQ], z(LANES - C_ROPE - IDX_HEADS),
        wi[:, _OFF_CQ:_OFF_CKV], wi[:, _OFF_CKV:_OFF_CKR]], axis=1).astype(BF16)
    wg = wi[:, _OFF_GATES:].astype(BF16)
    wuq = mla_w_uq[l].reshape(Q_LORA, C_HEADS, C_NOPE + C_ROPE)
    wuq = jnp.pad(wuq, ((0, 0), (0, 0), (0, LANES - C_NOPE - C_ROPE))).reshape(Q_LORA, C_HEADS * LANES).astype(BF16)
    wukv = mla_w_ukv[l].reshape(KV_LORA, C_HEADS, C_NOPE + C_V)
    wk = jnp.pad(wukv[:, :, :C_NOPE], ((0, 0), (0, 0), (0, LANES - C_NOPE))).reshape(KV_LORA, C_HEADS * LANES).astype(BF16)
    wv = jnp.pad(wukv[:, :, C_NOPE:], ((0, 0), (0, 0), (0, LANES - C_V))).reshape(KV_LORA, C_HEADS * LANES).astype(BF16)
    e = np.zeros((LANES, C_HEADS * LANES), np.float32)
    for h in range(C_HEADS):
        e[np.arange(C_ROPE), h * LANES + C_NOPE + np.arange(C_ROPE)] = 1.0
    we = jnp.asarray(e, BF16)
    wbc = jnp.pad(w_br_c[l].reshape(C_HEADS, C_V, D_MODEL), ((0, 0), (0, LANES - C_V), (0, 0)))
    wbc = wbc.reshape(C_HEADS * LANES, D_MODEL).astype(BF16)
    rw = jnp.pad(router_w[l], ((0, 0), (0, LANES - N_EXPERTS)))
    rb = jnp.pad(router_b[l], (0, LANES - N_EXPERTS), constant_values=NEG_INF).reshape(1, LANES)
    wgl = exp_w_gu[l].reshape(N_EXPERTS, D_MODEL, D_FF, 2)
    bgl = exp_b_gu[l].reshape(N_EXPERTS, 1, D_FF, 2)
    return dict(w1=w1, wg=wg, wuq=wuq, wk=wk, wv=wv, we=we, wbc=wbc, rw=rw, rb=rb,
                e_wg=wgl[..., 0].astype(BF16), e_wl=wgl[..., 1].astype(BF16),
                e_bg=bgl[..., 0], e_bl=bgl[..., 1])


def _route(e_pad, cnt, n_rows_pad):
    eb = EXPERT_ROWS
    flat_e = e_pad[:, :TOP_K].reshape(-1)
    rank = e_pad[:, TOP_K:2 * TOP_K].reshape(-1)
    nk = flat_e.shape[0]
    counts = cnt[0, :N_EXPERTS].astype(jnp.int32)
    padded = (counts + eb - 1) // eb * eb
    pad_end = jnp.cumsum(padded)
    pad_start = pad_end - padded
    dest = pad_start[flat_e] + rank
    row_tok = jnp.zeros((n_rows_pad,), jnp.int32).at[dest].set(jnp.arange(nk, dtype=jnp.int32) // TOP_K)
    n_blocks = n_rows_pad // eb
    blk_start = jnp.arange(n_blocks, dtype=jnp.int32) * eb
    blk_e = jnp.sum((pad_end[None, :] <= blk_start[:, None]).astype(jnp.int32), axis=1)
    blk_e = jnp.minimum(blk_e, N_EXPERTS - 1)
    n_used = (pad_end[-1] // eb).astype(jnp.int32).reshape(1)
    return dest, row_tok, blk_e, n_used


def kernel(x_prompt, x_sample, cache_a_k, cache_a_v, cache_b_k, cache_b_v, cache_b_idx_k, cache_c_latent, cache_c_k_rope, c_prompt, c_sample, w_ada, b_ada, norm_mix, norm_ffn, w_in, diff_lq1, diff_lk1, diff_lq2, diff_lk2, diff_subln, mla_q_norm, mla_w_uq, mla_kv_norm, mla_w_ukv, w_br_a, w_br_b, w_br_c, w_out, router_w, router_b, exp_w_gu, exp_b_gu, exp_w_down, exp_b_down, final_norm):
    depth = w_ada.shape[0]
    bp, tp, _ = x_prompt.shape
    bs, ts, _ = x_sample.shape
    past = cache_c_latent.shape[2]
    n_p, n_s = bp * tp, bs * ts
    n = n_p + n_s
    tm = TOKEN_BLOCK
    ch = tm // MOD_ROWS
    assert ts == CHUNK and tp % tm == 0 and n_s % tm == 0 and past % CHUNK == 0

    x = jnp.concatenate([x_prompt.reshape(n_p, D_MODEL), x_sample.reshape(n_s, D_MODEL)], axis=0)

    n_seq = bp + bs
    c_all = jnp.concatenate([c_prompt, c_sample], axis=0)
    c_pad = jnp.pad(c_all, ((0, (-n_seq) % 8), (0, 0)))
    mod = _ada_call(c_pad, w_ada, b_ada)
    def per_chunk(m, reps):
        return jnp.broadcast_to(m[:, :, None, :], m.shape[:2] + (reps, m.shape[2])).reshape(depth, -1, m.shape[2])

    mod_rows = jnp.concatenate([per_chunk(mod[:, :bp], tp // ch), per_chunk(mod[:, bp:n_seq], ts // ch)],
                               axis=1)

    pos = jnp.concatenate([jnp.arange(tp, dtype=F32),
                           jnp.tile(past + jnp.arange(ts, dtype=F32), tm // ts)])
    tabs = _rope_tables(pos)

    tq_p = min(256, tp)
    tk_p = min(512, tp)
    nq_p = tp // tq_p
    tq_c = min(512, tp)
    l_s = past + ts
    tk_s = 384
    lp_s = -(-l_s // tk_s) * tk_s
    n_rows_pad = -(-(n * TOP_K + N_EXPERTS * (EXPERT_ROWS - 1)) // EXPERT_ROWS) * EXPERT_ROWS

    def with_cache(cache_l, new, width):
        parts = [cache_l.reshape(bs, past, width).astype(BF16), new.reshape(bs, ts, width)]
        if lp_s > l_s:
            parts.append(jnp.zeros((bs, lp_s - l_s, width), BF16))
        return jnp.concatenate(parts, axis=1).reshape(bs * lp_s, width)

    caches = [[] for _ in range(7)]
    y = None
    for l in range(depth):
        lam_init = 0.8 - 0.6 * math.exp(-0.3 * l)
        wl = _layer_weights(l, w_in, mla_w_uq, mla_w_ukv, w_br_c, router_w, router_b, exp_w_gu, exp_b_gu)
        m6 = [mod_rows[l, :, j * D_MODEL:(j + 1) * D_MODEL] for j in range(6)]
        sh_a, sc_a, gt_a, sh_m, sc_m, gt_m = m6
        nmix = norm_mix[l].reshape(1, D_MODEL)
        nffn = norm_ffn[l].reshape(1, D_MODEL)

        (ak, av, bk, bv, bik, clat, ckr, misc,
         aq_b, ak_b, av_b, bq_b, bk_b, bv_b, biq_b, bik2_b, cq_b, ck_b, cv_b) = _ka_call(
            x, sh_a, sc_a, tabs, nmix, wl['w1'], mla_q_norm[l].reshape(1, Q_LORA), wl['wuq'],
            mla_kv_norm[l].reshape(1, KV_LORA), wl['wk'], wl['we'], wl['wv'],
            n_prompt_blocks=n_p // tm, tab_blocks=tp // tm)
        for i, a in enumerate((ak, av, bk, bv, bik, clat, ckr)):
            caches[i].append(a)

        lqk = jnp.pad(jnp.stack([diff_lq1[l], diff_lk1[l], diff_lq2[l], diff_lk2[l]]),
                      ((0, 4), (0, LANES - A_HD)))
        sub = diff_subln[l].reshape(1, 2 * A_HD)
        oa_p = _diff_call(lqk, sub, aq_b, ak_b, av_b, nb=bp, nq=nq_p, tq=tq_p, tk=tk_p, lp=tp,
                          q_blk0=0, k_blk0=0, q_pos0=0, lam_init=lam_init)
        ak_s = with_cache(cache_a_k[l], ak_b[n_p:], A_WIDTH)
        av_s = with_cache(cache_a_v[l], av_b[n_p:], A_WIDTH)
        oa_s = _diff_call(lqk, sub, aq_b, ak_s, av_s, nb=bs, nq=1, tq=ts, tk=tk_s, lp=lp_s,
                          q_blk0=n_p // ts, k_blk0=0, q_pos0=past, lam_init=lam_init, heads=A_HEADS)
        ob_p = _dsa_call(bq_b, biq_b, misc, bk_b, bv_b, bik2_b, nb=bp, nq=nq_p, tq=tq_p, tk=tk_p, lp=tp,
                         q_blk0=0, k_blk0=0, q_pos0=0, n_keys=tp, n_bisect=12)
        bk_s = with_cache(cache_b_k[l], bk_b[n_p:], B_WIDTH)
        bv_s = with_cache(cache_b_v[l], bv_b[n_p:], B_WIDTH)
        cik = cache_b_idx_k[l].reshape(bs, past, IDX_DIM)
        bik_s = with_cache(jnp.concatenate([cik, cik], axis=-1), bik2_b[n_p:], LANES)
        ob_s = _dsa_call(bq_b, biq_b, misc, bk_s, bv_s, bik_s, nb=bs, nq=1, tq=ts, tk=tk_s, lp=lp_s,
                         q_blk0=n_p // ts, k_blk0=0, q_pos0=past, n_keys=l_s, n_bisect=10)
        oc_p = _mla_call(cq_b, ck_b, cv_b, nb=bp, nq=tp // tq_c, tq=tq_c, tk=tk_p, lp=tp,
                         q_blk0=0, k_blk0=0, q_pos0=0)
        lat_c = cache_c_latent[l].reshape(bs * past, KV_LORA)
        kr_c = jnp.pad(cache_c_k_rope[l].reshape(bs * past, C_ROPE), ((0, 0), (0, LANES - C_ROPE)))
        ck_c, cv_c = _mla_kv_call(lat_c, kr_c, wl['wk'], wl['we'], wl['wv'])
        ck_s = with_cache(ck_c, ck_b[n_p:], 512)
        cv_s = with_cache(cv_c, cv_b[n_p:], 512)
        oc_s = _mla_call(cq_b, ck_s, cv_s, nb=bs, nq=1, tq=ts, tk=tk_s, lp=lp_s,
                         q_blk0=n_p // ts, k_blk0=0, q_pos0=past, heads=C_HEADS)

        oa = jnp.concatenate([oa_p, oa_s], axis=0)
        ob = jnp.concatenate([ob_p, ob_s], axis=0)
        oc = jnp.concatenate([oc_p, oc_s], axis=0)

        x1, h2, e_pad, g_pad, cnt = _kb_call(
            x, oa, ob, oc, (sh_a, sc_a, gt_a, sh_m, sc_m), nmix, nffn, wl['wg'],
            w_br_a[l].astype(BF16), w_br_b[l].astype(BF16), wl['wbc'], w_out[l].astype(BF16),
            wl['rw'], wl['rb'])

        dest, row_tok, blk_e, n_used = _route(e_pad, cnt, n_rows_pad)
        xg = jnp.take(h2, row_tok, axis=0)
        yr = _ke_call(blk_e, n_used, xg, wl['e_wg'], wl['e_wl'], exp_w_down[l].astype(BF16),
                      wl['e_bg'], wl['e_bl'], exp_b_down[l].reshape(N_EXPERTS, 1, D_MODEL))
        dest_blocks = dest.reshape(n // tm, tm, TOP_K).transpose(0, 2, 1).reshape(n // tm, 1, TOP_K * tm)
        x, y = _kc_call(x1, yr, dest_blocks, g_pad, gt_m, final_norm.reshape(1, D_MODEL))

    def split(a, tail):
        a = jnp.stack(a, axis=0)
        return (a[:, :n_p].reshape((depth, bp, tp) + tail), a[:, n_p:].reshape((depth, bs, ts) + tail))

    tails = ((A_HEADS, 2 * A_HD), (A_HEADS, 2 * A_HD), (B_HEADS, B_HD), (B_HEADS, B_HD),
             (IDX_DIM,), (KV_LORA,), (C_ROPE,))
    ps = [split(c, t) for c, t in zip(caches, tails)]
    y_prompt = y[:n_p].reshape(bp, tp, D_MODEL)
    y_sample = y[n_p:].reshape(bs, ts, D_MODEL)
    return (y_prompt, y_sample) + tuple(p[0] for p in ps) + tuple(p[1] for p in ps)
```

```python
import functools
import math

import numpy as np
import jax
import jax.numpy as jnp
from jax import lax
from jax.experimental import pallas as pl
from jax.experimental.pallas import tpu as pltpu

F32 = jnp.float32
BF16 = jnp.bfloat16

D_MODEL = 1024
CHUNK = 64
ROPE_THETA = 10000.0
NORM_EPS = 1e-6
NEG_INF = -1e30
HALF_NEG = -5e29
BELOW_NEG = -3e38
BIG_POS = 3e38
LOG2E = 1.4426950408889634

A_HEADS, A_HD = 4, 64
B_HEADS, B_HD = 4, 64
IDX_HEADS, IDX_DIM = 4, 64
DSA_TOPK = 256
C_HEADS, C_NOPE, C_ROPE, C_V = 4, 64, 32, 64
Q_LORA, KV_LORA = 256, 128
N_EXPERTS, TOP_K = 32, 4
D_FF = D_MODEL
SWIGLU_LIMIT = 7.0
SWIGLU_ALPHA = 1.702

A_WIDTH = A_HEADS * 2 * A_HD
B_WIDTH = B_HEADS * B_HD
LANES = 128

_OFF_AQ, _OFF_AK, _OFF_AV = 0, 512, 1024
_OFF_BQ, _OFF_BK, _OFF_BV = 1536, 1792, 2048
_OFF_BIQ, _OFF_BIK, _OFF_BIW = 2304, 2560, 2624
_OFF_CQ, _OFF_CKV, _OFF_CKR = 2628, 2884, 3012
_OFF_GATES = 3044
_P_AQ, _P_AK, _P_AV = 0, 512, 1024
_P_BQ, _P_BK, _P_BV = 1536, 1792, 2048
_P_BIQ, _P_BIK2, _P_MISC, _P_CQ, _P_CKV, _P_END = 2304, 2560, 2688, 2816, 3072, 3200
_MISC_BIW = 32

TOKEN_BLOCK = 512
MOD_ROWS = 8
EXPERT_ROWS = 256
VMEM_LIMIT = 56 * 1024 * 1024


def _cparams(sem, vmem=VMEM_LIMIT):
    return pltpu.CompilerParams(dimension_semantics=sem, vmem_limit_bytes=vmem)


def _rms(xf, g):
    return xf * lax.rsqrt(jnp.mean(xf * xf, axis=-1, keepdims=True) + NORM_EPS) * g


def _sigmoid(x):
    return 1.0 / (1.0 + jnp.exp(-x))


def _bdot(a, b):
    return jnp.dot(a, b, preferred_element_type=F32)


def _dot_nt(a, b):
    return lax.dot_general(a, b, (((1,), (1,)), ((), ())), preferred_element_type=F32)


def _dot_split(a, b):
    a_hi = a.astype(BF16)
    b_hi = b.astype(BF16)
    a_lo = (a - a_hi.astype(F32)).astype(BF16)
    b_lo = (b - b_hi.astype(F32)).astype(BF16)
    return _bdot(a_hi, b_hi) + (_bdot(a_hi, b_lo) + _bdot(a_lo, b_hi))


def _modulate(xn, sc, sh, rows, ch):
    n = rows // ch
    y = xn.reshape(n, ch, D_MODEL) * (1.0 + sc)[:, None, :] + sh[:, None, :]
    return y.reshape(rows, D_MODEL)


def _scale_rows(y, g, rows, ch):
    n = rows // ch
    return (y.reshape(n, ch, D_MODEL) * g[:, None, :]).reshape(rows, D_MODEL)


def _ada_kernel(c_ref, w_ref, b_ref, o_ref):
    c = c_ref[...]
    s = c * _sigmoid(c)
    o_ref[0] = _dot_split(s, w_ref[0]) + b_ref[0]


def _ada_call(c_pad, w_ada, b_ada):
    depth = w_ada.shape[0]
    mp = c_pad.shape[0]
    return pl.pallas_call(
        _ada_kernel,
        grid=(depth, 6),
        in_specs=[
            pl.BlockSpec((mp, D_MODEL), lambda l, j: (0, 0)),
            pl.BlockSpec((1, D_MODEL, D_MODEL), lambda l, j: (l, 0, j)),
            pl.BlockSpec((1, 1, D_MODEL), lambda l, j: (l, 0, j)),
        ],
        out_specs=pl.BlockSpec((1, mp, D_MODEL), lambda l, j: (l, 0, j)),
        out_shape=jax.ShapeDtypeStruct((depth, mp, 6 * D_MODEL), F32),
        compiler_params=_cparams(("arbitrary", "arbitrary")),
        name="ada_mod",
    )(c_pad, w_ada, b_ada.reshape(depth, 1, 6 * D_MODEL))


def _rope_partner(x, lane, half):
    first = (lane & (2 * half - 1)) < half
    return jnp.where(first, pltpu.roll(x, LANES - half, 1), pltpu.roll(x, half, 1))


def _ka_kernel(x_ref, sh_ref, sc_ref, cos_ref, sin_ref, cosm_ref, sinm_ref, cosq_ref, sinq_ref,
               nmix_ref, w1_ref, qn_ref, wuq_ref, kvn_ref, wk_ref, we_ref, wv_ref,
               ak_o, av_o, bk_o, bv_o, bik_o, clat_o, ckr_o, misc_o,
               aq_b, ak_b, av_b, bq_b, bk_b, bv_b, biq_b, bik2_b, cq_b, ck_b, cv_b, *, tm, ch):
    x = x_ref[...]
    h = _modulate(_rms(x, nmix_ref[...]), sc_ref[...], sh_ref[...], tm, ch)
    hb = h.astype(BF16)
    lane = lax.broadcasted_iota(jnp.int32, (tm, LANES), 1)
    cos = cos_ref[...]
    sin = sin_ref[...]

    def proj(c0, c1):
        return _bdot(hb, w1_ref[:, c0:c1])

    def rope64(xb):
        return xb * cos + _rope_partner(xb, lane, 32) * sin

    a_scale = (A_HD ** -0.5) * LOG2E
    b_scale = (B_HD ** -0.5) * LOG2E
    i_scale = IDX_DIM ** -0.5
    c_scale = ((C_NOPE + C_ROPE) ** -0.5) * LOG2E

    p = proj(_P_AQ, _P_AK)
    for c in range(A_WIDTH // LANES):
        sl = slice(c * LANES, (c + 1) * LANES)
        aq_b[:, sl] = (rope64(p[:, sl]) * a_scale).astype(BF16)
    p = proj(_P_AK, _P_AV)
    for c in range(A_WIDTH // LANES):
        sl = slice(c * LANES, (c + 1) * LANES)
        r = rope64(p[:, sl])
        ak_o[:, sl] = r
        ak_b[:, sl] = r.astype(BF16)
    p = proj(_P_AV, _P_BQ)
    av_o[...] = p
    av_b[...] = p.astype(BF16)
    p = proj(_P_BQ, _P_BK)
    for c in range(B_WIDTH // LANES):
        sl = slice(c * LANES, (c + 1) * LANES)
        bq_b[:, sl] = (rope64(p[:, sl]) * b_scale).astype(BF16)
    p = proj(_P_BK, _P_BV)
    for c in range(B_WIDTH // LANES):
        sl = slice(c * LANES, (c + 1) * LANES)
        r = rope64(p[:, sl])
        bk_o[:, sl] = r
        bk_b[:, sl] = r.astype(BF16)
    p = proj(_P_BV, _P_BIQ)
    bv_o[...] = p
    bv_b[...] = p.astype(BF16)
    p = proj(_P_BIQ, _P_BIK2)
    for c in range(B_WIDTH // LANES):
        sl = slice(c * LANES, (c + 1) * LANES)
        biq_b[:, sl] = (rope64(p[:, sl]) * i_scale).astype(BF16)
    r = rope64(proj(_P_BIK2, _P_MISC))
    bik_o[...] = r[:, :IDX_DIM]
    bik2_b[...] = r.astype(BF16)
    pm = proj(_P_MISC, _P_CQ)
    misc = pm * cosm_ref[...] + _rope_partner(pm, lane, 16) * sinm_ref[...]
    misc_o[...] = misc
    ckr_o[...] = misc[:, :C_ROPE]
    qlat = _rms(proj(_P_CQ, _P_CKV), qn_ref[...]).astype(BF16)
    cqf = _bdot(qlat, wuq_ref[...])
    cosq = cosq_ref[...]
    sinq = sinq_ref[...]
    for c in range(C_HEADS):
        sl = slice(c * LANES, (c + 1) * LANES)
        xb = cqf[:, sl]
        cq_b[:, sl] = ((xb * cosq + _rope_partner(xb, lane, 16) * sinq) * c_scale).astype(BF16)
    clat = _rms(proj(_P_CKV, _P_END), kvn_ref[...])
    clat_o[...] = clat
    clb = clat.astype(BF16)
    ck_b[...] = (_bdot(clb, wk_ref[...]) + _bdot(misc.astype(BF16), we_ref[...])).astype(BF16)
    cv_b[...] = _bdot(clb, wv_ref[...]).astype(BF16)


def _ka_call(x, sh, sc, tabs, nmix, w1, qn, wuq, kvn, wk, we, wv, *, n_prompt_blocks, tab_blocks):
    n = x.shape[0]
    tm = TOKEN_BLOCK
    ch = tm // MOD_ROWS
    nblk = n // tm

    def row(i):
        return (i, 0)

    def tab(i):
        return (jnp.where(i < n_prompt_blocks, i % tab_blocks, tab_blocks), 0)

    def const(i):
        return (0, 0)

    def full(a):
        return pl.BlockSpec(a.shape, const)

    widths_f32 = (A_WIDTH, A_WIDTH, B_WIDTH, B_WIDTH, IDX_DIM, KV_LORA, C_ROPE, LANES)
    widths_b16 = (A_WIDTH, A_WIDTH, A_WIDTH, B_WIDTH, B_WIDTH, B_WIDTH, B_WIDTH, LANES, 512, 512, 512)
    out_shape = ([jax.ShapeDtypeStruct((n, w), F32) for w in widths_f32]
                 + [jax.ShapeDtypeStruct((n, w), BF16) for w in widths_b16])
    out_specs = [pl.BlockSpec((tm, w), row) for w in widths_f32 + widths_b16]
    in_specs = ([pl.BlockSpec((tm, D_MODEL), row),
                 pl.BlockSpec((MOD_ROWS, D_MODEL), row),
                 pl.BlockSpec((MOD_ROWS, D_MODEL), row)]
                + [pl.BlockSpec((tm, LANES), tab) for _ in range(6)]
                + [full(a) for a in (nmix, w1, qn, wuq, kvn, wk, we, wv)])
    return pl.pallas_call(
        functools.partial(_ka_kernel, tm=tm, ch=ch),
        grid=(nblk,),
        in_specs=in_specs,
        out_specs=out_specs,
        out_shape=out_shape,
        compiler_params=_cparams(("arbitrary",)),
        name="pre_attention",
    )(x, sh, sc, *tabs, nmix, w1, qn, wuq, kvn, wk, we, wv)


def _mla_kv_kernel(lat_ref, kr_ref, wk_ref, we_ref, wv_ref, ck_o, cv_o):
    lb = lat_ref[...].astype(BF16)
    ck_o[...] = (_bdot(lb, wk_ref[...]) + _bdot(kr_ref[...].astype(BF16), we_ref[...])).astype(BF16)
    cv_o[...] = _bdot(lb, wv_ref[...]).astype(BF16)


def _mla_kv_call(lat, krp, wk, we, wv):
    n = lat.shape[0]
    tm = 512
    return pl.pallas_call(
        _mla_kv_kernel,
        grid=(n // tm,),
        in_specs=[pl.BlockSpec((tm, KV_LORA), lambda i: (i, 0)),
                  pl.BlockSpec((tm, LANES), lambda i: (i, 0)),
                  pl.BlockSpec(wk.shape, lambda i: (0, 0)),
                  pl.BlockSpec(we.shape, lambda i: (0, 0)),
                  pl.BlockSpec(wv.shape, lambda i: (0, 0))],
        out_specs=[pl.BlockSpec((tm, 512), lambda i: (i, 0)), pl.BlockSpec((tm, 512), lambda i: (i, 0))],
        out_shape=[jax.ShapeDtypeStruct((n, 512), BF16), jax.ShapeDtypeStruct((n, 512), BF16)],
        compiler_params=_cparams(("arbitrary",)),
        name="mla_cache_kv",
    )(lat, krp, wk, we, wv)


def _block_range(r0, tq, tk, nkb):
    n_full = jnp.minimum((r0 + CHUNK) // tk, nkb)
    n_vis = jnp.minimum((r0 + tq + tk - 1) // tk, nkb)
    return n_full, n_vis


def _visible(r0, ks, rows, tk):
    rpos = r0 + lax.broadcasted_iota(jnp.int32, (rows, tk), 0)
    kpos = ks + lax.broadcasted_iota(jnp.int32, (rows, tk), 1)
    return kpos < (((rpos >> 6) + 1) << 6)


def _fold_max(s, m_sc, rows):
    mp = m_sc[rows, :]
    for j in range(s.shape[1] // LANES):
        mp = jnp.maximum(mp, s[:, j * LANES:(j + 1) * LANES])
    m_sc[rows, :] = mp


def _finish_max(m_sc, rows):
    mp = m_sc[rows, :]
    m_sc[rows, :] = jnp.broadcast_to(jnp.max(mp, axis=1, keepdims=True), mp.shape)


def _accumulate(s, vblk, m_sc, l_sc, acc_sc, rows):
    m = m_sc[rows, :]
    lp = l_sc[rows, :]
    ps = []
    for j in range(s.shape[1] // LANES):
        pj = jnp.exp2(s[:, j * LANES:(j + 1) * LANES] - m)
        lp = lp + pj
        ps.append(pj.astype(BF16))
    l_sc[rows, :] = lp
    acc_sc[rows, :] = acc_sc[rows, :] + _bdot(jnp.concatenate(ps, axis=1), vblk)


WIDE = 4
BISECT_EXTRA = 6


def _two_pass(n_full, n_vis, score, values, m_sc, l_sc, acc_sc, sbuf, groups):
    tk = sbuf.shape[2]
    m_sc[...] = jnp.full(m_sc.shape, NEG_INF, F32)
    l_sc[...] = jnp.zeros(l_sc.shape, F32)
    acc_sc[...] = jnp.zeros(acc_sc.shape, F32)

    for g, rows in enumerate(groups):
        def keep(kb, nblk, masked):
            s = score(kb, nblk, masked, g)
            for w in range(nblk):
                sbuf[kb + w] = s[:, w * tk:(w + 1) * tk]
            _fold_max(s, m_sc, rows)

        def wide_body(j, c):
            keep(j * WIDE, WIDE, False)
            return c

        def full_body(kb, c):
            keep(kb, 1, False)
            return c

        def masked_body(kb, c):
            keep(kb, 1, True)
            return c

        n_wide = n_full // WIDE
        lax.fori_loop(0, n_wide, wide_body, 0)
        lax.fori_loop(n_wide * WIDE, n_full, full_body, 0)
        lax.fori_loop(n_full, n_vis, masked_body, 0)
        _finish_max(m_sc, rows)

        def acc_wide(j, c):
            kb = j * WIDE
            s = jnp.concatenate([sbuf[kb + w] for w in range(WIDE)], axis=1)
            _accumulate(s, values(kb, WIDE, g), m_sc, l_sc, acc_sc, rows)
            return c

        def acc_body(kb, c):
            _accumulate(sbuf[kb], values(kb, 1, g), m_sc, l_sc, acc_sc, rows)
            return c

        n_wide = n_vis // WIDE
        lax.fori_loop(0, n_wide, acc_wide, 0)
        lax.fori_loop(n_wide * WIDE, n_vis, acc_body, 0)
    return acc_sc[...] / jnp.sum(l_sc[...], axis=1, keepdims=True)


def _split_halves(q, lane=None):
    lane1 = lax.broadcasted_iota(jnp.int32, (1, LANES), 1)
    lo = jnp.where(lane1 < 64, 1.0, 0.0).astype(q.dtype)
    return q * lo, q * (1.0 - lo).astype(q.dtype)


def _diff_kernel(lqk_ref, sub_ref, q_ref, k_ref, v_ref, o_ref, m_sc, l_sc, acc_sc, sbuf,
                 *, tq, tk, nkb, q_pos0, lam_init, heads):
    i = pl.program_id(2)
    r0 = q_pos0 + i * tq
    n_full, n_vis = _block_range(r0, tq, tk, nkb)
    lane = lax.broadcasted_iota(jnp.int32, (tq, LANES), 1)
    lq = lqk_ref[...]
    lam = (jnp.exp(jnp.sum(lq[0:1] * lq[1:2], axis=1, keepdims=True))
           - jnp.exp(jnp.sum(lq[2:3] * lq[3:4], axis=1, keepdims=True)) + lam_init)

    for h in range(heads):
        hl = slice(h * LANES, (h + 1) * LANES)
        qq = jnp.concatenate(_split_halves(q_ref[:, hl], lane), axis=0)

        def score(kb, nblk, masked, g, qq=qq, hl=hl):
            ks = pl.multiple_of(kb * tk, tk)
            s = _dot_nt(qq, k_ref[pl.ds(ks, nblk * tk), hl])
            if masked:
                vis = _visible(r0, ks, tq, nblk * tk)
                s = jnp.where(jnp.concatenate([vis, vis], axis=0), s, NEG_INF)
            return s

        def values(kb, nblk, g, hl=hl):
            return v_ref[pl.ds(pl.multiple_of(kb * tk, tk), nblk * tk), hl]

        o = _two_pass(n_full, n_vis, score, values, m_sc, l_sc, acc_sc, sbuf, (slice(0, 2 * tq),))
        o = o[:tq] - lam * o[tq:]
        o = o * lax.rsqrt(jnp.mean(o * o, axis=1, keepdims=True) + NORM_EPS)
        o_ref[:, hl] = (o * sub_ref[...] * (1.0 - lam_init)).astype(BF16)


def _diff_call(lqk, sub, q, k, v, *, nb, nq, tq, tk, lp, q_blk0, k_blk0, q_pos0, lam_init, heads=1):
    nkb = lp // tk
    kern = functools.partial(_diff_kernel, tq=tq, tk=tk, nkb=nkb, q_pos0=q_pos0, lam_init=lam_init,
                             heads=heads)
    hw = heads * LANES
    return pl.pallas_call(
        kern,
        grid=(nb, A_HEADS // heads, nq),
        in_specs=[pl.BlockSpec(lqk.shape, lambda b, h, i: (0, 0)),
                  pl.BlockSpec((1, LANES), lambda b, h, i: (0, 0)),
                  pl.BlockSpec((tq, hw), lambda b, h, i: (q_blk0 + b * nq + i, h)),
                  pl.BlockSpec((lp, hw), lambda b, h, i: (k_blk0 + b, h)),
                  pl.BlockSpec((lp, hw), lambda b, h, i: (k_blk0 + b, h))],
        out_specs=pl.BlockSpec((tq, hw), lambda b, h, i: (b * nq + i, h)),
        out_shape=jax.ShapeDtypeStruct((nb * nq * tq, A_WIDTH), BF16),
        scratch_shapes=([pltpu.VMEM((2 * tq, LANES), F32) for _ in range(3)]
                        + [pltpu.VMEM((nkb, 2 * tq, tk), F32)]),
        compiler_params=_cparams(("arbitrary", "arbitrary", "arbitrary")),
        name="diff_attention",
    )(lqk, sub, q, k, v)


def _mla_kernel(q_ref, k_ref, v_ref, o_ref, m_sc, l_sc, acc_sc, sbuf, *, tq, tk, nkb, q_pos0, heads):
    i = pl.program_id(2)
    r0 = q_pos0 + i * tq
    n_full, n_vis = _block_range(r0, tq, tk, nkb)

    for h in range(heads):
        hl = slice(h * LANES, (h + 1) * LANES)
        q = q_ref[:, hl]

        def score(kb, nblk, masked, g, q=q, hl=hl):
            ks = pl.multiple_of(kb * tk, tk)
            s = _dot_nt(q, k_ref[pl.ds(ks, nblk * tk), hl])
            if masked:
                s = jnp.where(_visible(r0, ks, tq, nblk * tk), s, NEG_INF)
            return s

        def values(kb, nblk, g, hl=hl):
            return v_ref[pl.ds(pl.multiple_of(kb * tk, tk), nblk * tk), hl]

        o = _two_pass(n_full, n_vis, score, values, m_sc, l_sc, acc_sc, sbuf, (slice(0, tq),))
        o_ref[:, hl] = o.astype(BF16)


def _mla_call(q, k, v, *, nb, nq, tq, tk, lp, q_blk0, k_blk0, q_pos0, heads=1):
    nkb = lp // tk
    kern = functools.partial(_mla_kernel, tq=tq, tk=tk, nkb=nkb, q_pos0=q_pos0, heads=heads)
    hw = heads * LANES
    return pl.pallas_call(
        kern,
        grid=(nb, C_HEADS // heads, nq),
        in_specs=[pl.BlockSpec((tq, hw), lambda b, h, i: (q_blk0 + b * nq + i, h)),
                  pl.BlockSpec((lp, hw), lambda b, h, i: (k_blk0 + b, h)),
                  pl.BlockSpec((lp, hw), lambda b, h, i: (k_blk0 + b, h))],
        out_specs=pl.BlockSpec((tq, hw), lambda b, h, i: (b * nq + i, h)),
        out_shape=jax.ShapeDtypeStruct((nb * nq * tq, C_HEADS * LANES), BF16),
        scratch_shapes=([pltpu.VMEM((tq, LANES), F32) for _ in range(3)]
                        + [pltpu.VMEM((nkb, tq, tk), F32)]),
        compiler_params=_cparams(("arbitrary", "arbitrary", "arbitrary")),
        name="mla_attention",
    )(q, k, v)


def _dsa_kernel(q_ref, qi_ref, w_ref, k_ref, v_ref, ki_ref, o_ref,
                s_sc, m_sc, l_sc, acc_sc, t_sc, need_sc, carry_sc, wrep_sc, sbuf,
                *, tq, tk, nkb, q_pos0, n_keys, k_sel, rs, n_bisect):
    i = pl.program_id(1)
    r0 = q_pos0 + i * tq
    n_full, n_vis = _block_range(r0, tq, tk, nkb)
    lane = lax.broadcasted_iota(jnp.int32, (tq, LANES), 1)
    ksel_f = float(k_sel)

    qi = qi_ref[...]
    qa, qb = _split_halves(qi[:, :LANES], lane)
    qc, qd = _split_halves(qi[:, LANES:], lane)
    qi4 = jnp.concatenate([qa, qb, qc, qd], axis=0)
    wm = w_ref[...]
    for h in range(IDX_HEADS):
        wcol = wm[:, _MISC_BIW + h:_MISC_BIW + h + 1] * (IDX_HEADS ** -0.5)
        wrep_sc[h] = jnp.broadcast_to(wcol, (tq, LANES))

    def score_step(kb, masked):
        ks = pl.multiple_of(kb * tk, tk)
        rel = _dot_nt(qi4, ki_ref[pl.ds(ks, tk), :])
        cols = []
        for j in range(tk // LANES):
            cl = slice(j * LANES, (j + 1) * LANES)
            sc = wrep_sc[0] * jnp.maximum(rel[:tq, cl], 0.0)
            for h in range(1, IDX_HEADS):
                sc = sc + wrep_sc[h] * jnp.maximum(rel[h * tq:(h + 1) * tq, cl], 0.0)
            cols.append(sc)
        sc = jnp.concatenate(cols, axis=1)
        if masked:
            sc = jnp.where(_visible(r0, ks, tq, tk), sc, NEG_INF)
        s_sc[kb] = sc
        mx = m_sc[0:tq, :]
        mn = l_sc[0:tq, :]
        for j in range(tk // LANES):
            sj = sc[:, j * LANES:(j + 1) * LANES]
            mx = jnp.maximum(mx, sj)
            mn = jnp.minimum(mn, jnp.where(sj > HALF_NEG, sj, BIG_POS) if masked else sj)
        m_sc[0:tq, :] = mx
        l_sc[0:tq, :] = mn

    def score_full(kb, c):
        score_step(kb, False)
        return c

    def score_masked(kb, c):
        score_step(kb, True)
        return c

    m_sc[0:tq, :] = jnp.full((tq, LANES), BELOW_NEG, F32)
    l_sc[0:tq, :] = jnp.full((tq, LANES), BIG_POS, F32)
    lax.fori_loop(0, n_full, score_full, 0)
    lax.fori_loop(n_full, n_vis, score_masked, 0)

    nl = tk // LANES

    def search(sb, flag):
        rsl = pl.ds(pl.multiple_of(sb * rs, rs), rs)

        def fold(fn, init):
            def body(kb, part):
                s = s_sc[kb, rsl, :]
                for j in range(nl):
                    part = fn(part, s[:, j * LANES:(j + 1) * LANES])
                return part
            return lax.fori_loop(0, n_vis, body, init)

        def bc(x):
            return jnp.broadcast_to(x, (rs, LANES))

        def count(cmp, x):
            xb = bc(x)
            part = fold(lambda p, sj: p + jnp.where(cmp(sj, xb), 1.0, 0.0), jnp.zeros((rs, LANES), F32))
            return jnp.sum(part, axis=1, keepdims=True)

        def max_below(x, strict):
            xb = bc(x)
            if strict:
                part = fold(lambda p, sj: jnp.maximum(p, jnp.where(sj < xb, sj, BELOW_NEG)),
                            jnp.full((rs, LANES), BELOW_NEG, F32))
            else:
                part = fold(lambda p, sj: jnp.maximum(p, jnp.where(sj <= xb, sj, BELOW_NEG)),
                            jnp.full((rs, LANES), BELOW_NEG, F32))
            return jnp.max(part, axis=1, keepdims=True)

        ge = lambda a, b: a >= b
        gt = lambda a, b: a > b

        hi = jnp.max(m_sc[rsl, :], axis=1, keepdims=True)
        lo = jnp.min(l_sc[rsl, :], axis=1, keepdims=True)
        rpos = r0 + sb * rs + lax.broadcasted_iota(jnp.int32, (rs, 1), 0)
        n_valid = jnp.minimum(((rpos >> 6) + 1) << 6, n_keys)
        small = n_valid < k_sel

        def bisect_step(c):
            lo_c, hi_c, glo_c, ghi_c = c
            mid = 0.5 * (lo_c + hi_c)
            g = count(ge, mid)
            up = g >= ksel_f
            return (jnp.where(up, mid, lo_c), jnp.where(up, hi_c, mid),
                    jnp.where(up, g, glo_c), jnp.where(up, ghi_c, g))

        state = (lo, hi, n_valid.astype(F32), jnp.zeros((rs, 1), F32))
        state = lax.fori_loop(0, n_bisect, lambda _, c: bisect_step(c), state)

        def more_cond(c):
            return jnp.logical_and(c[0] < BISECT_EXTRA, jnp.max(c[1][2] - c[1][3]) > 1.5)

        def more_body(c):
            return c[0] + 1, bisect_step(c[1])

        _, state = lax.while_loop(more_cond, more_body, (jnp.int32(0), state))
        hi = state[1]

        v0 = max_below(hi, False)
        g0 = count(ge, v0)
        done0 = jnp.where(jnp.logical_or(g0 >= ksel_f, small), 1.0, 0.0)

        def walk_cond(c):
            return jnp.min(c[2]) < 0.5

        def walk_body(c):
            t_c, g_c, done_c = c
            v = max_below(t_c, True)
            g = count(ge, v)
            keep = done_c > 0.5
            return (jnp.where(keep, t_c, v), jnp.where(keep, g_c, g),
                    jnp.where(jnp.logical_or(keep, g >= ksel_f), 1.0, 0.0))

        t, g_t, _ = lax.while_loop(walk_cond, walk_body, (v0, g0, done0))
        c_gt = count(gt, t)
        need = ksel_f - c_gt
        excess = jnp.logical_and(jnp.logical_and(g_t - c_gt > need, jnp.logical_not(small)), t > HALF_NEG)
        t_sc[rsl, :] = jnp.where(small, NEG_INF, t)
        need_sc[rsl, :] = need
        return jnp.maximum(flag, jnp.max(jnp.where(excess, 1.0, 0.0)))

    tie_flag = lax.fori_loop(0, tq // rs, search, jnp.float32(0.0))

    q = q_ref[...]
    q0a, q0b = _split_halves(q[:, :LANES], lane)
    q1a, q1b = _split_halves(q[:, LANES:], lane)
    qq = (jnp.concatenate([q0a, q0b], axis=0), jnp.concatenate([q1a, q1b], axis=0))
    carry_sc[...] = jnp.zeros(carry_sc.shape, F32)
    t_all = t_sc[...]

    def bias_step(kb, masked):
        ks = pl.multiple_of(kb * tk, tk)
        sc = s_sc[kb]
        vis = _visible(r0, ks, tq, tk) if masked else None

        def store(sel):
            if masked:
                sel = jnp.logical_and(sel, vis)
            s_sc[kb] = jnp.where(sel, 0.0, NEG_INF)

        def plain():
            store(sc >= t_all)

        def with_ties():
            eq = sc == t_all
            if masked:
                eq = jnp.logical_and(eq, vis)
            eqf = jnp.where(eq, 1.0, 0.0)
            upper = (lax.broadcasted_iota(jnp.int32, (tk, tk), 0)
                     < lax.broadcasted_iota(jnp.int32, (tk, tk), 1))
            before = _bdot(eqf.astype(BF16), jnp.where(upper, 1.0, 0.0).astype(BF16)) + carry_sc[...]
            carry_sc[...] = carry_sc[...] + jnp.sum(eqf, axis=1, keepdims=True)
            store(jnp.logical_or(sc > t_all, jnp.logical_and(eq, before < need_sc[...])))

        lax.cond(tie_flag > 0.5, with_ties, plain)

    def bias_full(kb, c):
        bias_step(kb, False)
        return c

    def bias_masked(kb, c):
        bias_step(kb, True)
        return c

    lax.fori_loop(0, n_full, bias_full, 0)
    lax.fori_loop(n_full, n_vis, bias_masked, 0)

    def score(kb, nblk, masked, g):
        ks = pl.multiple_of(kb * tk, tk)
        bias = jnp.concatenate([s_sc[kb + w] for w in range(nblk)], axis=1) if nblk > 1 else s_sc[kb]
        s = _dot_nt(qq[g], k_ref[pl.ds(ks, nblk * tk), g * LANES:(g + 1) * LANES])
        return s + jnp.concatenate([bias, bias], axis=0)

    def values(kb, nblk, g):
        return v_ref[pl.ds(pl.multiple_of(kb * tk, tk), nblk * tk), g * LANES:(g + 1) * LANES]

    o = _two_pass(n_full, n_vis, score, values, m_sc, l_sc, acc_sc, sbuf,
                  (slice(0, 2 * tq), slice(2 * tq, 4 * tq)))
    for pr in range(2):
        lo_h = o[(2 * pr) * tq:(2 * pr + 1) * tq]
        hi_h = o[(2 * pr + 1) * tq:(2 * pr + 2) * tq]
        o_ref[:, pr * LANES:(pr + 1) * LANES] = jnp.where(lane < 64, lo_h, hi_h).astype(BF16)


def _dsa_call(q, qi, w, k, v, ki, *, nb, nq, tq, tk, lp, q_blk0, k_blk0, q_pos0, n_keys, n_bisect):
    nkb = lp // tk
    k_sel = min(DSA_TOPK, n_keys // 4)
    rs = min(64, tq)
    kern = functools.partial(_dsa_kernel, tq=tq, tk=tk, nkb=nkb, q_pos0=q_pos0, n_keys=n_keys,
                             k_sel=k_sel, rs=rs, n_bisect=n_bisect)
    return pl.pallas_call(
        kern,
        grid=(nb, nq),
        in_specs=[pl.BlockSpec((tq, B_WIDTH), lambda b, i: (q_blk0 + b * nq + i, 0)),
                  pl.BlockSpec((tq, B_WIDTH), lambda b, i: (q_blk0 + b * nq + i, 0)),
                  pl.BlockSpec((tq, LANES), lambda b, i: (q_blk0 + b * nq + i, 0)),
                  pl.BlockSpec((lp, B_WIDTH), lambda b, i: (k_blk0 + b, 0)),
                  pl.BlockSpec((lp, B_WIDTH), lambda b, i: (k_blk0 + b, 0)),
                  pl.BlockSpec((lp, LANES), lambda b, i: (k_blk0 + b, 0))],
        out_specs=pl.BlockSpec((tq, B_WIDTH), lambda b, i: (b * nq + i, 0)),
        out_shape=jax.ShapeDtypeStruct((nb * nq * tq, B_WIDTH), BF16),
        scratch_shapes=[pltpu.VMEM((nkb, tq, tk), F32),
                        pltpu.VMEM((4 * tq, LANES), F32), pltpu.VMEM((4 * tq, LANES), F32),
                        pltpu.VMEM((4 * tq, LANES), F32),
                        pltpu.VMEM((tq, 1), F32), pltpu.VMEM((tq, 1), F32), pltpu.VMEM((tq, 1), F32),
                        pltpu.VMEM((IDX_HEADS, tq, LANES), F32),
                        pltpu.VMEM((nkb, 2 * tq, tk), F32)],
        compiler_params=_cparams(("arbitrary", "arbitrary")),
        name="dsa_attention",
    )(q, qi, w, k, v, ki)


def _kb_kernel(x_ref, oa_ref, ob_ref, oc_ref, sha_ref, sca_ref, gta_ref, shm_ref, scm_ref,
               nmix_ref, nffn_ref, wg_ref, wba_ref, wbb_ref, wbc_ref, wout_ref, rw_ref, rb_ref,
               x1_o, h2_o, e_o, g_o, cnt_o, cnt_sc, *, tm, ch):
    x = x_ref[...]
    hb = _modulate(_rms(x, nmix_ref[...]), sca_ref[...], sha_ref[...], tm, ch).astype(BF16)

    def gate(c0):
        return _sigmoid(_bdot(hb, wg_ref[:, c0:c0 + D_MODEL]))

    merged = gate(0) * _bdot(oa_ref[...], wba_ref[...])
    merged = merged + gate(D_MODEL) * _bdot(ob_ref[...], wbb_ref[...])
    merged = merged + gate(2 * D_MODEL) * _bdot(oc_ref[...], wbc_ref[...])
    y = _bdot(merged.astype(BF16), wout_ref[...])
    x1 = x + _scale_rows(y, gta_ref[...], tm, ch)
    x1_o[...] = x1
    h2 = _modulate(_rms(x1, nffn_ref[...]), scm_ref[...], shm_ref[...], tm, ch)
    h2_o[...] = h2
    lg = _dot_split(h2, rw_ref[...]) + rb_ref[...]
    lanef = lax.broadcasted_iota(jnp.int32, (tm, LANES), 1).astype(F32)
    e_acc = jnp.zeros((tm, LANES), F32)
    v_acc = jnp.full((tm, LANES), NEG_INF, F32)
    chosen = jnp.zeros((tm, LANES), F32)
    picks = []
    for k in range(TOP_K):
        mx = jnp.max(lg, axis=1, keepdims=True)
        idx = jnp.min(jnp.where(lg == mx, lanef, float(LANES)), axis=1, keepdims=True)
        hit = lanef == idx
        picks.append(hit)
        chosen = jnp.where(hit, 1.0, chosen)
        e_acc = jnp.where(lanef == float(k), idx, e_acc)
        v_acc = jnp.where(lanef == float(k), mx, v_acc)
        lg = jnp.where(hit, BELOW_NEG, lg)
    ex = jnp.where(lanef < float(TOP_K), jnp.exp(v_acc - jnp.max(v_acc, axis=1, keepdims=True)), 0.0)
    g_o[...] = ex / jnp.sum(ex, axis=1, keepdims=True)
    @pl.when(pl.program_id(0) == 0)
    def _():
        cnt_sc[...] = jnp.zeros(cnt_sc.shape, F32)

    earlier = (lax.broadcasted_iota(jnp.int32, (tm, tm), 1) < lax.broadcasted_iota(jnp.int32, (tm, tm), 0))
    before = _bdot(jnp.where(earlier, 1.0, 0.0).astype(BF16), chosen.astype(BF16)) + cnt_sc[0:1, :]
    for k in range(TOP_K):
        rank = jnp.sum(jnp.where(picks[k], before, 0.0), axis=1, keepdims=True)
        e_acc = jnp.where(lanef == float(TOP_K + k), rank, e_acc)
    e_o[...] = e_acc.astype(jnp.int32)
    cnt_sc[...] = cnt_sc[...] + jnp.sum(chosen, axis=0, keepdims=True)
    cnt_o[...] = cnt_sc[...]


def _kb_call(x, oa, ob, oc, mods, nmix, nffn, wg, wba, wbb, wbc, wout, rw, rb):
    n = x.shape[0]
    tm = TOKEN_BLOCK
    ch = tm // MOD_ROWS

    def row(i):
        return (i, 0)

    def full(a):
        return pl.BlockSpec(a.shape, lambda i: (0, 0))

    in_specs = ([pl.BlockSpec((tm, D_MODEL), row), pl.BlockSpec((tm, A_WIDTH), row),
                 pl.BlockSpec((tm, B_WIDTH), row), pl.BlockSpec((tm, 512), row)]
                + [pl.BlockSpec((MOD_ROWS, D_MODEL), row) for _ in range(5)]
                + [full(a) for a in (nmix, nffn, wg, wba, wbb, wbc, wout, rw, rb)])
    out_shape = [jax.ShapeDtypeStruct((n, D_MODEL), F32), jax.ShapeDtypeStruct((n, D_MODEL), F32),
                 jax.ShapeDtypeStruct((n, LANES), jnp.int32), jax.ShapeDtypeStruct((n, LANES), F32),
                 jax.ShapeDtypeStruct((8, LANES), F32)]
    out_specs = [pl.BlockSpec((tm, D_MODEL), row), pl.BlockSpec((tm, D_MODEL), row),
                 pl.BlockSpec((tm, LANES), row), pl.BlockSpec((tm, LANES), row),
                 pl.BlockSpec((8, LANES), lambda i: (0, 0))]
    return pl.pallas_call(
        functools.partial(_kb_kernel, tm=tm, ch=ch),
        grid=(n // tm,),
        in_specs=in_specs,
        out_specs=out_specs,
        out_shape=out_shape,
        scratch_shapes=[pltpu.VMEM((8, LANES), F32)],
        compiler_params=_cparams(("arbitrary",)),
        name="post_attention",
    )(x, oa, ob, oc, *mods, nmix, nffn, wg, wba, wbb, wbc, wout, rw, rb)


def _ke_kernel(be_ref, nu_ref, x_ref, wg_ref, wl_ref, wd_ref, bg_ref, bl_ref, bd_ref, y_ref):
    i = pl.program_id(0)

    @pl.when(i < nu_ref[0])
    def _():
        x = x_ref[...].astype(BF16)
        g = jnp.minimum(_bdot(x, wg_ref[0]) + bg_ref[0], SWIGLU_LIMIT)
        l = jnp.clip(_bdot(x, wl_ref[0]) + bl_ref[0], -SWIGLU_LIMIT, SWIGLU_LIMIT)
        act = g * _sigmoid(SWIGLU_ALPHA * g) * (l + 1.0)
        y_ref[...] = _bdot(act.astype(BF16), wd_ref[0]) + bd_ref[0]

    @pl.when(i >= nu_ref[0])
    def _():
        y_ref[...] = jnp.zeros(y_ref.shape, F32)


def _ke_call(blk_e, n_used, xg, wg, wl, wd, bg, bl, bd):
    n_rows = xg.shape[0]
    eb = EXPERT_ROWS
    n_blocks = n_rows // eb
    grid_spec = pltpu.PrefetchScalarGridSpec(
        num_scalar_prefetch=2,
        grid=(n_blocks,),
        in_specs=[pl.BlockSpec((eb, D_MODEL), lambda i, be, nu: (i, 0)),
                  pl.BlockSpec((1, D_MODEL, D_FF), lambda i, be, nu: (be[i], 0, 0)),
                  pl.BlockSpec((1, D_MODEL, D_FF), lambda i, be, nu: (be[i], 0, 0)),
                  pl.BlockSpec((1, D_FF, D_MODEL), lambda i, be, nu: (be[i], 0, 0)),
                  pl.BlockSpec((1, 1, D_FF), lambda i, be, nu: (be[i], 0, 0)),
                  pl.BlockSpec((1, 1, D_FF), lambda i, be, nu: (be[i], 0, 0)),
                  pl.BlockSpec((1, 1, D_MODEL), lambda i, be, nu: (be[i], 0, 0))],
        out_specs=pl.BlockSpec((eb, D_MODEL), lambda i, be, nu: (i, 0)),
    )
    return pl.pallas_call(
        _ke_kernel,
        grid_spec=grid_spec,
        out_shape=jax.ShapeDtypeStruct((n_rows, D_MODEL), F32),
        compiler_params=_cparams(("arbitrary",)),
        name="moe_experts",
    )(blk_e, n_used, xg, wg, wl, wd, bg, bl, bd)


def _kc_kernel(dcur_ref, dnext_ref, x1_ref, yr_hbm, g_ref, gtm_ref, fn_ref, x2_o, y_o, buf, sem,
               *, tm, ch):
    i = pl.program_id(0)
    nblk = pl.num_programs(0)
    slot = lax.rem(i, 2)
    nrow = TOP_K * tm

    def row_copy(d_ref, s, r):
        return pltpu.make_async_copy(yr_hbm.at[pl.ds(d_ref[0, 0, r], 1), :],
                                     buf.at[s, pl.ds(r, 1), :], sem.at[s])

    def issue(d_ref, s):
        def body(r, c):
            row_copy(d_ref, s, r).start()
            return c
        lax.fori_loop(0, nrow, body, 0, unroll=8)

    @pl.when(i == 0)
    def _():
        issue(dcur_ref, 0)

    @pl.when(i + 1 < nblk)
    def _():
        issue(dnext_ref, 1 - slot)

    pltpu.make_async_copy(buf.at[slot], buf.at[slot], sem.at[slot]).wait()
    gate = g_ref[...]
    ffn = gate[:, 0:1] * buf[slot, 0:tm, :]
    for k in range(1, TOP_K):
        ffn = ffn + gate[:, k:k + 1] * buf[slot, k * tm:(k + 1) * tm, :]
    x2 = x1_ref[...] + _scale_rows(ffn, gtm_ref[...], tm, ch)
    x2_o[...] = x2
    y_o[...] = _rms(x2, fn_ref[...])


def _kc_call(x1, yr, dest_blocks, gate, gtm, fnorm):
    n = x1.shape[0]
    tm = TOKEN_BLOCK
    ch = tm // MOD_ROWS
    nblk = n // tm
    smem = functools.partial(pl.BlockSpec, (1, 1, TOP_K * tm), memory_space=pltpu.SMEM)
    return pl.pallas_call(
        functools.partial(_kc_kernel, tm=tm, ch=ch),
        grid=(nblk,),
        in_specs=[smem(lambda i: (i, 0, 0)),
                  smem(lambda i: (jnp.minimum(i + 1, nblk - 1), 0, 0)),
                  pl.BlockSpec((tm, D_MODEL), lambda i: (i, 0)),
                  pl.BlockSpec(memory_space=pl.ANY),
                  pl.BlockSpec((tm, LANES), lambda i: (i, 0)),
                  pl.BlockSpec((MOD_ROWS, D_MODEL), lambda i: (i, 0)),
                  pl.BlockSpec((1, D_MODEL), lambda i: (0, 0))],
        out_specs=[pl.BlockSpec((tm, D_MODEL), lambda i: (i, 0)), pl.BlockSpec((tm, D_MODEL), lambda i: (i, 0))],
        out_shape=[jax.ShapeDtypeStruct((n, D_MODEL), F32), jax.ShapeDtypeStruct((n, D_MODEL), F32)],
        scratch_shapes=[pltpu.VMEM((2, TOP_K * tm, D_MODEL), F32), pltpu.SemaphoreType.DMA((2,))],
        compiler_params=_cparams(("arbitrary",)),
        name="moe_combine",
    )(dest_blocks, dest_blocks, x1, yr, gate, gtm, fnorm)


def _rope_tables(pos):
    lane = np.arange(LANES)
    inv32 = ROPE_THETA ** (-jnp.arange(32, dtype=F32) / 32)
    inv16 = ROPE_THETA ** (-jnp.arange(16, dtype=F32) / 16)
    ang64 = pos[:, None] * inv32[None, :][:, lane & 31]
    ang32 = pos[:, None] * inv16[None, :][:, lane & 15]
    sign64 = jnp.asarray(np.where((lane & 63) < 32, -1.0, 1.0), F32)[None, :]
    sign32 = jnp.asarray(np.where((lane & 31) < 16, -1.0, 1.0), F32)[None, :]
    in_m = jnp.asarray(lane < C_ROPE)[None, :]
    in_q = jnp.asarray((lane >= C_NOPE) & (lane < C_NOPE + C_ROPE))[None, :]
    cos64, sin64 = jnp.cos(ang64), jnp.sin(ang64) * sign64
    cos32, sin32 = jnp.cos(ang32), jnp.sin(ang32) * sign32
    return (cos64, sin64,
            jnp.where(in_m, cos32, 1.0), jnp.where(in_m, sin32, 0.0),
            jnp.where(in_q, cos32, 1.0), jnp.where(in_q, sin32, 0.0))


def _layer_weights(l, w_in, mla_w_uq, mla_w_ukv, w_br_c, router_w, router_b, exp_w_gu, exp_b_gu):
    wi = w_in[l]
    z = lambda n: jnp.zeros((D_MODEL, n), F32)
    w1 = jnp.concatenate([
        wi[:, _OFF_AQ:_OFF_BIK],
        wi[:, _OFF_BIK:_OFF_BIW], wi[:, _OFF_BIK:_OFF_BIW],
        wi[:, _OFF_CKR:_OFF_GATES], wi[:, _OFF_BIW:_OFF_CQ], z(LANES - C_ROPE - IDX_HEADS),
        wi[:, _OFF_CQ:_OFF_CKV], wi[:, _OFF_CKV:_OFF_CKR]], axis=1).astype(BF16)
    wg = wi[:, _OFF_GATES:].astype(BF16)
    wuq = mla_w_uq[l].reshape(Q_LORA, C_HEADS, C_NOPE + C_ROPE)
    wuq = jnp.pad(wuq, ((0, 0), (0, 0), (0, LANES - C_NOPE - C_ROPE))).reshape(Q_LORA, C_HEADS * LANES).astype(BF16)
    wukv = mla_w_ukv[l].reshape(KV_LORA, C_HEADS, C_NOPE + C_V)
    wk = jnp.pad(wukv[:, :, :C_NOPE], ((0, 0), (0, 0), (0, LANES - C_NOPE))).reshape(KV_LORA, C_HEADS * LANES).astype(BF16)
    wv = jnp.pad(wukv[:, :, C_NOPE:], ((0, 0), (0, 0), (0, LANES - C_V))).reshape(KV_LORA, C_HEADS * LANES).astype(BF16)
    e = np.zeros((LANES, C_HEADS * LANES), np.float32)
    for h in range(C_HEADS):
        e[np.arange(C_ROPE), h * LANES + C_NOPE + np.arange(C_ROPE)] = 1.0
    we = jnp.asarray(e, BF16)
    wbc = jnp.pad(w_br_c[l].reshape(C_HEADS, C_V, D_MODEL), ((0, 0), (0, LANES - C_V), (0, 0)))
    wbc = wbc.reshape(C_HEADS * LANES, D_MODEL).astype(BF16)
    rw = jnp.pad(router_w[l], ((0, 0), (0, LANES - N_EXPERTS)))
    rb = jnp.pad(router_b[l], (0, LANES - N_EXPERTS), constant_values=NEG_INF).reshape(1, LANES)
    wgl = exp_w_gu[l].reshape(N_EXPERTS, D_MODEL, D_FF, 2)
    bgl = exp_b_gu[l].reshape(N_EXPERTS, 1, D_FF, 2)
    return dict(w1=w1, wg=wg, wuq=wuq, wk=wk, wv=wv, we=we, wbc=wbc, rw=rw, rb=rb,
                e_wg=wgl[..., 0].astype(BF16), e_wl=wgl[..., 1].astype(BF16),
                e_bg=bgl[..., 0], e_bl=bgl[..., 1])


def _route(e_pad, cnt, n_rows_pad):
    eb = EXPERT_ROWS
    flat_e = e_pad[:, :TOP_K].reshape(-1)
    rank = e_pad[:, TOP_K:2 * TOP_K].reshape(-1)
    nk = flat_e.shape[0]
    counts = cnt[0, :N_EXPERTS].astype(jnp.int32)
    padded = (counts + eb - 1) // eb * eb
    pad_end = jnp.cumsum(padded)
    pad_start = pad_end - padded
    dest = pad_start[flat_e] + rank
    row_tok = jnp.zeros((n_rows_pad,), jnp.int32).at[dest].set(jnp.arange(nk, dtype=jnp.int32) // TOP_K)
    n_blocks = n_rows_pad // eb
    blk_start = jnp.arange(n_blocks, dtype=jnp.int32) * eb
    blk_e = jnp.sum((pad_end[None, :] <= blk_start[:, None]).astype(jnp.int32), axis=1)
    blk_e = jnp.minimum(blk_e, N_EXPERTS - 1)
    n_used = (pad_end[-1] // eb).astype(jnp.int32).reshape(1)
    return dest, row_tok, blk_e, n_used


def kernel(x_prompt, x_sample, cache_a_k, cache_a_v, cache_b_k, cache_b_v, cache_b_idx_k, cache_c_latent, cache_c_k_rope, c_prompt, c_sample, w_ada, b_ada, norm_mix, norm_ffn, w_in, diff_lq1, diff_lk1, diff_lq2, diff_lk2, diff_subln, mla_q_norm, mla_w_uq, mla_kv_norm, mla_w_ukv, w_br_a, w_br_b, w_br_c, w_out, router_w, router_b, exp_w_gu, exp_b_gu, exp_w_down, exp_b_down, final_norm):
    depth = w_ada.shape[0]
    bp, tp, _ = x_prompt.shape
    bs, ts, _ = x_sample.shape
    past = cache_c_latent.shape[2]
    n_p, n_s = bp * tp, bs * ts
    n = n_p + n_s
    tm = TOKEN_BLOCK
    ch = tm // MOD_ROWS
    assert ts == CHUNK and tp % tm == 0 and n_s % tm == 0 and past % CHUNK == 0

    x = jnp.concatenate([x_prompt.reshape(n_p, D_MODEL), x_sample.reshape(n_s, D_MODEL)], axis=0)

    n_seq = bp + bs
    c_all = jnp.concatenate([c_prompt, c_sample], axis=0)
    c_pad = jnp.pad(c_all, ((0, (-n_seq) % 8), (0, 0)))
    mod = _ada_call(c_pad, w_ada, b_ada)
    def per_chunk(m, reps):
        return jnp.broadcast_to(m[:, :, None, :], m.shape[:2] + (reps, m.shape[2])).reshape(depth, -1, m.shape[2])

    mod_rows = jnp.concatenate([per_chunk(mod[:, :bp], tp // ch), per_chunk(mod[:, bp:n_seq], ts // ch)],
                               axis=1)

    pos = jnp.concatenate([jnp.arange(tp, dtype=F32),
                           jnp.tile(past + jnp.arange(ts, dtype=F32), tm // ts)])
    tabs = _rope_tables(pos)

    tq_p = min(256, tp)
    tk_p = min(512, tp)
    nq_p = tp // tq_p
    tq_c = min(512, tp)
    l_s = past + ts
    tk_s = 384
    lp_s = -(-l_s // tk_s) * tk_s
    n_rows_pad = -(-(n * TOP_K + N_EXPERTS * (EXPERT_ROWS - 1)) // EXPERT_ROWS) * EXPERT_ROWS

    def with_cache(cache_l, new, width):
        parts = [cache_l.reshape(bs, past, width).astype(BF16), new.reshape(bs, ts, width)]
        if lp_s > l_s:
            parts.append(jnp.zeros((bs, lp_s - l_s, width), BF16))
        return jnp.concatenate(parts, axis=1).reshape(bs * lp_s, width)

    caches = [[] for _ in range(7)]
    y = None
    for l in range(depth):
        lam_init = 0.8 - 0.6 * math.exp(-0.3 * l)
        wl = _layer_weights(l, w_in, mla_w_uq, mla_w_ukv, w_br_c, router_w, router_b, exp_w_gu, exp_b_gu)
        m6 = [mod_rows[l, :, j * D_MODEL:(j + 1) * D_MODEL] for j in range(6)]
        sh_a, sc_a, gt_a, sh_m, sc_m, gt_m = m6
        nmix = norm_mix[l].reshape(1, D_MODEL)
        nffn = norm_ffn[l].reshape(1, D_MODEL)

        (ak, av, bk, bv, bik, clat, ckr, misc,
         aq_b, ak_b, av_b, bq_b, bk_b, bv_b, biq_b, bik2_b, cq_b, ck_b, cv_b) = _ka_call(
            x, sh_a, sc_a, tabs, nmix, wl['w1'], mla_q_norm[l].reshape(1, Q_LORA), wl['wuq'],
            mla_kv_norm[l].reshape(1, KV_LORA), wl['wk'], wl['we'], wl['wv'],
            n_prompt_blocks=n_p // tm, tab_blocks=tp // tm)
        for i, a in enumerate((ak, av, bk, bv, bik, clat, ckr)):
            caches[i].append(a)

        lqk = jnp.pad(jnp.stack([diff_lq1[l], diff_lk1[l], diff_lq2[l], diff_lk2[l]]),
                      ((0, 4), (0, LANES - A_HD)))
        sub = diff_subln[l].reshape(1, 2 * A_HD)
        oa_p = _diff_call(lqk, sub, aq_b, ak_b, av_b, nb=bp, nq=nq_p, tq=tq_p, tk=tk_p, lp=tp,
                          q_blk0=0, k_blk0=0, q_pos0=0, lam_init=lam_init)
        ak_s = with_cache(cache_a_k[l], ak_b[n_p:], A_WIDTH)
        av_s = with_cache(cache_a_v[l], av_b[n_p:], A_WIDTH)
        oa_s = _diff_call(lqk, sub, aq_b, ak_s, av_s, nb=bs, nq=1, tq=ts, tk=tk_s, lp=lp_s,
                          q_blk0=n_p // ts, k_blk0=0, q_pos0=past, lam_init=lam_init, heads=A_HEADS)
        ob_p = _dsa_call(bq_b, biq_b, misc, bk_b, bv_b, bik2_b, nb=bp, nq=nq_p, tq=tq_p, tk=tk_p, lp=tp,
                         q_blk0=0, k_blk0=0, q_pos0=0, n_keys=tp, n_bisect=12)
        bk_s = with_cache(cache_b_k[l], bk_b[n_p:], B_WIDTH)
        bv_s = with_cache(cache_b_v[l], bv_b[n_p:], B_WIDTH)
        cik = cache_b_idx_k[l].reshape(bs, past, IDX_DIM)
        bik_s = with_cache(jnp.concatenate([cik, cik], axis=-1), bik2_b[n_p:], LANES)
        ob_s = _dsa_call(bq_b, biq_b, misc, bk_s, bv_s, bik_s, nb=bs, nq=1, tq=ts, tk=tk_s, lp=lp_s,
                         q_blk0=n_p // ts, k_blk0=0, q_pos0=past, n_keys=l_s, n_bisect=10)
        oc_p = _mla_call(cq_b, ck_b, cv_b, nb=bp, nq=tp // tq_c, tq=tq_c, tk=tk_p, lp=tp,
                         q_blk0=0, k_blk0=0, q_pos0=0)
        lat_c = cache_c_latent[l].reshape(bs * past, KV_LORA)
        kr_c = jnp.pad(cache_c_k_rope[l].reshape(bs * past, C_ROPE), ((0, 0), (0, LANES - C_ROPE)))
        ck_c, cv_c = _mla_kv_call(lat_c, kr_c, wl['wk'], wl['we'], wl['wv'])
        ck_s = with_cache(ck_c, ck_b[n_p:], 512)
        cv_s = with_cache(cv_c, cv_b[n_p:], 512)
        oc_s = _mla_call(cq_b, ck_s, cv_s, nb=bs, nq=1, tq=ts, tk=tk_s, lp=lp_s,
                         q_blk0=n_p // ts, k_blk0=0, q_pos0=past, heads=C_HEADS)

        oa = jnp.concatenate([oa_p, oa_s], axis=0)
        ob = jnp.concatenate([ob_p, ob_s], axis=0)
        oc = jnp.concatenate([oc_p, oc_s], axis=0)

        x1, h2, e_pad, g_pad, cnt = _kb_call(
            x, oa, ob, oc, (sh_a, sc_a, gt_a, sh_m, sc_m), nmix, nffn, wl['wg'],
            w_br_a[l].astype(BF16), w_br_b[l].astype(BF16), wl['wbc'], w_out[l].astype(BF16),
            wl['rw'], wl['rb'])

        dest, row_tok, blk_e, n_used = _route(e_pad, cnt, n_rows_pad)
        xg = jnp.take(h2, row_tok, axis=0)
        yr = _ke_call(blk_e, n_used, xg, wl['e_wg'], wl['e_wl'], exp_w_down[l].astype(BF16),
                      wl['e_bg'], wl['e_bl'], exp_b_down[l].reshape(N_EXPERTS, 1, D_MODEL))
        dest_blocks = dest.reshape(n // tm, tm, TOP_K).transpose(0, 2, 1).reshape(n // tm, 1, TOP_K * tm)
        x, y = _kc_call(x1, yr, dest_blocks, g_pad, gt_m, final_norm.reshape(1, D_MODEL))

    def split(a, tail):
        a = jnp.stack(a, axis=0)
        return (a[:, :n_p].reshape((depth, bp, tp) + tail), a[:, n_p:].reshape((depth, bs, ts) + tail))

    tails = ((A_HEADS, 2 * A_HD), (A_HEADS, 2 * A_HD), (B_HEADS, B_HD), (B_HEADS, B_HD),
             (IDX_DIM,), (KV_LORA,), (C_ROPE,))
    ps = [split(c, t) for c, t in zip(caches, tails)]
    y_prompt = y[:n_p].reshape(bp, tp, D_MODEL)
    y_sample = y[n_p:].reshape(bs, ts, D_MODEL)
    return (y_prompt, y_sample) + tuple(p[0] for p in ps) + tuple(p[1] for p in ps)
```
